```python
import math
import jax
import jax.numpy as jnp
from jax import lax
import numpy as np

D_MODEL = 2048
BATCH = 4
SEQ = 4096
DEPTH = 2

GRID_W = 64
CTX_LEN = 256
D_MIX = D_MODEL
EPS = 1e-6
NEG_INF = -1e30

GLA_HEADS = 4
GLA_DV = D_MIX // 4 // GLA_HEADS
GLA_DK = GLA_DV // 2
GLA_WIDTH = GLA_HEADS * GLA_DV
GLA_GATE_RANK = 16
GLA_GATE_NORM = 16.0
GLA_CHUNK = 64

S5_WIDTH = D_MIX // 4
S5_GROUP = 16
S5_GROUPS = S5_WIDTH // S5_GROUP
S5_STATE = 64
S5_DT_MIN = 1e-3
S5_DT_MAX = 1e-1

SWA_HEAD_DIM = 128
SWA_WIDTH = D_MIX - GLA_WIDTH - S5_WIDTH
SWA_HEADS = SWA_WIDTH // SWA_HEAD_DIM
SWA_KV_HEADS = 2
SWA_GROUP = SWA_HEADS // SWA_KV_HEADS
SWA_WINDOW = 128
SWA_BLOCK = SWA_WINDOW
ROPE_PAIRS = SWA_HEAD_DIM // 4
ROPE_BASE = 10000.0

N_EXPERTS = 16
N_EXPERT_GROUPS = 4
EXPERTS_PER_GROUP = N_EXPERTS // N_EXPERT_GROUPS
TOP_K = 2
D_EXPERT = 1024

IN_WIDTHS = (GLA_HEADS * GLA_DK, GLA_HEADS * GLA_DK, GLA_WIDTH, 2 * GLA_GATE_RANK, GLA_WIDTH,
             S5_WIDTH, SWA_WIDTH, SWA_KV_HEADS * SWA_HEAD_DIM, SWA_KV_HEADS * SWA_HEAD_DIM)
IN_COLS = sum(IN_WIDTHS)

kernel_name = 'hybrid_gla_s5_swa_moe_dit'

F32 = jnp.float32


def rms_norm(x, gain):
    xf = x.astype(F32)
    y = xf * lax.rsqrt(jnp.mean(xf * xf, axis=-1, keepdims=True) + EPS)
    return (y * gain.astype(F32)).astype(x.dtype)


def modulate(h, shift, scale):
    return h * (1 + scale[:, None, :]) + shift[:, None, :]


def split_columns(p):
    out, start = [], 0
    for w in IN_WIDTHS:
        out.append(p[..., start:start + w])
        start += w
    return out


def gla_scan(q, k, v, log_a, h0, with_output):
    bsz, nh, length, dk = q.shape
    dv = v.shape[-1]
    nc = length // GLA_CHUNK
    q = q.reshape(bsz, nh, nc, GLA_CHUNK, dk)
    k = k.reshape(bsz, nh, nc, GLA_CHUNK, dk)
    v = v.reshape(bsz, nh, nc, GLA_CHUNK, dv)
    b = jnp.cumsum(log_a.reshape(bsz, nh, nc, GLA_CHUNK, dk), axis=3)
    b_last = b[:, :, :, -1:, :]
    d_state = jnp.einsum('bhncd,bhncv->bhndv', k * jnp.exp(b_last - b), v)
    decay = jnp.exp(b_last[:, :, :, 0, :])

    def step(state, inp):
        dec, ds = inp
        return dec[..., None] * state + ds, state

    s_final, s_prev = lax.scan(step, h0, (jnp.moveaxis(decay, 2, 0), jnp.moveaxis(d_state, 2, 0)))
    if not with_output:
        return None, s_final
    s_prev = jnp.moveaxis(s_prev, 0, 2)
    q_dec = q * jnp.exp(b)
    k_dec = k * jnp.exp(-b)
    causal = jnp.tril(jnp.ones((GLA_CHUNK, GLA_CHUNK), dtype=bool))
    att = jnp.where(causal, jnp.einsum('bhncd,bhnsd->bhncs', q_dec, k_dec), 0.0)
    o = jnp.einsum('bhncs,bhnsv->bhncv', att, v) + jnp.einsum('bhncd,bhndv->bhncv', q_dec, s_prev)
    return o.reshape(bsz, nh, length, dv), s_final


def gla_prepare(q, k, v, lr, w_alpha, b_alpha):
    bsz, length = q.shape[:2]

    def heads(t, d):
        return t.astype(F32).reshape(bsz, length, GLA_HEADS, d).transpose(0, 2, 1, 3)

    lr = lr.astype(F32)
    log_a = [heads(jax.nn.log_sigmoid(lr[..., i * GLA_GATE_RANK:(i + 1) * GLA_GATE_RANK] @ w_alpha[i].astype(F32)
                                      + b_alpha[i].astype(F32)) / GLA_GATE_NORM, GLA_DK) for i in range(2)]
    return heads(q, GLA_DK) * GLA_DK ** -0.5, heads(k, GLA_DK), heads(v, GLA_DV), log_a


def gla_output(o, r, gain):
    bsz, _, length, _ = o.shape
    o = o * lax.rsqrt(jnp.mean(o * o, axis=-1, keepdims=True) + EPS)
    o = o.transpose(0, 2, 1, 3).reshape(bsz, length, GLA_WIDTH) * gain.astype(F32)
    return (o * jax.nn.silu(r.astype(F32))).astype(r.dtype)


def gla_mixer(lat, ctx, w_alpha, b_alpha, o_norm, ctx_out):
    ql, kl, vl, al = gla_prepare(lat[0], lat[1], lat[2], lat[3], w_alpha, b_alpha)
    qc, kc, vc, ac = gla_prepare(ctx[0], ctx[1], ctx[2], ctx[3], w_alpha, b_alpha)
    zero = jnp.zeros((ql.shape[0], GLA_HEADS, GLA_DK, GLA_DV), F32)

    def flip(t):
        return jnp.flip(t, axis=2)

    oc_f, sc_f = gla_scan(qc, kc, vc, ac[0], zero, ctx_out)
    oc_b, sc_b = gla_scan(flip(qc), flip(kc), flip(vc), flip(ac[1]), zero, ctx_out)
    ol_f, _ = gla_scan(ql, kl, vl, al[0], sc_f, True)
    ol_b, _ = gla_scan(flip(ql), flip(kl), flip(vl), flip(al[1]), sc_b, True)
    y_lat = gla_output(ol_f + flip(ol_b), lat[4], o_norm)
    y_ctx = gla_output(oc_f + flip(oc_b), ctx[4], o_norm) if ctx_out else None
    return y_lat, y_ctx


def s5_discretize(a_re, a_im, log_dt, b_re, b_im):
    a_re, a_im, b_re, b_im = a_re.astype(F32), a_im.astype(F32), b_re.astype(F32), b_im.astype(F32)
    dt = jnp.exp(log_dt.astype(F32))[:, None]
    mag = jnp.exp(dt * a_re)
    ab_re, ab_im = mag * jnp.cos(dt * a_im), mag * jnp.sin(dt * a_im)
    den = a_re * a_re + a_im * a_im
    nr = ab_re - 1.0
    coef_re = (nr * a_re + ab_im * a_im) / den
    coef_im = (ab_im * a_re - nr * a_im) / den
    bb_re = coef_re[..., None] * b_re - coef_im[..., None] * b_im
    bb_im = coef_re[..., None] * b_im + coef_im[..., None] * b_re
    return ab_re, ab_im, bb_re, bb_im


def ssm_combine(e1, e2):
    a1r, a1i, b1r, b1i = e1
    a2r, a2i, b2r, b2i = e2
    return (a2r * a1r - a2i * a1i, a2r * a1i + a2i * a1r,
            a2r * b1r - a2i * b1i + b2r, a2r * b1i + a2i * b1r + b2i)


def s5_scan(u, ab_re, ab_im, bb_re, bb_im, h0, reverse):
    x_re = jnp.einsum('blgc,gnc->blgn', u, bb_re)
    x_im = jnp.einsum('blgc,gnc->blgn', u, bb_im)
    if h0 is not None:
        edge = -1 if reverse else 0
        x_re = x_re.at[:, edge].add(ab_re * h0[0] - ab_im * h0[1])
        x_im = x_im.at[:, edge].add(ab_re * h0[1] + ab_im * h0[0])
    a_re = jnp.broadcast_to(ab_re, x_re.shape)
    a_im = jnp.broadcast_to(ab_im, x_im.shape)
    _, _, h_re, h_im = lax.associative_scan(ssm_combine, (a_re, a_im, x_re, x_im), reverse=reverse, axis=1)
    return h_re, h_im


def s5_readout(h_re, h_im, c_re, c_im):
    return (jnp.einsum('blgn,gcn->blgc', h_re, c_re.astype(F32))
            - jnp.einsum('blgn,gcn->blgc', h_im, c_im.astype(F32)))


def s5_glu(y, w_glu, b_glu, dtype):
    z = jax.nn.gelu(y.reshape(y.shape[0], y.shape[1], S5_WIDTH))
    return (z * jax.nn.sigmoid(z @ w_glu.astype(F32) + b_glu.astype(F32))).astype(dtype)


def s5_mixer(u_lat, u_ctx, a_re, a_im, log_dt, b_re, b_im, c_re, c_im, d, w_glu, b_glu, ctx_out):
    def grouped(u):
        return u.astype(F32).reshape(u.shape[0], u.shape[1], S5_GROUPS, S5_GROUP)

    ul, uc = grouped(u_lat), grouped(u_ctx)
    d_g = d.astype(F32).reshape(S5_GROUPS, S5_GROUP)
    y_lat = ul * d_g
    y_ctx = uc * d_g if ctx_out else None
    for direction, reverse in ((0, False), (1, True)):
        ab_re, ab_im, bb_re, bb_im = s5_discretize(a_re[direction], a_im[direction], log_dt[direction],
                                                   b_re[direction], b_im[direction])
        hc_re, hc_im = s5_scan(uc, ab_re, ab_im, bb_re, bb_im, None, reverse)
        edge = 0 if reverse else -1
        h_re, h_im = s5_scan(ul, ab_re, ab_im, bb_re, bb_im, (hc_re[:, edge], hc_im[:, edge]), reverse)
        y_lat = y_lat + s5_readout(h_re, h_im, c_re[direction], c_im[direction])
        if ctx_out:
            y_ctx = y_ctx + s5_readout(hc_re, hc_im, c_re[direction], c_im[direction])
    out_lat = s5_glu(y_lat, w_glu, b_glu, u_lat.dtype)
    out_ctx = s5_glu(y_ctx, w_glu, b_glu, u_ctx.dtype) if ctx_out else None
    return out_lat, out_ctx


def axial_rope_tables(rows):
    row = jnp.repeat(jnp.arange(rows, dtype=F32), GRID_W)
    col = jnp.tile(jnp.arange(GRID_W, dtype=F32), rows)
    inv_freq = ROPE_BASE ** (-jnp.arange(ROPE_PAIRS, dtype=F32) / ROPE_PAIRS)
    ang = jnp.stack([row[:, None] * inv_freq, col[:, None] * inv_freq], axis=1)
    return jnp.cos(ang), jnp.sin(ang)


def apply_axial_rope(x, cos, sin):
    bsz, length, nh, hd = x.shape
    xr = x.reshape(bsz, length, nh, 2, 2, ROPE_PAIRS)
    x1, x2 = xr[..., 0, :], xr[..., 1, :]
    c = cos.astype(x.dtype)[None, :, None]
    s = sin.astype(x.dtype)[None, :, None]
    return jnp.stack([x1 * c - x2 * s, x2 * c + x1 * s], axis=-2).reshape(bsz, length, nh, hd)


def swa_project(q, k, v, q_gain, k_gain):
    bsz, length = q.shape[:2]
    q = rms_norm(q.reshape(bsz, length, SWA_HEADS, SWA_HEAD_DIM), q_gain)
    k = rms_norm(k.reshape(bsz, length, SWA_KV_HEADS, SWA_HEAD_DIM), k_gain)
    return q, k, v.reshape(bsz, length, SWA_KV_HEADS, SWA_HEAD_DIM)


def swa_mixer(q, k, v, qc, kc, vc, sink, ctx_out):
    bsz, length = q.shape[:2]
    n_ctx = kc.shape[1]
    nb = length // SWA_BLOCK
    scale = SWA_HEAD_DIM ** -0.5
    sink = sink.astype(F32).reshape(SWA_KV_HEADS, SWA_GROUP)
    qb = q.reshape(bsz, nb, SWA_BLOCK, SWA_KV_HEADS, SWA_GROUP, SWA_HEAD_DIM)

    def band(t):
        tp = jnp.pad(t, ((0, 0), (SWA_WINDOW, SWA_WINDOW), (0, 0), (0, 0)))
        tp = tp.reshape(bsz, nb + 2, SWA_BLOCK, SWA_KV_HEADS, SWA_HEAD_DIM)
        return jnp.concatenate([tp[:, :-2], tp[:, 1:-1], tp[:, 2:]], axis=2)

    kw, vw = band(k), band(v)
    qpos = jnp.arange(nb)[:, None] * SWA_BLOCK + jnp.arange(SWA_BLOCK)[None]
    kpos = jnp.arange(nb)[:, None] * SWA_BLOCK - SWA_WINDOW + jnp.arange(3 * SWA_BLOCK)[None]
    valid = ((kpos[:, None, :] >= 0) & (kpos[:, None, :] < length)
             & (jnp.abs(qpos[:, :, None] - kpos[:, None, :]) <= SWA_WINDOW))
    s_win = jnp.einsum('bnqhgd,bnkhd->bhgnqk', qb, kw).astype(F32) * scale
    s_win = jnp.where(valid, s_win, NEG_INF)
    s_ctx = jnp.einsum('bnqhgd,bkhd->bhgnqk', qb, kc).astype(F32) * scale
    s_sink = jnp.broadcast_to(sink[None, :, :, None, None, None], s_win.shape[:-1] + (1,))
    p = jax.nn.softmax(jnp.concatenate([s_win, s_ctx, s_sink], axis=-1), axis=-1)
    nw = 3 * SWA_BLOCK
    out = (jnp.einsum('bhgnqk,bnkhd->bnqhgd', p[..., :nw].astype(v.dtype), vw)
           + jnp.einsum('bhgnqk,bkhd->bnqhgd', p[..., nw:nw + n_ctx].astype(v.dtype), vc))
    y_lat = out.reshape(bsz, length, SWA_WIDTH)
    if not ctx_out:
        return y_lat, None
    qcb = qc.reshape(bsz, n_ctx, SWA_KV_HEADS, SWA_GROUP, SWA_HEAD_DIM)
    sc = jnp.einsum('bqhgd,bkhd->bhgq k'.replace(' ', ''), qcb, kc).astype(F32) * scale
    sc_sink = jnp.broadcast_to(sink[None, :, :, None, None], sc.shape[:-1] + (1,))
    pc = jax.nn.softmax(jnp.concatenate([sc, sc_sink], axis=-1), axis=-1)
    yc = jnp.einsum('bhgqk,bkhd->bqhgd', pc[..., :n_ctx].astype(vc.dtype), vc)
    return y_lat, yc.reshape(bsz, n_ctx, SWA_WIDTH)


def group_limited_top2(tokens, router_w, router_b):
    aff = jax.nn.sigmoid(tokens.astype(F32) @ router_w.astype(F32))
    biased = aff + router_b.astype(F32)
    grouped = biased.reshape(-1, N_EXPERT_GROUPS, EXPERTS_PER_GROUP)
    group_score = jnp.sum(lax.top_k(grouped, TOP_K)[0], axis=-1)
    best = jnp.argmax(group_score, axis=-1)
    in_group = (jnp.arange(N_EXPERTS) // EXPERTS_PER_GROUP)[None, :] == best[:, None]
    _, idx = lax.top_k(jnp.where(in_group, biased, -jnp.inf), TOP_K)
    sel = jnp.take_along_axis(aff, idx, axis=-1)
    weights = sel / jnp.sum(sel, axis=-1, keepdims=True)
    return jnp.sum(jax.nn.one_hot(idx, N_EXPERTS, dtype=F32) * weights[..., None], axis=1)


def moe_ffn(tokens, router_w, router_b, w_gate, w_up, w_down):
    gates = group_limited_top2(tokens, router_w, router_b).astype(tokens.dtype)
    out = jnp.zeros_like(tokens)
    for e in range(N_EXPERTS):
        act = jax.nn.silu(tokens @ w_gate[e]) * (tokens @ w_up[e])
        out = out + gates[:, e:e + 1] * (act @ w_down[e])
    return out


def setup_inputs(seed: int = 0) -> dict:
    key = jax.random.key(seed)
    ks = jax.random.split(key, 32)

    def nrm(k, shape, scale):
        return jax.random.normal(k, shape, F32) * scale

    L = DEPTH
    n_idx = jnp.arange(S5_STATE, dtype=F32)
    return {
        'x': nrm(ks[0], (BATCH, SEQ, D_MODEL), 1.0),
        'c': nrm(ks[1], (BATCH, D_MODEL), 1.0),
        'ctx': nrm(ks[2], (BATCH, CTX_LEN, D_MODEL), 1.0),
        'c_ctx': nrm(ks[3], (D_MODEL,), 1.0),
        'ada_w': nrm(ks[4], (L, D_MODEL, 6 * D_MODEL), 0.5 * D_MODEL ** -0.5),
        'ada_b': nrm(ks[5], (L, 6 * D_MODEL), 0.01),
        'norm1': 1.0 + nrm(ks[6], (L, D_MODEL), 0.02),
        'norm2': 1.0 + nrm(ks[7], (L, D_MODEL), 0.02),
        'w_in': nrm(ks[8], (L, D_MODEL, IN_COLS), D_MODEL ** -0.5),
        'gla_w_alpha': nrm(ks[9], (L, 2, GLA_GATE_RANK, GLA_HEADS * GLA_DK), GLA_GATE_RANK ** -0.5),
        'gla_b_alpha': nrm(ks[10], (L, 2, GLA_HEADS * GLA_DK), 0.1),
        'gla_o_norm': 1.0 + nrm(ks[11], (L, GLA_WIDTH), 0.02),
        's5_a_re': -0.5 + nrm(ks[12], (L, 2, S5_GROUPS, S5_STATE), 0.01),
        's5_a_im': math.pi * n_idx + nrm(ks[13], (L, 2, S5_GROUPS, S5_STATE), 0.01),
        's5_log_dt': jax.random.uniform(ks[14], (L, 2, S5_GROUPS), F32,
                                        minval=math.log(S5_DT_MIN), maxval=math.log(S5_DT_MAX)),
        's5_b_re': nrm(ks[15], (L, 2, S5_GROUPS, S5_STATE, S5_GROUP), (2 * S5_GROUP) ** -0.5),
        's5_b_im': nrm(ks[16], (L, 2, S5_GROUPS, S5_STATE, S5_GROUP), (2 * S5_GROUP) ** -0.5),
        's5_c_re': nrm(ks[17], (L, 2, S5_GROUPS, S5_GROUP, S5_STATE), (2 * S5_STATE) ** -0.5),
        's5_c_im': nrm(ks[18], (L, 2, S5_GROUPS, S5_GROUP, S5_STATE), (2 * S5_STATE) ** -0.5),
        's5_d': nrm(ks[19], (L, S5_WIDTH), 1.0),
        's5_w_glu': nrm(ks[20], (L, S5_WIDTH, S5_WIDTH), S5_WIDTH ** -0.5),
        's5_b_glu': nrm(ks[21], (L, S5_WIDTH), 0.01),
        'swa_q_norm': 1.0 + nrm(ks[22], (L, SWA_HEAD_DIM), 0.02),
        'swa_k_norm': 1.0 + nrm(ks[23], (L, SWA_HEAD_DIM), 0.02),
        'swa_sink': nrm(ks[24], (L, SWA_HEADS), 1.0),
        'w_out': nrm(ks[25], (L, D_MIX, D_MODEL), D_MIX ** -0.5),
        'router_w': nrm(ks[26], (D_MODEL, N_EXPERTS), D_MODEL ** -0.5),
        'router_b': nrm(ks[27], (N_EXPERTS,), 0.01),
        'exp_w_gate': nrm(ks[28], (L, N_EXPERTS, D_MODEL, D_EXPERT), D_MODEL ** -0.5),
        'exp_w_up': nrm(ks[29], (L, N_EXPERTS, D_MODEL, D_EXPERT), D_MODEL ** -0.5),
        'exp_w_down': nrm(ks[30], (L, N_EXPERTS, D_EXPERT, D_MODEL), D_EXPERT ** -0.5),
    }


def reference(x, c, ctx, c_ctx, ada_w, ada_b, norm1, norm2, w_in, gla_w_alpha, gla_b_alpha, gla_o_norm,
              s5_a_re, s5_a_im, s5_log_dt, s5_b_re, s5_b_im, s5_c_re, s5_c_im, s5_d, s5_w_glu, s5_b_glu,
              swa_q_norm, swa_k_norm, swa_sink, w_out, router_w, router_b, exp_w_gate, exp_w_up, exp_w_down):
    bsz, seq_len, d_model = x.shape
    rows = seq_len // GRID_W
    cos, sin = axial_rope_tables(rows)
    xc = ctx
    silu_c = jax.nn.silu(c)
    silu_cc = jax.nn.silu(c_ctx)[None]
    for layer in range(DEPTH):
        ctx_out = layer < DEPTH - 1
        mod = silu_c @ ada_w[layer] + ada_b[layer]
        mod_c = silu_cc @ ada_w[layer] + ada_b[layer]
        sh1, sc1, g1, sh2, sc2, g2 = jnp.split(mod, 6, axis=-1)
        csh1, csc1, cg1, csh2, csc2, cg2 = jnp.split(mod_c, 6, axis=-1)

        h = modulate(rms_norm(x, norm1[layer]), sh1, sc1)
        hc = modulate(rms_norm(xc, norm1[layer]), csh1, csc1)
        p = split_columns(h @ w_in[layer])
        pc = split_columns(hc @ w_in[layer])

        gla_y, gla_yc = gla_mixer(p[0:5], pc[0:5], gla_w_alpha[layer], gla_b_alpha[layer],
                                  gla_o_norm[layer], ctx_out)
        s5_y, s5_yc = s5_mixer(p[5], pc[5], s5_a_re[layer], s5_a_im[layer], s5_log_dt[layer],
                               s5_b_re[layer], s5_b_im[layer], s5_c_re[layer], s5_c_im[layer],
                               s5_d[layer], s5_w_glu[layer], s5_b_glu[layer], ctx_out)
        q, k, v = swa_project(p[6], p[7], p[8], swa_q_norm[layer], swa_k_norm[layer])
        q, k = apply_axial_rope(q, cos, sin), apply_axial_rope(k, cos, sin)
        qc, kc, vc = swa_project(pc[6], pc[7], pc[8], swa_q_norm[layer], swa_k_norm[layer])
        swa_y, swa_yc = swa_mixer(q, k, v, qc, kc, vc, swa_sink[layer], ctx_out)

        y = jnp.concatenate([gla_y, s5_y, swa_y], axis=-1) @ w_out[layer]
        x = x + g1[:, None, :] * y
        h2 = modulate(rms_norm(x, norm2[layer]), sh2, sc2)
        n_lat = bsz * seq_len
        if ctx_out:
            yc = jnp.concatenate([gla_yc, s5_yc, swa_yc], axis=-1) @ w_out[layer]
            xc = xc + cg1[:, None, :] * yc
            h2c = modulate(rms_norm(xc, norm2[layer]), csh2, csc2)
            tokens = jnp.concatenate([h2.reshape(n_lat, d_model), h2c.reshape(-1, d_model)], axis=0)
            f = moe_ffn(tokens, router_w, router_b, exp_w_gate[layer], exp_w_up[layer], exp_w_down[layer])
            x = x + g2[:, None, :] * f[:n_lat].reshape(x.shape)
            xc = xc + cg2[:, None, :] * f[n_lat:].reshape(xc.shape)
        else:
            f = moe_ffn(h2.reshape(n_lat, d_model), router_w, router_b,
                        exp_w_gate[layer], exp_w_up[layer], exp_w_down[layer])
            x = x + g2[:, None, :] * f.reshape(x.shape)
    return x
```

```python
import functools
import math

import jax
import jax.numpy as jnp
from jax import lax
from jax.experimental import pallas as pl
from jax.experimental.pallas import tpu as pltpu

F32 = jnp.float32
BF16 = jnp.bfloat16
I32 = jnp.int32

EPS = 1e-6
NEG_INF = -1e30

LANES = 128
VMEM_BYTES = 64 * 1024 * 1024

ROW_TILE = 256

GLA_HEADS = 4
GLA_DK = 64
GLA_DV = 128
GLA_RANK = 16
GLA_GATE_NORM = 16.0
GLA_CHUNK = 64

S5_GROUP = 16
S5_STATE = 64
S5_CHUNK = 8
S5_BLOCK_GROUPS = LANES // S5_GROUP

SWA_HEAD_DIM = 128
SWA_HEADS = 8
SWA_KV_HEADS = 2
SWA_GROUP = SWA_HEADS // SWA_KV_HEADS
SWA_BLOCK = 128
ROPE_PAIRS = 32
ROPE_BASE = 10000.0
GRID_W = 64

N_EXPERTS = 16
N_EXPERT_GROUPS = 4
EXPERTS_PER_GROUP = 4
MOE_TILE = 256


def _cparams(semantics, vmem_mb):
    return pltpu.CompilerParams(dimension_semantics=semantics,
                                vmem_limit_bytes=vmem_mb * 1024 * 1024)


def _mm(a, b):
    return jnp.dot(a.astype(BF16), b.astype(BF16), preferred_element_type=F32)


def _mm_nt(a, b):
    return lax.dot_general(a.astype(BF16), b.astype(BF16), (((1,), (1,)), ((), ())),
                           preferred_element_type=F32)


def _mm_tn(a, b):
    return lax.dot_general(a.astype(BF16), b.astype(BF16), (((0,), (0,)), ((), ())),
                           preferred_element_type=F32)


def _split3(a):
    hi = a.astype(BF16)
    r1 = a - hi.astype(F32)
    mid = r1.astype(BF16)
    lo = (r1 - mid.astype(F32)).astype(BF16)
    return hi, mid, lo


def _mm_f32(a, b, dims=(((1,), (0,)), ((), ()))):
    a3, b3 = _split3(a), _split3(b)
    out = None
    for i in range(3):
        for j in range(3 - i):
            t = lax.dot_general(a3[i], b3[j], dims, preferred_element_type=F32)
            out = t if out is None else out + t
    return out


def _sigmoid(x):
    return 1.0 / (1.0 + jnp.exp(-x))


def _silu(x):
    return x * _sigmoid(x)


def _log_sigmoid(x):
    return jnp.minimum(x, 0.0) - jnp.log(1.0 + jnp.exp(-jnp.abs(x)))


def _gelu_tanh(x):
    return 0.5 * x * (1.0 + jnp.tanh(math.sqrt(2.0 / math.pi) * (x + 0.044715 * (x * x * x))))


def _rms(x, gain):
    return x * lax.rsqrt(jnp.mean(x * x, axis=-1, keepdims=True) + EPS) * gain


def _ada_body(c_ref, w_ref, b_ref, o_ref):
    s = _silu(c_ref[...])
    o_ref[...] = _mm(s, w_ref[...]) + b_ref[...]


def _ada_mod(cvec, w, b):
    rows, d = cvec.shape
    n = w.shape[1]
    tn = 1024
    return pl.pallas_call(
        _ada_body,
        grid=(n // tn,),
        in_specs=[pl.BlockSpec((rows, d), lambda i: (0, 0)),
                  pl.BlockSpec((d, tn), lambda i: (0, i)),
                  pl.BlockSpec((1, tn), lambda i: (0, i))],
        out_specs=pl.BlockSpec((rows, tn), lambda i: (0, i)),
        out_shape=jax.ShapeDtypeStruct((rows, n), F32),
        compiler_params=_cparams(("arbitrary",), 40),
        name="ada_mod",
    )(cvec, w, b.reshape(1, n))


def _mod_row(i, tiles_per_batch, ctx_tiles, n_batch):
    return jnp.where(i % tiles_per_batch < ctx_tiles, n_batch, i // tiles_per_batch)


def _proj_body(x_ref, sh_ref, sc_ref, g_ref, w_ref, *out_refs, splits):
    y = _rms(x_ref[...], g_ref[...])
    h = (y * (1.0 + sc_ref[0]) + sh_ref[0]).astype(BF16)
    for o_ref, (c0, width) in zip(out_refs, splits):
        o_ref[...] = jnp.dot(h, w_ref[:, c0:c0 + width],
                             preferred_element_type=F32).astype(o_ref.dtype)


def _in_proj(xall, mod3, gain, w, outs, tiles_per_batch, ctx_tiles, n_batch):
    rows, d = xall.shape
    tm = ROW_TILE
    mrow = functools.partial(_mod_row, tiles_per_batch=tiles_per_batch, ctx_tiles=ctx_tiles,
                             n_batch=n_batch)
    splits, out_specs, out_shapes = [], [], []
    c0 = 0
    for width, dtype in outs:
        splits.append((c0, width))
        out_specs.append(pl.BlockSpec((tm, width), lambda i: (i, 0)))
        out_shapes.append(jax.ShapeDtypeStruct((rows, width), dtype))
        c0 += width
    return pl.pallas_call(
        functools.partial(_proj_body, splits=tuple(splits)),
        grid=(rows // tm,),
        in_specs=[pl.BlockSpec((tm, d), lambda i: (i, 0)),
                  pl.BlockSpec((1, 1, d), lambda i: (mrow(i) * 6 + 0, 0, 0)),
                  pl.BlockSpec((1, 1, d), lambda i: (mrow(i) * 6 + 1, 0, 0)),
                  pl.BlockSpec((1, d), lambda i: (0, 0)),
                  pl.BlockSpec(w.shape, lambda i: (0, 0), pipeline_mode=pl.Buffered(1))],
        out_specs=out_specs,
        out_shape=out_shapes,
        compiler_params=_cparams(("arbitrary",), 48),
        name="in_proj",
    )(xall, mod3, mod3, gain.reshape(1, d), w)


def _gla_body(qkf, vf, lrf, qkb, vb, lrb, wa_ref, ba_ref, of_ref, ob_ref, sf_ref, sb_ref):
    t = pl.program_id(1)
    ck = GLA_CHUNK
    hk = GLA_HEADS * GLA_DK

    @pl.when(t == 0)
    def _init():
        sf_ref[...] = jnp.zeros_like(sf_ref)
        sb_ref[...] = jnp.zeros_like(sb_ref)

    row = lax.broadcasted_iota(I32, (ck, ck), 0)
    col = lax.broadcasted_iota(I32, (ck, ck), 1)
    row4 = lax.broadcasted_iota(I32, (GLA_HEADS * ck, ck), 0) % ck
    col4 = lax.broadcasted_iota(I32, (GLA_HEADS * ck, ck), 1)
    lane = lax.broadcasted_iota(I32, (1, hk), 1)
    head_masks = [(lane // GLA_DK == h).astype(F32) for h in range(GLA_HEADS)]
    ones_cv = jnp.ones((ck, GLA_DV), F32)

    def run(qk_ref, v_ref, lr_ref, w, bias, s_ref, o_ref, reverse):
        tri = (row <= col if reverse else row >= col).astype(F32)
        amask = row4 <= col4 if reverse else row4 >= col4
        state = s_ref[...]
        n_chunks = qk_ref.shape[0] // ck
        order = range(n_chunks - 1, -1, -1) if reverse else range(n_chunks)
        for ci in order:
            rows = slice(ci * ck, (ci + 1) * ck)
            q = qk_ref[rows, 0:hk].astype(F32)
            k = qk_ref[rows, hk:2 * hk].astype(F32)
            v = v_ref[rows, :]
            la = _log_sigmoid(_mm_f32(lr_ref[rows, :], w) + bias) * (1.0 / GLA_GATE_NORM)
            b = _mm_f32(tri, la)
            b_last = b[0:1, :] if reverse else b[ck - 1:ck, :]
            qd = q * (GLA_DK ** -0.5) * jnp.exp(b)
            kd = k * jnp.exp(-b)
            ke = k * jnp.exp(b_last - b)
            qs = jnp.concatenate([qd * m for m in head_masks], axis=0).astype(BF16)
            att = jnp.where(amask, _mm_nt(qs, kd), 0.0)
            inter = _mm(qs, state)
            outs = []
            for h in range(GLA_HEADS):
                hr = slice(h * ck, (h + 1) * ck)
                outs.append(_mm(att[hr, :], v[:, h * GLA_DV:(h + 1) * GLA_DV]) + inter[hr, :])
            o_ref[rows, :] = jnp.concatenate(outs, axis=1)
            ds = _mm_tn(ke, v)
            ds = jnp.concatenate([ds[h * GLA_DK:(h + 1) * GLA_DK, h * GLA_DV:(h + 1) * GLA_DV]
                                  for h in range(GLA_HEADS)], axis=0)
            decay = jnp.exp(_mm_f32(la, ones_cv, dims=(((0,), (0,)), ((), ()))))
            state = decay * state + ds
        s_ref[...] = state

    run(qkf, vf, lrf, wa_ref[:, 0:hk], ba_ref[:, 0:hk], sf_ref, of_ref, False)
    run(qkb, vb, lrb, wa_ref[:, hk:2 * hk], ba_ref[:, hk:2 * hk], sb_ref, ob_ref, True)


def _gla_scan(qk, v, lr, wa, ba, n_batch, tiles_per_batch, ctx_tiles):
    rows = qk.shape[0]
    tm = ROW_TILE
    tpb = tiles_per_batch

    def fwd(b, t):
        return (b * tpb + t, 0)

    def bwd(b, t):
        return (b * tpb + jnp.where(t < ctx_tiles, ctx_tiles - 1 - t, tpb - 1 - (t - ctx_tiles)), 0)

    wv = v.shape[1]
    return pl.pallas_call(
        _gla_body,
        grid=(n_batch, tpb),
        in_specs=[pl.BlockSpec((tm, qk.shape[1]), fwd), pl.BlockSpec((tm, wv), fwd),
                  pl.BlockSpec((tm, lr.shape[1]), fwd),
                  pl.BlockSpec((tm, qk.shape[1]), bwd), pl.BlockSpec((tm, wv), bwd),
                  pl.BlockSpec((tm, lr.shape[1]), bwd),
                  pl.BlockSpec(wa.shape, lambda b, t: (0, 0)),
                  pl.BlockSpec(ba.shape, lambda b, t: (0, 0))],
        out_specs=[pl.BlockSpec((tm, wv), fwd), pl.BlockSpec((tm, wv), bwd)],
        out_shape=[jax.ShapeDtypeStruct((rows, wv), F32)] * 2,
        scratch_shapes=[pltpu.VMEM((GLA_HEADS * GLA_DK, GLA_DV), F32)] * 2,
        compiler_params=_cparams(("arbitrary", "arbitrary"), 32),
        name="gla_scan",
    )(qk, v, lr, qk, v, lr, wa, ba)


def _s5_prep(a_re, a_im, log_dt, b_re, b_im, c_re, c_im, d):
    hp = lax.Precision.HIGHEST
    tt = S5_CHUNK
    n_groups = a_re.shape[1]
    bg = S5_BLOCK_GROUPS
    nb = n_groups // bg
    a_re, a_im, b_re, b_im = a_re.astype(F32), a_im.astype(F32), b_re.astype(F32), b_im.astype(F32)
    c_re, c_im = c_re.astype(F32), c_im.astype(F32)
    dt = jnp.exp(log_dt.astype(F32))[..., None]
    mag = jnp.exp(dt * a_re)
    ab_re, ab_im = mag * jnp.cos(dt * a_im), mag * jnp.sin(dt * a_im)
    den = a_re * a_re + a_im * a_im
    nr = ab_re - 1.0
    coef_re = (nr * a_re + ab_im * a_im) / den
    coef_im = (ab_im * a_re - nr * a_im) / den
    bb_re = coef_re[..., None] * b_re - coef_im[..., None] * b_im
    bb_im = coef_re[..., None] * b_im + coef_im[..., None] * b_re
    pr, pi = [jnp.ones_like(ab_re)], [jnp.zeros_like(ab_im)]
    for _ in range(tt):
        pr, pi = pr + [pr[-1] * ab_re - pi[-1] * ab_im], pi + [pr[-1] * ab_im + pi[-1] * ab_re]
    pr, pi = jnp.stack(pr), jnp.stack(pi)
    cpr = c_re[None] * pr[:, :, :, None, :] - c_im[None] * pi[:, :, :, None, :]
    cpi = c_re[None] * pi[:, :, :, None, :] + c_im[None] * pr[:, :, :, None, :]
    kern = (jnp.einsum('kdgcn,dgnm->kdgcm', cpr, bb_re, precision=hp)
            - jnp.einsum('kdgcn,dgnm->kdgcm', cpi, bb_im, precision=hp))
    eye_g = jnp.eye(bg, dtype=F32)
    s_idx = jnp.arange(tt)[:, None]
    t_idx = jnp.arange(tt)[None, :]
    lag_f = jnp.clip(t_idx - s_idx, 0, tt)
    lag_b = jnp.clip(s_idx - t_idx, 0, tt)
    toe = (jnp.where((t_idx >= s_idx)[:, :, None, None, None], kern[lag_f, 0], 0.0)
           + jnp.where((s_idx >= t_idx)[:, :, None, None, None], kern[lag_b, 1], 0.0))
    skip = (jnp.eye(tt, dtype=F32)[:, :, None, None, None]
            * d.astype(F32).reshape(n_groups, S5_GROUP)[None, None, :, :, None]
            * jnp.eye(S5_GROUP, dtype=F32)[None, None, None])
    toe = (toe + skip).reshape(tt, tt, nb, bg, S5_GROUP, S5_GROUP)
    m_mat = jnp.einsum('stjgcm,gh->jsgmthc', toe, eye_g).reshape(nb, tt * LANES, tt * LANES)

    def inject(direction, power_of_s):
        p_re, p_im = pr[power_of_s, direction], pi[power_of_s, direction]
        v_re = p_re[..., None] * bb_re[direction][None] - p_im[..., None] * bb_im[direction][None]
        v_im = p_re[..., None] * bb_im[direction][None] + p_im[..., None] * bb_re[direction][None]
        def blk(v):
            v = v.reshape(tt, nb, bg, S5_STATE, S5_GROUP)
            return jnp.einsum('sjgnm,gh->jsgmhn', v, eye_g).reshape(nb, tt * LANES, bg * S5_STATE)
        return jnp.concatenate([blk(v_re), blk(v_im)], axis=-1)

    def readout(direction, power_of_t):
        e_re, e_im = cpr[power_of_t, direction], -cpi[power_of_t, direction]
        def blk(v):
            v = v.reshape(tt, nb, bg, S5_GROUP, S5_STATE)
            return jnp.einsum('tjgcn,gh->jgnthc', v, eye_g).reshape(nb, bg * S5_STATE, tt * LANES)
        return jnp.concatenate([blk(e_re), blk(e_im)], axis=1)

    steps = jnp.arange(tt)
    g_f = inject(0, tt - 1 - steps)
    g_b = inject(1, steps)
    e_f = readout(0, steps + 1)
    e_b = readout(1, tt - steps)
    half = bg * S5_STATE
    a_t = jnp.stack([pr[tt, 0].reshape(nb, half), pi[tt, 0].reshape(nb, half),
                     pr[tt, 1].reshape(nb, half), pi[tt, 1].reshape(nb, half)], axis=1)
    cast = lambda z: z.astype(BF16)
    return cast(m_mat), cast(g_f), cast(g_b), cast(e_f), cast(e_b), a_t


def _s5_body(u_ref, m_ref, gf_ref, gb_ref, ef_ref, eb_ref, a_ref, y_ref,
             us_ref, gfs_ref, gbs_ref, sfs_ref, sbs_ref, *, ctx_rows, width):
    j = pl.program_id(1)
    tt = S5_CHUNK
    nb = width // LANES
    n_rows = u_ref.shape[0]
    half = S5_BLOCK_GROUPS * S5_STATE

    for jj in range(nb):
        @pl.when(j == jj)
        def _gather(jj=jj):
            for s in range(tt):
                us_ref[:, s * LANES:(s + 1) * LANES] = u_ref[:, s * width + jj * LANES:s * width + (jj + 1) * LANES]

    ustack = us_ref[...]
    gfs_ref[...] = jnp.dot(ustack, gf_ref[0], preferred_element_type=F32)
    gbs_ref[...] = jnp.dot(ustack, gb_ref[0], preferred_element_type=F32)

    a = a_ref[0]
    afr, afi, abr, abi = a[0:1, :], a[1:2, :], a[2:3, :], a[3:4, :]

    def step(i, carry):
        fr, fi, br, bi = carry
        sfs_ref[pl.ds(i, 1), 0:half] = fr
        sfs_ref[pl.ds(i, 1), half:2 * half] = fi
        g = gfs_ref[pl.ds(i, 1), :]
        nfr = afr * fr - afi * fi + g[:, 0:half]
        nfi = afr * fi + afi * fr + g[:, half:2 * half]
        rb = jnp.where(i < ctx_rows, ctx_rows - 1 - i, n_rows - 1 - (i - ctx_rows))
        sbs_ref[pl.ds(rb, 1), 0:half] = br
        sbs_ref[pl.ds(rb, 1), half:2 * half] = bi
        g = gbs_ref[pl.ds(rb, 1), :]
        nbr = abr * br - abi * bi + g[:, 0:half]
        nbi = abr * bi + abi * br + g[:, half:2 * half]
        return nfr, nfi, nbr, nbi

    zero = jnp.zeros((1, half), F32)
    lax.fori_loop(0, n_rows, step, (zero, zero, zero, zero))

    y = (jnp.dot(ustack, m_ref[0], preferred_element_type=F32)
         + jnp.dot(sfs_ref[...].astype(BF16), ef_ref[0], preferred_element_type=F32)
         + jnp.dot(sbs_ref[...].astype(BF16), eb_ref[0], preferred_element_type=F32))
    for jj in range(nb):
        @pl.when(j == jj)
        def _scatter(jj=jj):
            for s in range(tt):
                y_ref[:, s * width + jj * LANES:s * width + (jj + 1) * LANES] = (
                    y[:, s * LANES:(s + 1) * LANES].astype(y_ref.dtype))


def _s5_scan(u, mats, n_batch, ctx_len):
    m_mat, g_f, g_b, e_f, e_b, a_t = mats
    rows, width = u.shape
    tt = S5_CHUNK
    nb = width // LANES
    srows = rows // tt // n_batch
    u2 = u.reshape(rows // tt, tt * width)
    kdim = tt * LANES
    sdim = 2 * S5_BLOCK_GROUPS * S5_STATE
    wspec = lambda shape: pl.BlockSpec((1,) + shape, lambda b, j: (j, 0, 0))
    y2 = pl.pallas_call(
        functools.partial(_s5_body, ctx_rows=ctx_len // tt, width=width),
        grid=(n_batch, nb),
        in_specs=[pl.BlockSpec((srows, tt * width), lambda b, j: (b, 0), pipeline_mode=pl.Buffered(1)),
                  wspec((kdim, kdim)), wspec((kdim, sdim)), wspec((kdim, sdim)),
                  wspec((sdim, kdim)), wspec((sdim, kdim)), wspec((4, sdim // 2))],
        out_specs=pl.BlockSpec((srows, tt * width), lambda b, j: (b, 0)),
        out_shape=jax.ShapeDtypeStruct((rows // tt, tt * width), BF16),
        scratch_shapes=[pltpu.VMEM((srows, kdim), BF16),
                        pltpu.VMEM((srows, sdim), F32), pltpu.VMEM((srows, sdim), F32),
                        pltpu.VMEM((srows, sdim), F32), pltpu.VMEM((srows, sdim), F32)],
        compiler_params=_cparams(("arbitrary", "arbitrary"), 56),
        name="s5_scan",
    )(u2, m_mat, g_f, g_b, e_f, e_b, a_t)
    return y2.reshape(rows, width)


def _rope_tables(seq_len):
    pos = jnp.arange(seq_len)
    row = (pos // GRID_W).astype(F32)
    col = (pos % GRID_W).astype(F32)
    inv_freq = ROPE_BASE ** (-jnp.arange(ROPE_PAIRS, dtype=F32) / ROPE_PAIRS)
    ar, ac = row[:, None] * inv_freq, col[:, None] * inv_freq
    cos = jnp.concatenate([jnp.cos(ar), jnp.cos(ar), jnp.cos(ac), jnp.cos(ac)], axis=1)
    sin = jnp.concatenate([-jnp.sin(ar), jnp.sin(ar), -jnp.sin(ac), jnp.sin(ac)], axis=1)
    return cos, sin


def _swa_body(sink_ref, q_ref, kp_ref, kc_ref, kn_ref, kx_ref, cp_ref, sp_ref, cc_ref, sc_ref,
              cn_ref, sn_ref, qg_ref, kg_ref, o_ref, *, lat_blocks):
    n = pl.program_id(1)
    hd = SWA_HEAD_DIM
    blk = SWA_BLOCK
    kvw = SWA_KV_HEADS * hd
    scale = hd ** -0.5
    lane = lax.broadcasted_iota(I32, (1, hd), 1)
    first_half = (lane % (2 * ROPE_PAIRS)) < ROPE_PAIRS
    row = lax.broadcasted_iota(I32, (blk, blk), 0)
    col = lax.broadcasted_iota(I32, (blk, blk), 1)
    qg, kg = qg_ref[...], kg_ref[...]

    def rope(x, cos, sin):
        rot = jnp.where(first_half, pltpu.roll(x, hd - ROPE_PAIRS, 1), pltpu.roll(x, ROPE_PAIRS, 1))
        return x * cos + rot * sin

    def attend(is_lat):
        for g in range(SWA_KV_HEADS):
            kcols = slice(g * hd, (g + 1) * hd)
            vcols = slice(kvw + g * hd, kvw + (g + 1) * hd)
            kx = _rms(kx_ref[:, kcols].astype(F32), kg).astype(BF16)
            vx = kx_ref[:, vcols]
            if is_lat:
                wins = ((kp_ref, cp_ref, sp_ref), (kc_ref, cc_ref, sc_ref), (kn_ref, cn_ref, sn_ref))
                ks = [rope(_rms(r[:, kcols].astype(F32), kg), c[...], s[...]).astype(BF16) for r, c, s in wins]
                vs = [r[:, vcols] for r, _, _ in wins]
                valid = [(col >= row) & (col + (n - 1) * blk >= 0), None,
                         (col <= row) & (col + (n + 1) * blk < lat_blocks * blk)]
            for hh in range(SWA_GROUP):
                h = g * SWA_GROUP + hh
                q = _rms(q_ref[:, h * hd:(h + 1) * hd].astype(F32), qg)
                if is_lat:
                    q = rope(q, cc_ref[...], sc_ref[...])
                q = (q * scale).astype(BF16)
                sink = sink_ref[h]
                scores = [_mm_nt(q, kx)]
                values = [vx]
                if is_lat:
                    for kk, vv, ok in zip(ks, vs, valid):
                        s = _mm_nt(q, kk)
                        scores.append(s if ok is None else jnp.where(ok, s, NEG_INF))
                        values.append(vv)
                m = jnp.max(scores[0], axis=-1, keepdims=True)
                for s in scores[1:]:
                    m = jnp.maximum(m, jnp.max(s, axis=-1, keepdims=True))
                m = jnp.maximum(m, sink)
                den = jnp.exp(sink - m)
                acc = None
                for s, vv in zip(scores, values):
                    p = jnp.exp(s - m)
                    den = den + jnp.sum(p, axis=-1, keepdims=True)
                    pv = _mm(p, vv)
                    acc = pv if acc is None else acc + pv
                o_ref[:, h * hd:(h + 1) * hd] = (acc / den).astype(o_ref.dtype)

    @pl.when(n < lat_blocks)
    def _lat():
        attend(True)

    @pl.when(n >= lat_blocks)
    def _ctx():
        attend(False)


def _swa_attend(q, kv, cos, sin, q_gain, k_gain, sink, n_batch, seq_len, ctx_len):
    rows = q.shape[0]
    blk = SWA_BLOCK
    lat_blocks = seq_len // blk
    ctx_blocks = ctx_len // blk
    bpb = lat_blocks + ctx_blocks
    hd = SWA_HEAD_DIM

    def q_map(b, n, s):
        return (b * bpb + jnp.where(n < lat_blocks, ctx_blocks + n, n - lat_blocks), 0)

    def win(off):
        def kv_map(b, n, s):
            return (b * bpb + ctx_blocks + jnp.clip(n + off, 0, lat_blocks - 1), 0)

        def tab_map(b, n, s):
            return (jnp.clip(n + off, 0, lat_blocks - 1), 0)
        return kv_map, tab_map

    (kv_p, tab_p), (kv_c, tab_c), (kv_n, tab_n) = win(-1), win(0), win(1)
    kvs = lambda m: pl.BlockSpec((blk, kv.shape[1]), m)
    tab = lambda m: pl.BlockSpec((blk, hd), m)
    grid_spec = pltpu.PrefetchScalarGridSpec(
        num_scalar_prefetch=1,
        grid=(n_batch, bpb),
        in_specs=[pl.BlockSpec((blk, q.shape[1]), q_map),
                  kvs(kv_p), kvs(kv_c), kvs(kv_n),
                  pl.BlockSpec((ctx_len, kv.shape[1]), lambda b, n, s: (b * (bpb * blk // ctx_len), 0)),
                  tab(tab_p), tab(tab_p), tab(tab_c), tab(tab_c), tab(tab_n), tab(tab_n),
                  pl.BlockSpec((1, hd), lambda b, n, s: (0, 0)),
                  pl.BlockSpec((1, hd), lambda b, n, s: (0, 0))],
        out_specs=pl.BlockSpec((blk, q.shape[1]), q_map),
    )
    return pl.pallas_call(
        functools.partial(_swa_body, lat_blocks=lat_blocks),
        grid_spec=grid_spec,
        out_shape=jax.ShapeDtypeStruct((rows, q.shape[1]), BF16),
        compiler_params=_cparams(("arbitrary", "arbitrary"), 32),
        name="swa_attend",
    )(sink.astype(F32), q, kv, kv, kv, kv, cos, sin, cos, sin, cos, sin,
      q_gain.reshape(1, hd).astype(F32), k_gain.reshape(1, hd).astype(F32))


def _route_rows(logits_t, bias_col):
    aff = _sigmoid(logits_t)
    biased = aff + bias_col
    v = [biased[e:e + 1, :] for e in range(N_EXPERTS)]
    a = [aff[e:e + 1, :] for e in range(N_EXPERTS)]
    gsz = EXPERTS_PER_GROUP
    best_g = best_s = None
    for g in range(N_EXPERT_GROUPS):
        vg = v[g * gsz:(g + 1) * gsz]
        score = None
        for i in range(gsz):
            for k in range(i + 1, gsz):
                pair = vg[i] + vg[k]
                score = pair if score is None else jnp.maximum(score, pair)
        if best_g is None:
            best_g, best_s = jnp.zeros_like(score, dtype=I32), score
        else:
            upd = score > best_s
            best_g = jnp.where(upd, g, best_g)
            best_s = jnp.where(upd, score, best_s)

    def pick(rows):
        out = []
        for i in range(gsz):
            x = rows[i]
            for g in range(1, N_EXPERT_GROUPS):
                x = jnp.where(best_g == g, rows[g * gsz + i], x)
            out.append(x)
        return out

    vb, ab = pick(v), pick(a)
    i1, v1, a1 = jnp.zeros_like(best_g), vb[0], ab[0]
    for i in range(1, gsz):
        upd = vb[i] > v1
        i1, v1, a1 = jnp.where(upd, i, i1), jnp.where(upd, vb[i], v1), jnp.where(upd, ab[i], a1)
    i2 = v2 = a2 = None
    for i in range(gsz):
        cand = jnp.where(i1 == i, -jnp.inf, vb[i])
        if i2 is None:
            i2, v2, a2 = jnp.zeros_like(best_g), cand, ab[0]
        else:
            upd = cand > v2
            i2, v2, a2 = jnp.where(upd, i, i2), jnp.where(upd, cand, v2), jnp.where(upd, ab[i], a2)
    tot = a1 + a2
    e1 = (best_g * gsz + i1).astype(F32)
    e2 = (best_g * gsz + i2).astype(F32)
    zeros = jnp.zeros((4, e1.shape[1]), F32)
    return jnp.concatenate([e1, e2, a1 / tot, a2 / tot, zeros], axis=0)


def _out_body(of_ref, ob_ref, r_ref, ys_ref, sw_ref, x_ref, g1_ref, sh_ref, sc_ref, n2_ref,
              og_ref, wglu_ref, bglu_ref, wo_ref, rw_ref, rb_ref, xo_ref, h_ref, rt_ref):
    o = of_ref[...] + ob_ref[...]
    r = r_ref[...].astype(F32)
    gain = og_ref[...]
    heads = []
    for h in range(GLA_HEADS):
        cols = slice(h * GLA_DV, (h + 1) * GLA_DV)
        heads.append(_rms(o[:, cols], gain[:, cols]))
    gla_y = jnp.concatenate(heads, axis=1) * _silu(r)
    z = _gelu_tanh(ys_ref[...].astype(F32))
    s5_y = z * _sigmoid(_mm(z, wglu_ref[...]) + bglu_ref[...])
    w1 = gla_y.shape[1]
    w2 = w1 + s5_y.shape[1]
    y = (_mm(gla_y, wo_ref[0:w1, :]) + _mm(s5_y, wo_ref[w1:w2, :])
         + jnp.dot(sw_ref[...], wo_ref[w2:, :], preferred_element_type=F32))
    x_new = x_ref[...] + g1_ref[0] * y
    xo_ref[...] = x_new
    h2 = _rms(x_new, n2_ref[...]) * (1.0 + sc_ref[0]) + sh_ref[0]
    h_ref[...] = h2
    logits_t = _mm_f32(rw_ref[...], h2, dims=(((1,), (1,)), ((), ())))
    rt_ref[0] = _route_rows(logits_t, rb_ref[...])


def _out_proj(o_f, o_b, r, ys, sw, xall, mod3, norm2, o_gain, w_glu, b_glu, w_out, rw_t, rb_col,
              tiles_per_batch, ctx_tiles, n_batch):
    rows, d = xall.shape
    tm = ROW_TILE
    nt = rows // tm
    mrow = functools.partial(_mod_row, tiles_per_batch=tiles_per_batch, ctx_tiles=ctx_tiles,
                             n_batch=n_batch)
    rowspec = lambda a: pl.BlockSpec((tm, a.shape[1]), lambda i: (i, 0))
    const = lambda a: pl.BlockSpec(a.shape, lambda i: (0,) * a.ndim)
    modspec = lambda k: pl.BlockSpec((1, 1, d), lambda i: (mrow(i) * 6 + k, 0, 0))
    consts = [norm2.reshape(1, d), o_gain.reshape(1, -1), w_glu, b_glu.reshape(1, -1)]
    return pl.pallas_call(
        _out_body,
        grid=(nt,),
        in_specs=[rowspec(o_f), rowspec(o_b), rowspec(r), rowspec(ys), rowspec(sw), rowspec(xall),
                  modspec(2), modspec(3), modspec(4)]
                 + [const(a) for a in consts]
                 + [pl.BlockSpec(w_out.shape, lambda i: (0, 0), pipeline_mode=pl.Buffered(1)),
                    const(rw_t), const(rb_col)],
        out_specs=[pl.BlockSpec((tm, d), lambda i: (i, 0)), pl.BlockSpec((tm, d), lambda i: (i, 0)),
                   pl.BlockSpec((1, 8, tm), lambda i: (i, 0, 0))],
        out_shape=[jax.ShapeDtypeStruct((rows, d), F32), jax.ShapeDtypeStruct((rows, d), F32),
                   jax.ShapeDtypeStruct((nt, 8, tm), F32)],
        compiler_params=_cparams(("arbitrary",), 48),
        name="out_proj",
    )(o_f, o_b, r, ys, sw, xall, mod3, mod3, mod3, *consts, w_out, rw_t, rb_col)


def _row_copy(src_hbm, row, dst, slot, sem):
    return pltpu.make_async_copy(src_hbm.at[pl.ds(row, 1)], dst.at[pl.ds(slot, 1)], sem)


def _moe_body(te_ref, tv_ref, idx_ref, x_hbm, wg_ref, wu_ref, wd_ref, o_ref, xbuf, sem):
    i = pl.program_id(0)
    tm = xbuf.shape[0]

    @pl.when(tv_ref[i] > 0)
    def _active():
        def issue(r, carry):
            _row_copy(x_hbm, idx_ref[0, 0, r], xbuf, r, sem).start()
            return carry
        lax.fori_loop(0, tm, issue, 0)

        def wait(r, carry):
            _row_copy(x_hbm, 0, xbuf, r, sem).wait()
            return carry
        lax.fori_loop(0, tm, wait, 0)
        x = xbuf[...].astype(BF16)
        gate = jnp.dot(x, wg_ref[0], preferred_element_type=F32)
        up = jnp.dot(x, wu_ref[0], preferred_element_type=F32)
        act = (_silu(gate) * up).astype(BF16)
        o_ref[...] = jnp.dot(act, wd_ref[0], preferred_element_type=F32)

    @pl.when(tv_ref[i] == 0)
    def _idle():
        o_ref[...] = jnp.zeros_like(o_ref)


def _moe_experts(tile_expert, tile_valid, slot_token, tokens, w_gate, w_up, w_down):
    tm = MOE_TILE
    nt = tile_expert.shape[0]
    d = tokens.shape[1]
    de = w_gate.shape[2]
    grid_spec = pltpu.PrefetchScalarGridSpec(
        num_scalar_prefetch=2,
        grid=(nt,),
        in_specs=[pl.BlockSpec((1, 1, tm), lambda i, te, tv: (i, 0, 0), memory_space=pltpu.SMEM),
                  pl.BlockSpec(memory_space=pl.ANY),
                  pl.BlockSpec((1, d, de), lambda i, te, tv: (te[i], 0, 0)),
                  pl.BlockSpec((1, d, de), lambda i, te, tv: (te[i], 0, 0)),
                  pl.BlockSpec((1, de, d), lambda i, te, tv: (te[i], 0, 0))],
        out_specs=pl.BlockSpec((tm, d), lambda i, te, tv: (i, 0)),
        scratch_shapes=[pltpu.VMEM((tm, d), F32), pltpu.SemaphoreType.DMA(())],
    )
    return pl.pallas_call(
        _moe_body,
        grid_spec=grid_spec,
        out_shape=jax.ShapeDtypeStruct((nt * tm, d), F32),
        compiler_params=_cparams(("arbitrary",), 52),
        name="moe_experts",
    )(tile_expert, tile_valid, slot_token.reshape(nt, 1, tm), tokens, w_gate, w_up, w_down)


def _combine_body(pos_ref, x_ref, g2_ref, w_ref, y_hbm, o_ref, ybuf, sem):
    tm = x_ref.shape[0]

    def issue(r, carry):
        _row_copy(y_hbm, pos_ref[0, 0, r], ybuf, r, sem).start()
        return carry
    lax.fori_loop(0, 2 * tm, issue, 0)

    def wait(r, carry):
        _row_copy(y_hbm, 0, ybuf, r, sem).wait()
        return carry
    lax.fori_loop(0, 2 * tm, wait, 0)
    w = w_ref[...]
    f = w[:, 0:1] * ybuf[0:tm, :] + w[:, 1:2] * ybuf[tm:2 * tm, :]
    o_ref[...] = x_ref[...] + g2_ref[0] * f


def _moe_combine(pos, x_new, mod3, wts, y_sorted, n_batch, tiles_per_batch, ctx_tiles, keep_ctx):
    rows, d = x_new.shape
    tm = ROW_TILE
    tpb = tiles_per_batch
    first = 0 if keep_ctx else ctx_tiles
    out_tpb = tpb - first
    in_tile = lambda b, t: b * tpb + first + t
    mrow = lambda b, t: jnp.where(first + t < ctx_tiles, n_batch, b)
    pos3 = pos.reshape(rows // tm, tm, 2).transpose(0, 2, 1).reshape(rows // tm, 1, 2 * tm)
    return pl.pallas_call(
        _combine_body,
        grid=(n_batch, out_tpb),
        in_specs=[pl.BlockSpec((1, 1, 2 * tm), lambda b, t: (in_tile(b, t), 0, 0), memory_space=pltpu.SMEM),
                  pl.BlockSpec((tm, d), lambda b, t: (in_tile(b, t), 0)),
                  pl.BlockSpec((1, 1, d), lambda b, t: (mrow(b, t) * 6 + 5, 0, 0)),
                  pl.BlockSpec((tm, 2), lambda b, t: (in_tile(b, t), 0)),
                  pl.BlockSpec(memory_space=pl.ANY)],
        out_specs=pl.BlockSpec((tm, d), lambda b, t: (b * out_tpb + t, 0)),
        out_shape=jax.ShapeDtypeStruct((n_batch * out_tpb * tm, d), F32),
        scratch_shapes=[pltpu.VMEM((2 * tm, d), F32), pltpu.SemaphoreType.DMA(())],
        compiler_params=_cparams(("arbitrary", "arbitrary"), 32),
        name="moe_combine",
    )(pos3, x_new, mod3, wts, y_sorted)


def _moe_plan(route, n_rows):
    tm = MOE_TILE
    e = route[:, 0:2, :].astype(I32).transpose(0, 2, 1).reshape(n_rows, 2)
    wts = route[:, 2:4, :].transpose(0, 2, 1).reshape(n_rows, 2)
    e_flat = e.reshape(-1)
    onehot = (e_flat[:, None] == jnp.arange(N_EXPERTS, dtype=I32)[None, :]).astype(I32)
    csum = jnp.cumsum(onehot, axis=0)
    rank = jnp.sum(onehot * csum, axis=1) - 1
    counts = csum[-1]
    tiles_e = (counts + tm - 1) // tm
    tile_end = jnp.cumsum(tiles_e)
    tile_start = tile_end - tiles_e
    pos = tile_start[e_flat] * tm + rank
    nt = (2 * n_rows) // tm + N_EXPERTS
    slot_token = jnp.zeros((nt * tm,), I32).at[pos].set(jnp.arange(2 * n_rows, dtype=I32) // 2,
                                                        unique_indices=True)
    n_active = tile_end[-1]
    tile_id = jnp.minimum(jnp.arange(nt, dtype=I32), n_active - 1)
    tile_expert = jnp.minimum(jnp.sum((tile_id[:, None] >= tile_end[None, :]).astype(I32), axis=1),
                              N_EXPERTS - 1).astype(I32)
    tile_valid = (jnp.arange(nt, dtype=I32) < n_active).astype(I32)
    return tile_expert, tile_valid, slot_token, pos.reshape(n_rows, 2), wts


def kernel(x, c, ctx, c_ctx, ada_w, ada_b, norm1, norm2, w_in, gla_w_alpha, gla_b_alpha, gla_o_norm,
           s5_a_re, s5_a_im, s5_log_dt, s5_b_re, s5_b_im, s5_c_re, s5_c_im, s5_d, s5_w_glu, s5_b_glu,
           swa_q_norm, swa_k_norm, swa_sink, w_out, router_w, router_b, exp_w_gate, exp_w_up, exp_w_down):
    n_batch, seq_len, d = x.shape
    ctx_len = ctx.shape[1]
    depth = ada_w.shape[0]
    tm = ROW_TILE
    per_batch = ctx_len + seq_len
    assert ctx_len % tm == 0 and seq_len % tm == 0
    tpb, ctx_tiles = per_batch // tm, ctx_len // tm
    rows = n_batch * per_batch

    hk = GLA_HEADS * GLA_DK
    gla_w = GLA_HEADS * GLA_DV
    s5_w = s5_d.shape[1]
    swa_w = SWA_HEADS * SWA_HEAD_DIM
    kv_w = SWA_KV_HEADS * SWA_HEAD_DIM
    o_q, o_k, o_v = 0, hk, 2 * hk
    o_lr = o_v + gla_w
    o_r = o_lr + 2 * GLA_RANK
    o_u = o_r + gla_w
    o_sq = o_u + s5_w
    o_sk = o_sq + swa_w
    outs = [(2 * hk, BF16), (gla_w, BF16), (gla_w, BF16), (s5_w, BF16), (swa_w, BF16), (2 * kv_w, BF16),
            (LANES, F32)]

    xall = jnp.concatenate([ctx, x], axis=1).reshape(rows, d)
    cvec = jnp.concatenate([c, c_ctx[None], jnp.zeros((8 - n_batch - 1, d), c.dtype)], axis=0)
    cos, sin = _rope_tables(seq_len)
    rw_t = jnp.zeros((LANES, d), F32).at[:N_EXPERTS].set(router_w.astype(F32).T)
    rb_col = jnp.zeros((LANES, 1), F32).at[:N_EXPERTS, 0].set(router_b.astype(F32))

    out = None
    for layer in range(depth):
        last = layer == depth - 1
        mod = _ada_mod(cvec, ada_w[layer], ada_b[layer])
        mod3 = mod.reshape(8 * 6, 1, d)

        wl = w_in[layer]
        w_prep = jnp.concatenate(
            [wl[:, o_q:o_v], wl[:, o_v:o_lr], wl[:, o_r:o_u], wl[:, o_u:o_sq], wl[:, o_sq:o_sk],
             wl[:, o_sk:], wl[:, o_lr:o_r], jnp.zeros((d, LANES - 2 * GLA_RANK), wl.dtype)],
            axis=1).astype(BF16)
        qk, gv, gr, u, sq, skv, lr = _in_proj(xall, mod3, norm1[layer], w_prep, outs, tpb, ctx_tiles, n_batch)

        wa = jnp.zeros((LANES, 2 * hk), F32)
        wa = wa.at[0:GLA_RANK, 0:hk].set(gla_w_alpha[layer, 0].astype(F32))
        wa = wa.at[GLA_RANK:2 * GLA_RANK, hk:].set(gla_w_alpha[layer, 1].astype(F32))
        ba = gla_b_alpha[layer].astype(F32).reshape(1, 2 * hk)
        o_f, o_b = _gla_scan(qk, gv, lr, wa, ba, n_batch, tpb, ctx_tiles)

        mats = _s5_prep(s5_a_re[layer], s5_a_im[layer], s5_log_dt[layer], s5_b_re[layer], s5_b_im[layer],
                        s5_c_re[layer], s5_c_im[layer], s5_d[layer])
        ys = _s5_scan(u, mats, n_batch, ctx_len)

        sw = _swa_attend(sq, skv, cos, sin, swa_q_norm[layer], swa_k_norm[layer], swa_sink[layer],
                         n_batch, seq_len, ctx_len)

        x_new, h2, route = _out_proj(o_f, o_b, gr, ys, sw, xall, mod3, norm2[layer], gla_o_norm[layer],
                                     s5_w_glu[layer].astype(BF16), s5_b_glu[layer].astype(F32),
                                     w_out[layer].astype(BF16), rw_t, rb_col, tpb, ctx_tiles, n_batch)

        tile_expert, tile_valid, slot_token, pos, wts = _moe_plan(route, rows)
        y_sorted = _moe_experts(tile_expert, tile_valid, slot_token, h2,
                                exp_w_gate[layer].astype(BF16), exp_w_up[layer].astype(BF16),
                                exp_w_down[layer].astype(BF16))
        out = _moe_combine(pos, x_new, mod3, wts, y_sorted, n_batch, tpb, ctx_tiles, keep_ctx=not last)
        xall = out
    return out.reshape(n_batch, seq_len, d)
```

```python
import functools
import math

import jax
import jax.numpy as jnp
from jax import lax
from jax.experimental import pallas as pl
from jax.experimental.pallas import tpu as pltpu

F32 = jnp.float32
BF16 = jnp.bfloat16
I32 = jnp.int32

EPS = 1e-6
NEG_INF = -1e30

LANES = 128
VMEM_BYTES = 64 * 1024 * 1024

ROW_TILE = 256
MOD_ROWS = 8

GLA_HEADS = 4
GLA_DK = 64
GLA_DV = 128
GLA_RANK = 16
GLA_GATE_NORM = 16.0
GLA_CHUNK = 64

S5_GROUP = 16
S5_STATE = 64
S5_CHUNK = 8
S5_BLOCK_GROUPS = LANES // S5_GROUP

SWA_HEAD_DIM = 128
SWA_HEADS = 8
SWA_KV_HEADS = 2
SWA_GROUP = SWA_HEADS // SWA_KV_HEADS
SWA_BLOCK = 128
ROPE_PAIRS = 32
ROPE_BASE = 10000.0
GRID_W = 64

N_EXPERTS = 16
N_EXPERT_GROUPS = 4
EXPERTS_PER_GROUP = 4
MOE_TILE = 256
DMA_UNROLL = 8


def _cparams(semantics, vmem_mb):
    return pltpu.CompilerParams(dimension_semantics=semantics,
                                vmem_limit_bytes=vmem_mb * 1024 * 1024)


def _mm(a, b):
    return jnp.dot(a.astype(BF16), b.astype(BF16), preferred_element_type=F32)


def _mm_nt(a, b):
    return lax.dot_general(a.astype(BF16), b.astype(BF16), (((1,), (1,)), ((), ())),
                           preferred_element_type=F32)


def _mm_tn(a, b):
    return lax.dot_general(a.astype(BF16), b.astype(BF16), (((0,), (0,)), ((), ())),
                           preferred_element_type=F32)


def _split(a, parts):
    out, rest = [], a
    for p in range(parts):
        piece = rest.astype(BF16)
        out.append(piece)
        if p + 1 < parts:
            rest = rest - piece.astype(F32)
    return out


def _mm_split(a, b, dims=(((1,), (0,)), ((), ())), parts=3, a_exact=False, b_exact=False):
    pa = [a.astype(BF16)] if a_exact else _split(a, parts)
    pb = [b.astype(BF16)] if b_exact else _split(b, parts)
    out = None
    for i, ai in enumerate(pa):
        for j, bj in enumerate(pb):
            if i + j < parts:
                t = lax.dot_general(ai, bj, dims, preferred_element_type=F32)
                out = t if out is None else out + t
    return out


def _sigmoid(x):
    return 1.0 / (1.0 + jnp.exp(-x))


def _silu(x):
    return x * _sigmoid(x)


def _log_sigmoid(x):
    return jnp.minimum(x, 0.0) - jnp.log(1.0 + jnp.exp(-jnp.abs(x)))


def _gelu_tanh(x):
    return 0.5 * x * (1.0 + jnp.tanh(math.sqrt(2.0 / math.pi) * (x + 0.044715 * (x * x * x))))


def _rms(x, gain):
    return x * lax.rsqrt(jnp.mean(x * x, axis=-1, keepdims=True) + EPS) * gain


def _ada_body(c_ref, w_ref, b_ref, o_ref):
    s = _silu(c_ref[...])
    o_ref[...] = _mm(s, w_ref[...]) + b_ref[...]


def _ada_mod(cvec, w, b):
    rows, d = cvec.shape
    depth, _, n = w.shape
    tn = 1024
    return pl.pallas_call(
        _ada_body,
        grid=(depth, n // tn),
        in_specs=[pl.BlockSpec((rows, d), lambda l, i: (0, 0)),
                  pl.BlockSpec((None, d, tn), lambda l, i: (l, 0, i)),
                  pl.BlockSpec((None, 1, tn), lambda l, i: (l, 0, i))],
        out_specs=pl.BlockSpec((None, rows, tn), lambda l, i: (l, 0, i)),
        out_shape=jax.ShapeDtypeStruct((depth, rows, n), F32),
        compiler_params=_cparams(("arbitrary", "arbitrary"), 40),
        name="ada_mod",
    )(cvec, w, b.reshape(depth, 1, n))


def _mod_row(i, tiles_per_batch, ctx_tiles, n_batch, layer):
    return layer * MOD_ROWS + jnp.where(i % tiles_per_batch < ctx_tiles, n_batch, i // tiles_per_batch)


def _proj_body(x_ref, sh_ref, sc_ref, g_ref, w_ref, *out_refs, splits):
    y = _rms(x_ref[...], g_ref[...])
    h = (y * (1.0 + sc_ref[0]) + sh_ref[0]).astype(BF16)
    for o_ref, (c0, width) in zip(out_refs, splits):
        o_ref[...] = jnp.dot(h, w_ref[:, c0:c0 + width],
                             preferred_element_type=F32).astype(o_ref.dtype)


def _in_proj(xall, mod3, gain, w, outs, tiles_per_batch, ctx_tiles, n_batch, layer):
    rows, d = xall.shape
    tm = ROW_TILE
    mrow = functools.partial(_mod_row, tiles_per_batch=tiles_per_batch, ctx_tiles=ctx_tiles,
                             n_batch=n_batch, layer=layer)
    splits, out_specs, out_shapes = [], [], []
    c0 = 0
    for width, dtype in outs:
        splits.append((c0, width))
        out_specs.append(pl.BlockSpec((tm, width), lambda i: (i, 0)))
        out_shapes.append(jax.ShapeDtypeStruct((rows, width), dtype))
        c0 += width
    return pl.pallas_call(
        functools.partial(_proj_body, splits=tuple(splits)),
        grid=(rows // tm,),
        in_specs=[pl.BlockSpec((tm, d), lambda i: (i, 0)),
                  pl.BlockSpec((1, 1, d), lambda i: (mrow(i) * 6 + 0, 0, 0)),
                  pl.BlockSpec((1, 1, d), lambda i: (mrow(i) * 6 + 1, 0, 0)),
                  pl.BlockSpec((1, d), lambda i: (0, 0)),
                  pl.BlockSpec((None,) + w.shape[1:], lambda i: (layer, 0, 0),
                               pipeline_mode=pl.Buffered(1))],
        out_specs=out_specs,
        out_shape=out_shapes,
        compiler_params=_cparams(("arbitrary",), 48),
        name="in_proj",
    )(xall, mod3, mod3, gain.reshape(1, d), w)


def _gla_body(qkf, vf, lrf, qkb, vb, lrb, wa_ref, ba_ref, of_ref, ob_ref, sf_ref, sb_ref):
    t = pl.program_id(1)
    ck = GLA_CHUNK
    hk = GLA_HEADS * GLA_DK

    @pl.when(t == 0)
    def _init():
        sf_ref[...] = jnp.zeros_like(sf_ref)
        sb_ref[...] = jnp.zeros_like(sb_ref)

    tmr = qkf.shape[0]
    n_chunks = tmr // ck
    rowb = lax.broadcasted_iota(I32, (tmr, tmr), 0)
    colb = lax.broadcasted_iota(I32, (tmr, tmr), 1)
    same_chunk = rowb // ck == colb // ck
    row4 = lax.broadcasted_iota(I32, (GLA_HEADS * ck, ck), 0) % ck
    col4 = lax.broadcasted_iota(I32, (GLA_HEADS * ck, ck), 1)
    lane = lax.broadcasted_iota(I32, (1, hk), 1)
    head_masks = [(lane // GLA_DK == h).astype(F32) for h in range(GLA_HEADS)]
    sel = (lax.broadcasted_iota(I32, (tmr, n_chunks * GLA_DV), 0) // ck
           == lax.broadcasted_iota(I32, (tmr, n_chunks * GLA_DV), 1) // GLA_DV).astype(F32)

    def run(qk_ref, v_ref, lr_ref, w, bias, s_ref, o_ref, reverse):
        tri = (same_chunk & (rowb <= colb if reverse else rowb >= colb)).astype(F32)
        amask = row4 <= col4 if reverse else row4 >= col4
        la = _log_sigmoid(_mm_split(lr_ref[...], w, parts=2) + bias) * (1.0 / GLA_GATE_NORM)
        b = _mm_split(tri, la, a_exact=True)
        decay_all = jnp.exp(_mm_split(la, sel, dims=(((0,), (0,)), ((), ())), b_exact=True))
        q = qk_ref[:, 0:hk].astype(F32)
        k = qk_ref[:, hk:2 * hk].astype(F32)
        qd_all = q * (GLA_DK ** -0.5) * jnp.exp(b)
        kd_all = k * jnp.exp(-b)
        state = s_ref[...]
        order = range(n_chunks - 1, -1, -1) if reverse else range(n_chunks)
        for ci in order:
            rows = slice(ci * ck, (ci + 1) * ck)
            v = v_ref[rows, :]
            last = ci * ck if reverse else (ci + 1) * ck - 1
            ke = k[rows, :] * jnp.exp(b[last:last + 1, :] - b[rows, :])
            qd = qd_all[rows, :]
            qs = jnp.concatenate([qd * m for m in head_masks], axis=0).astype(BF16)
            att = jnp.where(amask, _mm_nt(qs, kd_all[rows, :]), 0.0)
            inter = _mm(qs, state)
            outs = []
            for h in range(GLA_HEADS):
                hr = slice(h * ck, (h + 1) * ck)
                outs.append(_mm(att[hr, :], v[:, h * GLA_DV:(h + 1) * GLA_DV]) + inter[hr, :])
            o_ref[rows, :] = jnp.concatenate(outs, axis=1)
            ds = _mm_tn(ke, v)
            ds = jnp.concatenate([ds[h * GLA_DK:(h + 1) * GLA_DK, h * GLA_DV:(h + 1) * GLA_DV]
                                  for h in range(GLA_HEADS)], axis=0)
            state = decay_all[:, ci * GLA_DV:(ci + 1) * GLA_DV] * state + ds
        s_ref[...] = state

    run(qkf, vf, lrf, wa_ref[:, 0:hk], ba_ref[:, 0:hk], sf_ref, of_ref, False)
    run(qkb, vb, lrb, wa_ref[:, hk:2 * hk], ba_ref[:, hk:2 * hk], sb_ref, ob_ref, True)


def _gla_scan(qk, v, lr, wa, ba, n_batch, tiles_per_batch, ctx_tiles):
    rows = qk.shape[0]
    tm = ROW_TILE
    tpb = tiles_per_batch

    def fwd(b, t):
        return (b * tpb + t, 0)

    def bwd(b, t):
        return (b * tpb + jnp.where(t < ctx_tiles, ctx_tiles - 1 - t, tpb - 1 - (t - ctx_tiles)), 0)

    wv = v.shape[1]
    return pl.pallas_call(
        _gla_body,
        grid=(n_batch, tpb),
        in_specs=[pl.BlockSpec((tm, qk.shape[1]), fwd), pl.BlockSpec((tm, wv), fwd),
                  pl.BlockSpec((tm, lr.shape[1]), fwd),
                  pl.BlockSpec((tm, qk.shape[1]), bwd), pl.BlockSpec((tm, wv), bwd),
                  pl.BlockSpec((tm, lr.shape[1]), bwd),
                  pl.BlockSpec(wa.shape, lambda b, t: (0, 0)),
                  pl.BlockSpec(ba.shape, lambda b, t: (0, 0))],
        out_specs=[pl.BlockSpec((tm, wv), fwd), pl.BlockSpec((tm, wv), bwd)],
        out_shape=[jax.ShapeDtypeStruct((rows, wv), F32)] * 2,
        scratch_shapes=[pltpu.VMEM((GLA_HEADS * GLA_DK, GLA_DV), F32)] * 2,
        compiler_params=_cparams(("arbitrary", "arbitrary"), 32),
        name="gla_scan",
    )(qk, v, lr, qk, v, lr, wa, ba)


def _s5_prep(a_re, a_im, log_dt, b_re, b_im, c_re, c_im, d):
    hp = lax.Precision.HIGHEST
    tt = S5_CHUNK
    n_groups = a_re.shape[1]
    bg = S5_BLOCK_GROUPS
    nb = n_groups // bg
    a_re, a_im, b_re, b_im = a_re.astype(F32), a_im.astype(F32), b_re.astype(F32), b_im.astype(F32)
    c_re, c_im = c_re.astype(F32), c_im.astype(F32)
    dt = jnp.exp(log_dt.astype(F32))[..., None]
    mag = jnp.exp(dt * a_re)
    ab_re, ab_im = mag * jnp.cos(dt * a_im), mag * jnp.sin(dt * a_im)
    den = a_re * a_re + a_im * a_im
    nr = ab_re - 1.0
    coef_re = (nr * a_re + ab_im * a_im) / den
    coef_im = (ab_im * a_re - nr * a_im) / den
    bb_re = coef_re[..., None] * b_re - coef_im[..., None] * b_im
    bb_im = coef_re[..., None] * b_im + coef_im[..., None] * b_re
    pr, pi = [jnp.ones_like(ab_re)], [jnp.zeros_like(ab_im)]
    for _ in range(tt):
        pr, pi = pr + [pr[-1] * ab_re - pi[-1] * ab_im], pi + [pr[-1] * ab_im + pi[-1] * ab_re]
    pr, pi = jnp.stack(pr), jnp.stack(pi)
    cpr = c_re[None] * pr[:, :, :, None, :] - c_im[None] * pi[:, :, :, None, :]
    cpi = c_re[None] * pi[:, :, :, None, :] + c_im[None] * pr[:, :, :, None, :]
    kern = (jnp.einsum('kdgcn,dgnm->kdgcm', cpr, bb_re, precision=hp)
            - jnp.einsum('kdgcn,dgnm->kdgcm', cpi, bb_im, precision=hp))
    kdim, half = tt * LANES, bg * S5_STATE
    ax_k, ax_s, ax_tc = jnp.arange(kdim), jnp.arange(half), jnp.arange(tt * S5_GROUP)
    k_group, k_step, k_chan = (ax_k // S5_GROUP) % bg, ax_k // LANES, ax_k % S5_GROUP
    rep_tc = ((ax_tc[:, None] // S5_GROUP == k_step[None, :])
              & (ax_tc[:, None] % S5_GROUP == k_chan[None, :])).astype(F32)
    rep_n = (jnp.arange(S5_STATE)[:, None] == ax_s[None, :] % S5_STATE).astype(F32)
    mask_kk = k_group[:, None] == k_group[None, :]
    mask_ks = k_group[:, None] == ax_s[None, :] // S5_STATE
    expand = lambda table, rep: jnp.einsum('jrk,kc->jrc', table, rep, precision=hp)

    s_idx = jnp.arange(tt)[:, None]
    t_idx = jnp.arange(tt)[None, :]
    lag_f = jnp.clip(t_idx - s_idx, 0, tt)
    lag_b = jnp.clip(s_idx - t_idx, 0, tt)
    toe = (jnp.where((t_idx >= s_idx)[:, :, None, None, None], kern[lag_f, 0], 0.0)
           + jnp.where((s_idx >= t_idx)[:, :, None, None, None], kern[lag_b, 1], 0.0))
    skip = (jnp.eye(tt, dtype=F32)[:, :, None, None, None]
            * d.astype(F32).reshape(n_groups, S5_GROUP)[None, None, :, :, None]
            * jnp.eye(S5_GROUP, dtype=F32)[None, None, None])
    toe = (toe + skip).reshape(tt, tt, nb, bg, S5_GROUP, S5_GROUP)
    toe = toe.transpose(2, 0, 3, 5, 1, 4).reshape(nb, kdim, tt * S5_GROUP)
    m_mat = jnp.where(mask_kk, expand(toe, rep_tc), 0.0)

    def inject(direction, power_of_s):
        p_re, p_im = pr[power_of_s, direction], pi[power_of_s, direction]
        v_re = p_re[..., None] * bb_re[direction][None] - p_im[..., None] * bb_im[direction][None]
        v_im = p_re[..., None] * bb_im[direction][None] + p_im[..., None] * bb_re[direction][None]
        def blk(v):
            v = v.reshape(tt, nb, bg, S5_STATE, S5_GROUP).transpose(1, 0, 2, 4, 3).reshape(nb, kdim, S5_STATE)
            return jnp.where(mask_ks, expand(v, rep_n), 0.0)
        return jnp.concatenate([blk(v_re), blk(v_im)], axis=-1)

    def readout(direction, power_of_t):
        e_re, e_im = cpr[power_of_t, direction], -cpi[power_of_t, direction]
        def blk(v):
            v = v.reshape(tt, nb, bg, S5_GROUP, S5_STATE).transpose(1, 2, 4, 0, 3).reshape(nb, half, tt * S5_GROUP)
            return jnp.where(mask_ks.T, expand(v, rep_tc), 0.0)
        return jnp.concatenate([blk(e_re), blk(e_im)], axis=1)

    steps = jnp.arange(tt)
    g_f = inject(0, tt - 1 - steps)
    g_b = inject(1, steps)
    e_f = readout(0, steps + 1)
    e_b = readout(1, tt - steps)
    a_t = jnp.stack([pr[tt, 0].reshape(nb, half), pi[tt, 0].reshape(nb, half),
                     pr[tt, 1].reshape(nb, half), pi[tt, 1].reshape(nb, half)], axis=1)
    cast = lambda z: z.astype(BF16)
    return cast(m_mat), cast(g_f), cast(g_b), cast(e_f), cast(e_b), a_t


def _s5_body(u_ref, m_ref, gf_ref, gb_ref, ef_ref, eb_ref, a_ref, y_ref,
             us_ref, gfs_ref, gbs_ref, sfs_ref, sbs_ref, *, ctx_rows, width):
    j = pl.program_id(1)
    tt = S5_CHUNK
    nb = width // LANES
    n_rows = u_ref.shape[0]
    half = S5_BLOCK_GROUPS * S5_STATE

    for jj in range(nb):
        @pl.when(j == jj)
        def _gather(jj=jj):
            for s in range(tt):
                us_ref[:, s * LANES:(s + 1) * LANES] = u_ref[:, s * width + jj * LANES:s * width + (jj + 1) * LANES]

    ustack = us_ref[...]
    gfs_ref[...] = jnp.dot(ustack, gf_ref[0], preferred_element_type=F32)
    gbs_ref[...] = jnp.dot(ustack, gb_ref[0], preferred_element_type=F32)

    a = a_ref[0]
    afr, afi, abr, abi = a[0:1, :], a[1:2, :], a[2:3, :], a[3:4, :]

    def step(i, carry):
        fr, fi, br, bi = carry
        sfs_ref[pl.ds(i, 1), 0:half] = fr
        sfs_ref[pl.ds(i, 1), half:2 * half] = fi
        g = gfs_ref[pl.ds(i, 1), :]
        nfr = afr * fr - afi * fi + g[:, 0:half]
        nfi = afr * fi + afi * fr + g[:, half:2 * half]
        rb = jnp.where(i < ctx_rows, ctx_rows - 1 - i, n_rows - 1 - (i - ctx_rows))
        sbs_ref[pl.ds(rb, 1), 0:half] = br
        sbs_ref[pl.ds(rb, 1), half:2 * half] = bi
        g = gbs_ref[pl.ds(rb, 1), :]
        nbr = abr * br - abi * bi + g[:, 0:half]
        nbi = abr * bi + abi * br + g[:, half:2 * half]
        return nfr, nfi, nbr, nbi

    zero = jnp.zeros((1, half), F32)
    lax.fori_loop(0, n_rows, step, (zero, zero, zero, zero))

    y = (jnp.dot(ustack, m_ref[0], preferred_element_type=F32)
         + jnp.dot(sfs_ref[...].astype(BF16), ef_ref[0], preferred_element_type=F32)
         + jnp.dot(sbs_ref[...].astype(BF16), eb_ref[0], preferred_element_type=F32))
    for jj in range(nb):
        @pl.when(j == jj)
        def _scatter(jj=jj):
            for s in range(tt):
                y_ref[:, s * width + jj * LANES:s * width + (jj + 1) * LANES] = (
                    y[:, s * LANES:(s + 1) * LANES].astype(y_ref.dtype))


def _s5_scan(u, mats, n_batch, ctx_len, layer):
    m_mat, g_f, g_b, e_f, e_b, a_t = mats
    rows, width = u.shape
    tt = S5_CHUNK
    nb = width // LANES
    srows = rows // tt // n_batch
    u2 = u.reshape(rows // tt, tt * width)
    kdim = tt * LANES
    sdim = 2 * S5_BLOCK_GROUPS * S5_STATE
    wspec = lambda shape: pl.BlockSpec((None, 1) + shape, lambda b, j: (layer, j, 0, 0))
    y2 = pl.pallas_call(
        functools.partial(_s5_body, ctx_rows=ctx_len // tt, width=width),
        grid=(n_batch, nb),
        in_specs=[pl.BlockSpec((srows, tt * width), lambda b, j: (b, 0), pipeline_mode=pl.Buffered(1)),
                  wspec((kdim, kdim)), wspec((kdim, sdim)), wspec((kdim, sdim)),
                  wspec((sdim, kdim)), wspec((sdim, kdim)), wspec((4, sdim // 2))],
        out_specs=pl.BlockSpec((srows, tt * width), lambda b, j: (b, 0)),
        out_shape=jax.ShapeDtypeStruct((rows // tt, tt * width), BF16),
        scratch_shapes=[pltpu.VMEM((srows, kdim), BF16),
                        pltpu.VMEM((srows, sdim), F32), pltpu.VMEM((srows, sdim), F32),
                        pltpu.VMEM((srows, sdim), F32), pltpu.VMEM((srows, sdim), F32)],
        compiler_params=_cparams(("arbitrary", "arbitrary"), 56),
        name="s5_scan",
    )(u2, m_mat, g_f, g_b, e_f, e_b, a_t)
    return y2.reshape(rows, width)


def _rope_tables(seq_len):
    pos = jnp.arange(seq_len)
    row = (pos // GRID_W).astype(F32)
    col = (pos % GRID_W).astype(F32)
    inv_freq = ROPE_BASE ** (-jnp.arange(ROPE_PAIRS, dtype=F32) / ROPE_PAIRS)
    ar, ac = row[:, None] * inv_freq, col[:, None] * inv_freq
    cos = jnp.concatenate([jnp.cos(ar), jnp.cos(ar), jnp.cos(ac), jnp.cos(ac)], axis=1)
    sin = jnp.concatenate([-jnp.sin(ar), jnp.sin(ar), -jnp.sin(ac), jnp.sin(ac)], axis=1)
    return cos, sin


def _swa_body(sink_ref, q_ref, kp_ref, kc_ref, kn_ref, kx_ref, cp_ref, sp_ref, cc_ref, sc_ref,
              cn_ref, sn_ref, qg_ref, kg_ref, o_ref, *, lat_blocks):
    n = pl.program_id(1)
    hd = SWA_HEAD_DIM
    blk = SWA_BLOCK
    kvw = SWA_KV_HEADS * hd
    scale = hd ** -0.5
    lane = lax.broadcasted_iota(I32, (1, hd), 1)
    first_half = (lane % (2 * ROPE_PAIRS)) < ROPE_PAIRS
    row = lax.broadcasted_iota(I32, (blk, blk), 0)
    col = lax.broadcasted_iota(I32, (blk, blk), 1)
    qg, kg = qg_ref[...], kg_ref[...]

    def rope(x, cos, sin):
        rot = jnp.where(first_half, pltpu.roll(x, hd - ROPE_PAIRS, 1), pltpu.roll(x, ROPE_PAIRS, 1))
        return x * cos + rot * sin

    def attend(is_lat):
        for g in range(SWA_KV_HEADS):
            kcols = slice(g * hd, (g + 1) * hd)
            vcols = slice(kvw + g * hd, kvw + (g + 1) * hd)
            kx = _rms(kx_ref[:, kcols].astype(F32), kg).astype(BF16)
            vx = kx_ref[:, vcols]
            if is_lat:
                wins = ((kp_ref, cp_ref, sp_ref), (kc_ref, cc_ref, sc_ref), (kn_ref, cn_ref, sn_ref))
                ks = [rope(_rms(r[:, kcols].astype(F32), kg), c[...], s[...]).astype(BF16) for r, c, s in wins]
                vs = [r[:, vcols] for r, _, _ in wins]
                valid = [(col >= row) & (col + (n - 1) * blk >= 0), None,
                         (col <= row) & (col + (n + 1) * blk < lat_blocks * blk)]
            for hh in range(SWA_GROUP):
                h = g * SWA_GROUP + hh
                q = _rms(q_ref[:, h * hd:(h + 1) * hd].astype(F32), qg)
                if is_lat:
                    q = rope(q, cc_ref[...], sc_ref[...])
                q = (q * scale).astype(BF16)
                sink = sink_ref[h]
                scores = [_mm_nt(q, kx)]
                values = [vx]
                if is_lat:
                    for kk, vv, ok in zip(ks, vs, valid):
                        s = _mm_nt(q, kk)
                        scores.append(s if ok is None else jnp.where(ok, s, NEG_INF))
                        values.append(vv)
                def lane_fold(op, blocks):
                    parts = [b[:, c0:c0 + blk] for b in blocks for c0 in range(0, b.shape[1], blk)]
                    out = parts[0]
                    for part in parts[1:]:
                        out = op(out, part)
                    return out
                m = jnp.maximum(jnp.max(lane_fold(jnp.maximum, scores), axis=-1, keepdims=True), sink)
                probs = [jnp.exp(s - m) for s in scores]
                den = jnp.exp(sink - m) + jnp.sum(lane_fold(jnp.add, probs), axis=-1, keepdims=True)
                acc = None
                for p, vv in zip(probs, values):
                    pv = _mm(p, vv)
                    acc = pv if acc is None else acc + pv
                o_ref[:, h * hd:(h + 1) * hd] = (acc / den).astype(o_ref.dtype)

    @pl.when(n < lat_blocks)
    def _lat():
        attend(True)

    @pl.when(n >= lat_blocks)
    def _ctx():
        attend(False)


def _swa_attend(q, kv, cos, sin, q_gain, k_gain, sink, n_batch, seq_len, ctx_len):
    rows = q.shape[0]
    blk = SWA_BLOCK
    lat_blocks = seq_len // blk
    ctx_blocks = ctx_len // blk
    bpb = lat_blocks + ctx_blocks
    hd = SWA_HEAD_DIM

    def q_map(b, n, s):
        return (b * bpb + jnp.where(n < lat_blocks, ctx_blocks + n, n - lat_blocks), 0)

    def win(off):
        def kv_map(b, n, s):
            return (b * bpb + ctx_blocks + jnp.clip(n + off, 0, lat_blocks - 1), 0)

        def tab_map(b, n, s):
            return (jnp.clip(n + off, 0, lat_blocks - 1), 0)
        return kv_map, tab_map

    (kv_p, tab_p), (kv_c, tab_c), (kv_n, tab_n) = win(-1), win(0), win(1)
    kvs = lambda m: pl.BlockSpec((blk, kv.shape[1]), m)
    tab = lambda m: pl.BlockSpec((blk, hd), m)
    grid_spec = pltpu.PrefetchScalarGridSpec(
        num_scalar_prefetch=1,
        grid=(n_batch, bpb),
        in_specs=[pl.BlockSpec((blk, q.shape[1]), q_map),
                  kvs(kv_p), kvs(kv_c), kvs(kv_n),
                  pl.BlockSpec((ctx_len, kv.shape[1]), lambda b, n, s: (b * (bpb * blk // ctx_len), 0)),
                  tab(tab_p), tab(tab_p), tab(tab_c), tab(tab_c), tab(tab_n), tab(tab_n),
                  pl.BlockSpec((1, hd), lambda b, n, s: (0, 0)),
                  pl.BlockSpec((1, hd), lambda b, n, s: (0, 0))],
        out_specs=pl.BlockSpec((blk, q.shape[1]), q_map),
    )
    return pl.pallas_call(
        functools.partial(_swa_body, lat_blocks=lat_blocks),
        grid_spec=grid_spec,
        out_shape=jax.ShapeDtypeStruct((rows, q.shape[1]), BF16),
        compiler_params=_cparams(("arbitrary", "arbitrary"), 32),
        name="swa_attend",
    )(sink.astype(F32), q, kv, kv, kv, kv, cos, sin, cos, sin, cos, sin,
      q_gain.reshape(1, hd).astype(F32), k_gain.reshape(1, hd).astype(F32))


def _route_rows(logits_t, bias_col):
    aff = _sigmoid(logits_t)
    biased = aff + bias_col
    v = [biased[e:e + 1, :] for e in range(N_EXPERTS)]
    a = [aff[e:e + 1, :] for e in range(N_EXPERTS)]
    gsz = EXPERTS_PER_GROUP
    best_g = best_s = None
    for g in range(N_EXPERT_GROUPS):
        vg = v[g * gsz:(g + 1) * gsz]
        score = None
        for i in range(gsz):
            for k in range(i + 1, gsz):
                pair = vg[i] + vg[k]
                score = pair if score is None else jnp.maximum(score, pair)
        if best_g is None:
            best_g, best_s = jnp.zeros_like(score, dtype=I32), score
        else:
            upd = score > best_s
            best_g = jnp.where(upd, g, best_g)
            best_s = jnp.where(upd, score, best_s)

    def pick(rows):
        out = []
        for i in range(gsz):
            x = rows[i]
            for g in range(1, N_EXPERT_GROUPS):
                x = jnp.where(best_g == g, rows[g * gsz + i], x)
            out.append(x)
        return out

    vb, ab = pick(v), pick(a)
    i1, v1, a1 = jnp.zeros_like(best_g), vb[0], ab[0]
    for i in range(1, gsz):
        upd = vb[i] > v1
        i1, v1, a1 = jnp.where(upd, i, i1), jnp.where(upd, vb[i], v1), jnp.where(upd, ab[i], a1)
    i2 = v2 = a2 = None
    for i in range(gsz):
        cand = jnp.where(i1 == i, -jnp.inf, vb[i])
        if i2 is None:
            i2, v2, a2 = jnp.zeros_like(best_g), cand, ab[0]
        else:
            upd = cand > v2
            i2, v2, a2 = jnp.where(upd, i, i2), jnp.where(upd, cand, v2), jnp.where(upd, ab[i], a2)
    tot = a1 + a2
    e1 = (best_g * gsz + i1).astype(F32)
    e2 = (best_g * gsz + i2).astype(F32)
    zeros = jnp.zeros((4, e1.shape[1]), F32)
    return jnp.concatenate([e1, e2, a1 / tot, a2 / tot, zeros], axis=0)


def _out_body(of_ref, ob_ref, r_ref, ys_ref, sw_ref, x_ref, g1_ref, sh_ref, sc_ref, n2_ref,
              og_ref, wglu_ref, bglu_ref, wo_ref, rw_ref, rb_ref, xo_ref, h_ref, rt_ref):
    o = of_ref[...] + ob_ref[...]
    r = r_ref[...].astype(F32)
    gain = og_ref[...]
    heads = []
    for h in range(GLA_HEADS):
        cols = slice(h * GLA_DV, (h + 1) * GLA_DV)
        heads.append(_rms(o[:, cols], gain[:, cols]))
    gla_y = jnp.concatenate(heads, axis=1) * _silu(r)
    z = _gelu_tanh(ys_ref[...].astype(F32))
    s5_y = z * _sigmoid(_mm(z, wglu_ref[...]) + bglu_ref[...])
    w1 = gla_y.shape[1]
    w2 = w1 + s5_y.shape[1]
    y = (_mm(gla_y, wo_ref[0:w1, :]) + _mm(s5_y, wo_ref[w1:w2, :])
         + jnp.dot(sw_ref[...], wo_ref[w2:, :], preferred_element_type=F32))
    x_new = x_ref[...] + g1_ref[0] * y
    xo_ref[...] = x_new
    h2 = _rms(x_new, n2_ref[...]) * (1.0 + sc_ref[0]) + sh_ref[0]
    h_ref[...] = h2
    logits_t = _mm_split(rw_ref[...], h2, dims=(((1,), (1,)), ((), ())), parts=2)
    rt_ref[0] = _route_rows(logits_t, rb_ref[...])


def _out_proj(o_f, o_b, r, ys, sw, xall, mod3, norm2, o_gain, w_glu, b_glu, w_out, rw_t, rb_col,
              tiles_per_batch, ctx_tiles, n_batch, layer):
    rows, d = xall.shape
    tm = ROW_TILE
    nt = rows // tm
    mrow = functools.partial(_mod_row, tiles_per_batch=tiles_per_batch, ctx_tiles=ctx_tiles,
                             n_batch=n_batch, layer=layer)
    rowspec = lambda a: pl.BlockSpec((tm, a.shape[1]), lambda i: (i, 0))
    const = lambda a: pl.BlockSpec(a.shape, lambda i: (0,) * a.ndim)
    layered = lambda a, **kw: pl.BlockSpec((None,) + a.shape[1:], lambda i: (layer,) + (0,) * (a.ndim - 1), **kw)
    modspec = lambda k: pl.BlockSpec((1, 1, d), lambda i: (mrow(i) * 6 + k, 0, 0))
    consts = [norm2.reshape(1, d), o_gain.reshape(1, -1)]
    return pl.pallas_call(
        _out_body,
        grid=(nt,),
        in_specs=[rowspec(o_f), rowspec(o_b), rowspec(r), rowspec(ys), rowspec(sw), rowspec(xall),
                  modspec(2), modspec(3), modspec(4)]
                 + [const(a) for a in consts]
                 + [layered(w_glu), const(b_glu.reshape(1, -1)),
                    layered(w_out, pipeline_mode=pl.Buffered(1)), const(rw_t), const(rb_col)],
        out_specs=[pl.BlockSpec((tm, d), lambda i: (i, 0)), pl.BlockSpec((tm, d), lambda i: (i, 0)),
                   pl.BlockSpec((1, 8, tm), lambda i: (i, 0, 0))],
        out_shape=[jax.ShapeDtypeStruct((rows, d), F32), jax.ShapeDtypeStruct((rows, d), F32),
                   jax.ShapeDtypeStruct((nt, 8, tm), F32)],
        compiler_params=_cparams(("arbitrary",), 48),
        name="out_proj",
    )(o_f, o_b, r, ys, sw, xall, mod3, mod3, mod3, *consts, w_glu, b_glu.reshape(1, -1), w_out, rw_t, rb_col)


def _row_copy(src_hbm, row, dst, slot, sem):
    return pltpu.make_async_copy(src_hbm.at[pl.ds(row, 1)], dst.at[pl.ds(slot, 1)], sem)


def _gather_start(src_hbm, idx_ref, dst, sem, n_rows, unrolled=False):
    if unrolled:
        for r in range(n_rows):
            _row_copy(src_hbm, idx_ref[0, 0, r], dst, r, sem).start()
        return

    def body(g, carry):
        for u in range(DMA_UNROLL):
            r = g * DMA_UNROLL + u
            _row_copy(src_hbm, idx_ref[0, 0, r], dst, r, sem).start()
        return carry
    lax.fori_loop(0, n_rows // DMA_UNROLL, body, 0)


def _gather_wait(src_hbm, dst, sem, n_rows):
    for r in range(n_rows):
        _row_copy(src_hbm, 0, dst, r, sem).wait()


def _double_buffered(step_idx, n_steps, src_hbm, bufs, sem, start, work):
    n_rows = bufs[0].shape[0]

    @pl.when(step_idx == 0)
    def _prime():
        start(bufs[0], sem.at[0], False, False)

    for parity in range(2):
        cur, nxt = bufs[parity], bufs[1 - parity]

        @pl.when(step_idx % 2 == parity)
        def _step(parity=parity, cur=cur, nxt=nxt):
            _gather_wait(src_hbm, cur, sem.at[parity], n_rows)
            work(cur, lambda unrolled: start(nxt, sem.at[1 - parity], True, unrolled))

            @pl.when(step_idx == n_steps - 1)
            def _drain():
                _gather_wait(src_hbm, nxt, sem.at[1 - parity], n_rows)


def _moe_body(te_ref, tv_ref, idx_ref, nxt_ref, x_hbm, wg_ref, wu_ref, wd_ref, o_ref, xb0, xb1, sem, *,
              n_steps):
    i = pl.program_id(0)
    tm = xb0.shape[0]

    def start(buf, s, for_next, unrolled):
        _gather_start(x_hbm, nxt_ref if for_next else idx_ref, buf, s, tm, unrolled)

    def work(cur, start_next):
        @pl.when(tv_ref[i] > 0)
        def _active():
            start_next(True)
            x = cur[...].astype(BF16)
            gate = jnp.dot(x, wg_ref[...], preferred_element_type=F32)
            up = jnp.dot(x, wu_ref[...], preferred_element_type=F32)
            act = (_silu(gate) * up).astype(BF16)
            o_ref[...] = jnp.dot(act, wd_ref[...], preferred_element_type=F32)

        @pl.when(tv_ref[i] == 0)
        def _idle():
            start_next(False)
            o_ref[...] = jnp.zeros_like(o_ref)

    _double_buffered(i, n_steps, x_hbm, (xb0, xb1), sem, start, work)


def _moe_experts(tile_expert, tile_valid, slot_token, tokens, w_gate, w_up, w_down, layer):
    tm = MOE_TILE
    nt = tile_expert.shape[0]
    d = tokens.shape[1]
    de = w_gate.shape[3]
    wspec = lambda k, n: pl.BlockSpec((None, None, k, n), lambda i, te, tv: (layer, te[i], 0, 0))
    idx = lambda off: pl.BlockSpec((1, 1, tm), lambda i, te, tv: (jnp.minimum(i + off, nt - 1), 0, 0),
                                   memory_space=pltpu.SMEM)
    grid_spec = pltpu.PrefetchScalarGridSpec(
        num_scalar_prefetch=2,
        grid=(nt,),
        in_specs=[idx(0), idx(1), pl.BlockSpec(memory_space=pl.ANY),
                  wspec(d, de), wspec(d, de), wspec(de, d)],
        out_specs=pl.BlockSpec((tm, d), lambda i, te, tv: (i, 0)),
        scratch_shapes=[pltpu.VMEM((tm, d), F32), pltpu.VMEM((tm, d), F32), pltpu.SemaphoreType.DMA((2,))],
    )
    slots = slot_token.reshape(nt, 1, tm)
    return pl.pallas_call(
        functools.partial(_moe_body, n_steps=nt),
        grid_spec=grid_spec,
        out_shape=jax.ShapeDtypeStruct((nt * tm, d), F32),
        compiler_params=_cparams(("arbitrary",), 56),
        name="moe_experts",
    )(tile_expert, tile_valid, slots, slots, tokens, w_gate, w_up, w_down)


def _combine_body(pos_ref, nxt_ref, x_ref, g2_ref, w_ref, y_hbm, o_ref, yb0, yb1, sem, *, n_steps):
    tm = x_ref.shape[0]

    def start(buf, s, for_next, unrolled):
        _gather_start(y_hbm, nxt_ref if for_next else pos_ref, buf, s, 2 * tm, unrolled)

    def work(cur, start_next):
        start_next(True)
        w = w_ref[...]
        f = w[:, 0:1] * cur[0:tm, :] + w[:, 1:2] * cur[tm:2 * tm, :]
        o_ref[...] = x_ref[...] + g2_ref[0] * f

    _double_buffered(pl.program_id(0), n_steps, y_hbm, (yb0, yb1), sem, start, work)


def _moe_combine(pos, x_new, mod3, wts, y_sorted, n_batch, tiles_per_batch, ctx_tiles, keep_ctx, layer):
    rows, d = x_new.shape
    tm = ROW_TILE
    tpb = tiles_per_batch
    first = 0 if keep_ctx else ctx_tiles
    out_tpb = tpb - first
    ns = n_batch * out_tpb
    in_tile = lambda s: (s // out_tpb) * tpb + first + s % out_tpb
    mrow = lambda s: layer * MOD_ROWS + jnp.where(first + s % out_tpb < ctx_tiles, n_batch, s // out_tpb)
    pos3 = pos.reshape(rows // tm, tm, 2).transpose(0, 2, 1).reshape(rows // tm, 1, 2 * tm)
    idx = lambda off: pl.BlockSpec((1, 1, 2 * tm), lambda s: (in_tile(jnp.minimum(s + off, ns - 1)), 0, 0),
                                   memory_space=pltpu.SMEM)
    return pl.pallas_call(
        functools.partial(_combine_body, n_steps=ns),
        grid=(ns,),
        in_specs=[idx(0), idx(1),
                  pl.BlockSpec((tm, d), lambda s: (in_tile(s), 0)),
                  pl.BlockSpec((1, 1, d), lambda s: (mrow(s) * 6 + 5, 0, 0)),
                  pl.BlockSpec((tm, 2), lambda s: (in_tile(s), 0)),
                  pl.BlockSpec(memory_space=pl.ANY)],
        out_specs=pl.BlockSpec((tm, d), lambda s: (s, 0)),
        out_shape=jax.ShapeDtypeStruct((ns * tm, d), F32),
        scratch_shapes=[pltpu.VMEM((2 * tm, d), F32), pltpu.VMEM((2 * tm, d), F32),
                        pltpu.SemaphoreType.DMA((2,))],
        compiler_params=_cparams(("arbitrary",), 40),
        name="moe_combine",
    )(pos3, pos3, x_new, mod3, wts, y_sorted)


def _moe_plan(route, n_rows):
    tm = MOE_TILE
    e = route[:, 0:2, :].astype(I32).transpose(0, 2, 1).reshape(n_rows, 2)
    wts = route[:, 2:4, :].transpose(0, 2, 1).reshape(n_rows, 2)
    e_flat = e.reshape(-1)
    onehot = (e_flat[:, None] == jnp.arange(N_EXPERTS, dtype=I32)[None, :]).astype(I32)
    csum = jnp.cumsum(onehot, axis=0)
    rank = jnp.sum(onehot * csum, axis=1) - 1
    counts = csum[-1]
    tiles_e = (counts + tm - 1) // tm
    tile_end = jnp.cumsum(tiles_e)
    tile_start = tile_end - tiles_e
    pos = tile_start[e_flat] * tm + rank
    nt = (2 * n_rows) // tm + N_EXPERTS
    slot_token = jnp.zeros((nt * tm,), I32).at[pos].set(jnp.arange(2 * n_rows, dtype=I32) // 2,
                                                        unique_indices=True)
    n_active = tile_end[-1]
    tile_id = jnp.minimum(jnp.arange(nt, dtype=I32), n_active - 1)
    tile_expert = jnp.minimum(jnp.sum((tile_id[:, None] >= tile_end[None, :]).astype(I32), axis=1),
                              N_EXPERTS - 1).astype(I32)
    tile_valid = (jnp.arange(nt, dtype=I32) < n_active).astype(I32)
    return tile_expert, tile_valid, slot_token, pos.reshape(n_rows, 2), wts


def kernel(x, c, ctx, c_ctx, ada_w, ada_b, norm1, norm2, w_in, gla_w_alpha, gla_b_alpha, gla_o_norm,
           s5_a_re, s5_a_im, s5_log_dt, s5_b_re, s5_b_im, s5_c_re, s5_c_im, s5_d, s5_w_glu, s5_b_glu,
           swa_q_norm, swa_k_norm, swa_sink, w_out, router_w, router_b, exp_w_gate, exp_w_up, exp_w_down):
    n_batch, seq_len, d = x.shape
    ctx_len = ctx.shape[1]
    depth = ada_w.shape[0]
    tm = ROW_TILE
    per_batch = ctx_len + seq_len
    assert ctx_len % tm == 0 and seq_len % tm == 0
    tpb, ctx_tiles = per_batch // tm, ctx_len // tm
    rows = n_batch * per_batch

    hk = GLA_HEADS * GLA_DK
    gla_w = GLA_HEADS * GLA_DV
    s5_w = s5_d.shape[1]
    swa_w = SWA_HEADS * SWA_HEAD_DIM
    kv_w = SWA_KV_HEADS * SWA_HEAD_DIM
    o_q, o_k, o_v = 0, hk, 2 * hk
    o_lr = o_v + gla_w
    o_r = o_lr + 2 * GLA_RANK
    o_u = o_r + gla_w
    o_sq = o_u + s5_w
    o_sk = o_sq + swa_w
    outs = [(2 * hk, BF16), (gla_w, BF16), (gla_w, BF16), (s5_w, BF16), (swa_w, BF16), (2 * kv_w, BF16),
            (LANES, F32)]

    xall = jnp.concatenate([ctx, x], axis=1).reshape(rows, d)
    cvec = jnp.concatenate([c, c_ctx[None], jnp.zeros((8 - n_batch - 1, d), c.dtype)], axis=0)
    cos, sin = _rope_tables(seq_len)
    rw_t = jnp.zeros((LANES, d), F32).at[:N_EXPERTS].set(router_w.astype(F32).T)
    rb_col = jnp.zeros((LANES, 1), F32).at[:N_EXPERTS, 0].set(router_b.astype(F32))

    mod3 = _ada_mod(cvec, ada_w, ada_b).reshape(depth * MOD_ROWS * 6, 1, d)
    w_prep = jnp.concatenate(
        [w_in[:, :, o_q:o_v], w_in[:, :, o_v:o_lr], w_in[:, :, o_r:o_u], w_in[:, :, o_u:o_sq],
         w_in[:, :, o_sq:o_sk], w_in[:, :, o_sk:], w_in[:, :, o_lr:o_r],
         jnp.zeros((depth, d, LANES - 2 * GLA_RANK), w_in.dtype)], axis=2).astype(BF16)
    w_glu16, w_out16 = s5_w_glu.astype(BF16), w_out.astype(BF16)
    wg16, wu16, wd16 = exp_w_gate.astype(BF16), exp_w_up.astype(BF16), exp_w_down.astype(BF16)
    s5_mats = jax.vmap(_s5_prep)(s5_a_re, s5_a_im, s5_log_dt, s5_b_re, s5_b_im, s5_c_re, s5_c_im, s5_d)

    out = None
    for layer in range(depth):
        last = layer == depth - 1
        qk, gv, gr, u, sq, skv, lr = _in_proj(xall, mod3, norm1[layer], w_prep, outs, tpb, ctx_tiles, n_batch,
                                              layer)

        wa = jnp.zeros((LANES, 2 * hk), F32)
        wa = wa.at[0:GLA_RANK, 0:hk].set(gla_w_alpha[layer, 0].astype(F32))
        wa = wa.at[GLA_RANK:2 * GLA_RANK, hk:].set(gla_w_alpha[layer, 1].astype(F32))
        ba = gla_b_alpha[layer].astype(F32).reshape(1, 2 * hk)
        o_f, o_b = _gla_scan(qk, gv, lr, wa, ba, n_batch, tpb, ctx_tiles)

        ys = _s5_scan(u, s5_mats, n_batch, ctx_len, layer)

        sw = _swa_attend(sq, skv, cos, sin, swa_q_norm[layer], swa_k_norm[layer], swa_sink[layer],
                         n_batch, seq_len, ctx_len)

        x_new, h2, route = _out_proj(o_f, o_b, gr, ys, sw, xall, mod3, norm2[layer], gla_o_norm[layer],
                                     w_glu16, s5_b_glu[layer].astype(F32), w_out16, rw_t, rb_col,
                                     tpb, ctx_tiles, n_batch, layer)

        tile_expert, tile_valid, slot_token, pos, wts = _moe_plan(route, rows)
        y_sorted = _moe_experts(tile_expert, tile_valid, slot_token, h2, wg16, wu16, wd16, layer)
        out = _moe_combine(pos, x_new, mod3, wts, y_sorted, n_batch, tpb, ctx_tiles, keep_ctx=not last,
                           layer=layer)
        xall = out
    return out.reshape(n_batch, seq_len, d)
```

```python
import functools
import math

import jax
import jax.numpy as jnp
from jax import lax
from jax.experimental import pallas as pl
from jax.experimental.pallas import tpu as pltpu

F32 = jnp.float32
BF16 = jnp.bfloat16
I32 = jnp.int32

EPS = 1e-6
NEG_INF = -1e30
LOG2E = math.log2(math.e)

LANES = 128
VMEM_BYTES = 64 * 1024 * 1024

ROW_TILE = 256
MOD_ROWS = 8

GLA_HEADS = 4
GLA_DK = 64
GLA_DV = 128
GLA_RANK = 16
GLA_GATE_NORM = 16.0
GLA_CHUNK = 64

S5_GROUP = 16
S5_STATE = 64
S5_CHUNK = 8
S5_BLOCK_GROUPS = LANES // S5_GROUP

SWA_HEAD_DIM = 128
SWA_HEADS = 8
SWA_KV_HEADS = 2
SWA_GROUP = SWA_HEADS // SWA_KV_HEADS
SWA_BLOCK = 128
ROPE_PAIRS = 32
ROPE_BASE = 10000.0
GRID_W = 64

N_EXPERTS = 16
N_EXPERT_GROUPS = 4
EXPERTS_PER_GROUP = 4
MOE_TILE = 256
DMA_UNROLL = 8
GATHER_RING = 3


def _cparams(semantics, vmem_mb):
    return pltpu.CompilerParams(dimension_semantics=semantics,
                                vmem_limit_bytes=vmem_mb * 1024 * 1024)


def _mm(a, b):
    return jnp.dot(a.astype(BF16), b.astype(BF16), preferred_element_type=F32)


def _mm_nt(a, b):
    return lax.dot_general(a.astype(BF16), b.astype(BF16), (((1,), (1,)), ((), ())),
                           preferred_element_type=F32)


def _mm_tn(a, b):
    return lax.dot_general(a.astype(BF16), b.astype(BF16), (((0,), (0,)), ((), ())),
                           preferred_element_type=F32)


def _split(a, parts):
    out, rest = [], a
    for p in range(parts):
        piece = rest.astype(BF16)
        out.append(piece)
        if p + 1 < parts:
            rest = rest - piece.astype(F32)
    return out


def _mm_split(a, b, dims=(((1,), (0,)), ((), ())), parts=3, a_exact=False, b_exact=False):
    pa = [a.astype(BF16)] if a_exact else _split(a, parts)
    pb = [b.astype(BF16)] if b_exact else _split(b, parts)
    out = None
    for i, ai in enumerate(pa):
        for j, bj in enumerate(pb):
            if i + j < parts:
                t = lax.dot_general(ai, bj, dims, preferred_element_type=F32)
                out = t if out is None else out + t
    return out


def _sigmoid(x):
    return 1.0 / (1.0 + jnp.exp(-x))


def _silu(x):
    return x * _sigmoid(x)


def _log_sigmoid(x):
    return jnp.minimum(x, 0.0) - jnp.log(1.0 + jnp.exp(-jnp.abs(x)))


def _gelu_tanh(x):
    return 0.5 * x * (1.0 + jnp.tanh(math.sqrt(2.0 / math.pi) * (x + 0.044715 * (x * x * x))))


def _rms(x, gain):
    return x * lax.rsqrt(jnp.mean(x * x, axis=-1, keepdims=True) + EPS) * gain


def _ada_body(c_ref, w_ref, b_ref, o_ref):
    s = _silu(c_ref[...])
    o_ref[...] = _mm(s, w_ref[...]) + b_ref[...]


def _ada_mod(cvec, w, b):
    rows, d = cvec.shape
    depth, _, n = w.shape
    tn = 1024
    return pl.pallas_call(
        _ada_body,
        grid=(depth, n // tn),
        in_specs=[pl.BlockSpec((rows, d), lambda l, i: (0, 0)),
                  pl.BlockSpec((None, d, tn), lambda l, i: (l, 0, i)),
                  pl.BlockSpec((None, 1, tn), lambda l, i: (l, 0, i))],
        out_specs=pl.BlockSpec((None, rows, tn), lambda l, i: (l, 0, i)),
        out_shape=jax.ShapeDtypeStruct((depth, rows, n), F32),
        compiler_params=_cparams(("arbitrary", "arbitrary"), 40),
        name="ada_mod",
    )(cvec, w, b.reshape(depth, 1, n))


def _mod_row(i, tiles_per_batch, ctx_tiles, n_batch, layer):
    return layer * MOD_ROWS + jnp.where(i % tiles_per_batch < ctx_tiles, n_batch, i // tiles_per_batch)


def _proj_body(x_ref, sh_ref, sc_ref, g_ref, w_ref, *out_refs, splits):
    y = _rms(x_ref[...], g_ref[...])
    h = (y * (1.0 + sc_ref[0]) + sh_ref[0]).astype(BF16)
    for o_ref, (c0, width) in zip(out_refs, splits):
        o_ref[...] = jnp.dot(h, w_ref[:, c0:c0 + width],
                             preferred_element_type=F32).astype(o_ref.dtype)


def _in_proj(xall, mod3, gain, w, outs, tiles_per_batch, ctx_tiles, n_batch, layer):
    rows, d = xall.shape
    tm = ROW_TILE
    mrow = functools.partial(_mod_row, tiles_per_batch=tiles_per_batch, ctx_tiles=ctx_tiles,
                             n_batch=n_batch, layer=layer)
    splits, out_specs, out_shapes = [], [], []
    c0 = 0
    for width, dtype in outs:
        splits.append((c0, width))
        out_specs.append(pl.BlockSpec((tm, width), lambda i: (i, 0)))
        out_shapes.append(jax.ShapeDtypeStruct((rows, width), dtype))
        c0 += width
    return pl.pallas_call(
        functools.partial(_proj_body, splits=tuple(splits)),
        grid=(rows // tm,),
        in_specs=[pl.BlockSpec((tm, d), lambda i: (i, 0)),
                  pl.BlockSpec((1, 1, d), lambda i: (mrow(i) * 6 + 0, 0, 0)),
                  pl.BlockSpec((1, 1, d), lambda i: (mrow(i) * 6 + 1, 0, 0)),
                  pl.BlockSpec((1, d), lambda i: (0, 0)),
                  pl.BlockSpec((None,) + w.shape[1:], lambda i: (layer, 0, 0),
                               pipeline_mode=pl.Buffered(1))],
        out_specs=out_specs,
        out_shape=out_shapes,
        compiler_params=_cparams(("arbitrary",), 48),
        name="in_proj",
    )(xall, mod3, mod3, gain.reshape(1, d), w)


def _gla_body(qkf, vf, lrf, qkb, vb, lrb, wa_ref, ba_ref, of_ref, ob_ref, sf_ref, sb_ref):
    t = pl.program_id(1)
    ck = GLA_CHUNK
    hk = GLA_HEADS * GLA_DK

    @pl.when(t == 0)
    def _init():
        sf_ref[...] = jnp.zeros_like(sf_ref)
        sb_ref[...] = jnp.zeros_like(sb_ref)

    tmr = qkf.shape[0]
    n_chunks = tmr // ck
    rowb = lax.broadcasted_iota(I32, (tmr, tmr), 0)
    colb = lax.broadcasted_iota(I32, (tmr, tmr), 1)
    same_chunk = rowb // ck == colb // ck
    row4 = lax.broadcasted_iota(I32, (GLA_HEADS * ck, ck), 0) % ck
    col4 = lax.broadcasted_iota(I32, (GLA_HEADS * ck, ck), 1)
    lane = lax.broadcasted_iota(I32, (1, hk), 1)
    head_masks = [(lane // GLA_DK == h).astype(F32) for h in range(GLA_HEADS)]
    sel = (lax.broadcasted_iota(I32, (tmr, n_chunks * GLA_DV), 0) // ck
           == lax.broadcasted_iota(I32, (tmr, n_chunks * GLA_DV), 1) // GLA_DV).astype(F32)

    def run(qk_ref, v_ref, lr_ref, w, bias, s_ref, o_ref, reverse):
        tri = (same_chunk & (rowb <= colb if reverse else rowb >= colb)).astype(F32)
        amask = row4 <= col4 if reverse else row4 >= col4
        la = _log_sigmoid(_mm_split(lr_ref[...], w, parts=2) + bias) * (1.0 / GLA_GATE_NORM)
        b = _mm_split(tri, la, a_exact=True)
        decay_all = jnp.exp(_mm_split(la, sel, dims=(((0,), (0,)), ((), ())), b_exact=True))
        q = qk_ref[:, 0:hk].astype(F32)
        k = qk_ref[:, hk:2 * hk].astype(F32)
        qd_all = q * (GLA_DK ** -0.5) * jnp.exp(b)
        kd_all = k * jnp.exp(-b)
        state = s_ref[...]
        order = range(n_chunks - 1, -1, -1) if reverse else range(n_chunks)
        for ci in order:
            rows = slice(ci * ck, (ci + 1) * ck)
            v = v_ref[rows, :]
            last = ci * ck if reverse else (ci + 1) * ck - 1
            ke = k[rows, :] * jnp.exp(b[last:last + 1, :] - b[rows, :])
            qd = qd_all[rows, :]
            qs = jnp.concatenate([qd * m for m in head_masks], axis=0).astype(BF16)
            att = jnp.where(amask, _mm_nt(qs, kd_all[rows, :]), 0.0)
            inter = _mm(qs, state)
            outs = []
            for h in range(GLA_HEADS):
                hr = slice(h * ck, (h + 1) * ck)
                outs.append(_mm(att[hr, :], v[:, h * GLA_DV:(h + 1) * GLA_DV]) + inter[hr, :])
            o_ref[rows, :] = jnp.concatenate(outs, axis=1)
            ds = _mm_tn(ke, v)
            ds = jnp.concatenate([ds[h * GLA_DK:(h + 1) * GLA_DK, h * GLA_DV:(h + 1) * GLA_DV]
                                  for h in range(GLA_HEADS)], axis=0)
            state = decay_all[:, ci * GLA_DV:(ci + 1) * GLA_DV] * state + ds
        s_ref[...] = state

    run(qkf, vf, lrf, wa_ref[:, 0:hk], ba_ref[:, 0:hk], sf_ref, of_ref, False)
    run(qkb, vb, lrb, wa_ref[:, hk:2 * hk], ba_ref[:, hk:2 * hk], sb_ref, ob_ref, True)


def _gla_scan(qk, v, lr, wa, ba, n_batch, tiles_per_batch, ctx_tiles):
    rows = qk.shape[0]
    tm = ROW_TILE
    tpb = tiles_per_batch

    def fwd(b, t):
        return (b * tpb + t, 0)

    def bwd(b, t):
        return (b * tpb + jnp.where(t < ctx_tiles, ctx_tiles - 1 - t, tpb - 1 - (t - ctx_tiles)), 0)

    wv = v.shape[1]
    return pl.pallas_call(
        _gla_body,
        grid=(n_batch, tpb),
        in_specs=[pl.BlockSpec((tm, qk.shape[1]), fwd), pl.BlockSpec((tm, wv), fwd),
                  pl.BlockSpec((tm, lr.shape[1]), fwd),
                  pl.BlockSpec((tm, qk.shape[1]), bwd), pl.BlockSpec((tm, wv), bwd),
                  pl.BlockSpec((tm, lr.shape[1]), bwd),
                  pl.BlockSpec(wa.shape, lambda b, t: (0, 0)),
                  pl.BlockSpec(ba.shape, lambda b, t: (0, 0))],
        out_specs=[pl.BlockSpec((tm, wv), fwd), pl.BlockSpec((tm, wv), bwd)],
        out_shape=[jax.ShapeDtypeStruct((rows, wv), F32)] * 2,
        scratch_shapes=[pltpu.VMEM((GLA_HEADS * GLA_DK, GLA_DV), F32)] * 2,
        compiler_params=_cparams(("arbitrary", "arbitrary"), 32),
        name="gla_scan",
    )(qk, v, lr, qk, v, lr, wa, ba)


def _s5_prep(a_re, a_im, log_dt, b_re, b_im, c_re, c_im, d):
    hp = lax.Precision.HIGHEST
    tt = S5_CHUNK
    n_groups = a_re.shape[1]
    bg = S5_BLOCK_GROUPS
    nb = n_groups // bg
    a_re, a_im, b_re, b_im = a_re.astype(F32), a_im.astype(F32), b_re.astype(F32), b_im.astype(F32)
    c_re, c_im = c_re.astype(F32), c_im.astype(F32)
    dt = jnp.exp(log_dt.astype(F32))[..., None]
    mag = jnp.exp(dt * a_re)
    ab_re, ab_im = mag * jnp.cos(dt * a_im), mag * jnp.sin(dt * a_im)
    den = a_re * a_re + a_im * a_im
    nr = ab_re - 1.0
    coef_re = (nr * a_re + ab_im * a_im) / den
    coef_im = (ab_im * a_re - nr * a_im) / den
    bb_re = coef_re[..., None] * b_re - coef_im[..., None] * b_im
    bb_im = coef_re[..., None] * b_im + coef_im[..., None] * b_re
    pr, pi = [jnp.ones_like(ab_re)], [jnp.zeros_like(ab_im)]
    for _ in range(tt):
        pr, pi = pr + [pr[-1] * ab_re - pi[-1] * ab_im], pi + [pr[-1] * ab_im + pi[-1] * ab_re]
    pr, pi = jnp.stack(pr), jnp.stack(pi)
    cpr = c_re[None] * pr[:, :, :, None, :] - c_im[None] * pi[:, :, :, None, :]
    cpi = c_re[None] * pi[:, :, :, None, :] + c_im[None] * pr[:, :, :, None, :]
    kern = (jnp.einsum('kdgcn,dgnm->kdgcm', cpr, bb_re, precision=hp)
            - jnp.einsum('kdgcn,dgnm->kdgcm', cpi, bb_im, precision=hp))
    kdim, half = tt * LANES, bg * S5_STATE
    ax_k, ax_s, ax_tc = jnp.arange(kdim), jnp.arange(half), jnp.arange(tt * S5_GROUP)
    k_group, k_step, k_chan = (ax_k // S5_GROUP) % bg, ax_k // LANES, ax_k % S5_GROUP
    rep_tc = ((ax_tc[:, None] // S5_GROUP == k_step[None, :])
              & (ax_tc[:, None] % S5_GROUP == k_chan[None, :])).astype(F32)
    rep_n = (jnp.arange(S5_STATE)[:, None] == ax_s[None, :] % S5_STATE).astype(F32)
    mask_kk = k_group[:, None] == k_group[None, :]
    mask_ks = k_group[:, None] == ax_s[None, :] // S5_STATE
    expand = lambda table, rep: jnp.einsum('jrk,kc->jrc', table, rep, precision=hp)

    s_idx = jnp.arange(tt)[:, None]
    t_idx = jnp.arange(tt)[None, :]
    lag_f = jnp.clip(t_idx - s_idx, 0, tt)
    lag_b = jnp.clip(s_idx - t_idx, 0, tt)
    toe = (jnp.where((t_idx >= s_idx)[:, :, None, None, None], kern[lag_f, 0], 0.0)
           + jnp.where((s_idx >= t_idx)[:, :, None, None, None], kern[lag_b, 1], 0.0))
    skip = (jnp.eye(tt, dtype=F32)[:, :, None, None, None]
            * d.astype(F32).reshape(n_groups, S5_GROUP)[None, None, :, :, None]
            * jnp.eye(S5_GROUP, dtype=F32)[None, None, None])
    toe = (toe + skip).reshape(tt, tt, nb, bg, S5_GROUP, S5_GROUP)
    toe = toe.transpose(2, 0, 3, 5, 1, 4).reshape(nb, kdim, tt * S5_GROUP)
    m_mat = jnp.where(mask_kk, expand(toe, rep_tc), 0.0)

    def inject(direction, power_of_s):
        p_re, p_im = pr[power_of_s, direction], pi[power_of_s, direction]
        v_re = p_re[..., None] * bb_re[direction][None] - p_im[..., None] * bb_im[direction][None]
        v_im = p_re[..., None] * bb_im[direction][None] + p_im[..., None] * bb_re[direction][None]
        def blk(v):
            v = v.reshape(tt, nb, bg, S5_STATE, S5_GROUP).transpose(1, 0, 2, 4, 3).reshape(nb, kdim, S5_STATE)
            return jnp.where(mask_ks, expand(v, rep_n), 0.0)
        return jnp.concatenate([blk(v_re), blk(v_im)], axis=-1)

    def readout(direction, power_of_t):
        e_re, e_im = cpr[power_of_t, direction], -cpi[power_of_t, direction]
        def blk(v):
            v = v.reshape(tt, nb, bg, S5_GROUP, S5_STATE).transpose(1, 2, 4, 0, 3).reshape(nb, half, tt * S5_GROUP)
            return jnp.where(mask_ks.T, expand(v, rep_tc), 0.0)
        return jnp.concatenate([blk(e_re), blk(e_im)], axis=1)

    steps = jnp.arange(tt)
    g_f = inject(0, tt - 1 - steps)
    g_b = inject(1, steps)
    e_f = readout(0, steps + 1)
    e_b = readout(1, tt - steps)
    a_t = jnp.stack([pr[tt, 0].reshape(nb, half), pi[tt, 0].reshape(nb, half),
                     pr[tt, 1].reshape(nb, half), pi[tt, 1].reshape(nb, half)], axis=1)
    cast = lambda z: z.astype(BF16)
    return cast(m_mat), cast(g_f), cast(g_b), cast(e_f), cast(e_b), a_t


def _s5_body(u_ref, m_ref, gf_ref, gb_ref, ef_ref, eb_ref, a_ref, y_ref,
             us_ref, gfs_ref, gbs_ref, sfs_ref, sbs_ref, *, ctx_rows, width):
    j = pl.program_id(1)
    tt = S5_CHUNK
    nb = width // LANES
    n_rows = u_ref.shape[0]
    half = S5_BLOCK_GROUPS * S5_STATE

    for jj in range(nb):
        @pl.when(j == jj)
        def _gather(jj=jj):
            for s in range(tt):
                us_ref[:, s * LANES:(s + 1) * LANES] = u_ref[:, s * width + jj * LANES:s * width + (jj + 1) * LANES]

    ustack = us_ref[...]
    gfs_ref[...] = jnp.dot(ustack, gf_ref[0], preferred_element_type=F32)
    gbs_ref[...] = jnp.dot(ustack, gb_ref[0], preferred_element_type=F32)

    a = a_ref[0]
    afr, afi, abr, abi = a[0:1, :], a[1:2, :], a[2:3, :], a[3:4, :]

    def step(i, carry):
        fr, fi, br, bi = carry
        sfs_ref[pl.ds(i, 1), 0:half] = fr
        sfs_ref[pl.ds(i, 1), half:2 * half] = fi
        g = gfs_ref[pl.ds(i, 1), :]
        nfr = afr * fr - afi * fi + g[:, 0:half]
        nfi = afr * fi + afi * fr + g[:, half:2 * half]
        rb = jnp.where(i < ctx_rows, ctx_rows - 1 - i, n_rows - 1 - (i - ctx_rows))
        sbs_ref[pl.ds(rb, 1), 0:half] = br
        sbs_ref[pl.ds(rb, 1), half:2 * half] = bi
        g = gbs_ref[pl.ds(rb, 1), :]
        nbr = abr * br - abi * bi + g[:, 0:half]
        nbi = abr * bi + abi * br + g[:, half:2 * half]
        return nfr, nfi, nbr, nbi

    zero = jnp.zeros((1, half), F32)
    lax.fori_loop(0, n_rows, step, (zero, zero, zero, zero))

    y = (jnp.dot(ustack, m_ref[0], preferred_element_type=F32)
         + jnp.dot(sfs_ref[...].astype(BF16), ef_ref[0], preferred_element_type=F32)
         + jnp.dot(sbs_ref[...].astype(BF16), eb_ref[0], preferred_element_type=F32))
    for jj in range(nb):
        @pl.when(j == jj)
        def _scatter(jj=jj):
            for s in range(tt):
                y_ref[:, s * width + jj * LANES:s * width + (jj + 1) * LANES] = (
                    y[:, s * LANES:(s + 1) * LANES].astype(y_ref.dtype))


def _s5_scan(u, mats, n_batch, ctx_len, layer):
    m_mat, g_f, g_b, e_f, e_b, a_t = mats
    rows, width = u.shape
    tt = S5_CHUNK
    nb = width // LANES
    srows = rows // tt // n_batch
    u2 = u.reshape(rows // tt, tt * width)
    kdim = tt * LANES
    sdim = 2 * S5_BLOCK_GROUPS * S5_STATE
    wspec = lambda shape: pl.BlockSpec((None, 1) + shape, lambda b, j: (layer, j, 0, 0))
    y2 = pl.pallas_call(
        functools.partial(_s5_body, ctx_rows=ctx_len // tt, width=width),
        grid=(n_batch, nb),
        in_specs=[pl.BlockSpec((srows, tt * width), lambda b, j: (b, 0), pipeline_mode=pl.Buffered(1)),
                  wspec((kdim, kdim)), wspec((kdim, sdim)), wspec((kdim, sdim)),
                  wspec((sdim, kdim)), wspec((sdim, kdim)), wspec((4, sdim // 2))],
        out_specs=pl.BlockSpec((srows, tt * width), lambda b, j: (b, 0)),
        out_shape=jax.ShapeDtypeStruct((rows // tt, tt * width), BF16),
        scratch_shapes=[pltpu.VMEM((srows, kdim), BF16),
                        pltpu.VMEM((srows, sdim), F32), pltpu.VMEM((srows, sdim), F32),
                        pltpu.VMEM((srows, sdim), F32), pltpu.VMEM((srows, sdim), F32)],
        compiler_params=_cparams(("arbitrary", "arbitrary"), 56),
        name="s5_scan",
    )(u2, m_mat, g_f, g_b, e_f, e_b, a_t)
    return y2.reshape(rows, width)


def _rope_tables(seq_len):
    pos = jnp.arange(seq_len)
    row = (pos // GRID_W).astype(F32)
    col = (pos % GRID_W).astype(F32)
    inv_freq = ROPE_BASE ** (-jnp.arange(ROPE_PAIRS, dtype=F32) / ROPE_PAIRS)
    ar, ac = row[:, None] * inv_freq, col[:, None] * inv_freq
    cos = jnp.concatenate([jnp.cos(ar), jnp.cos(ar), jnp.cos(ac), jnp.cos(ac)], axis=1)
    sin = jnp.concatenate([-jnp.sin(ar), jnp.sin(ar), -jnp.sin(ac), jnp.sin(ac)], axis=1)
    return cos, sin


def _swa_body(sink_ref, q_ref, kp_ref, kc_ref, kn_ref, kx_ref, cp_ref, sp_ref, cc_ref, sc_ref,
              cn_ref, sn_ref, qg_ref, kg_ref, o_ref, *, lat_blocks):
    n = pl.program_id(1)
    hd = SWA_HEAD_DIM
    blk = SWA_BLOCK
    kvw = SWA_KV_HEADS * hd
    scale = hd ** -0.5
    row = lax.broadcasted_iota(I32, (SWA_GROUP * blk, blk), 0) % blk
    col = lax.broadcasted_iota(I32, (SWA_GROUP * blk, blk), 1)
    qg, kg = qg_ref[...], kg_ref[...]
    ones_hd = jnp.ones((hd, hd), BF16)
    src = lax.broadcasted_iota(I32, (hd, hd), 0)
    dst = lax.broadcasted_iota(I32, (hd, hd), 1)
    partner = jnp.where(dst % (2 * ROPE_PAIRS) < ROPE_PAIRS, dst + ROPE_PAIRS, dst - ROPE_PAIRS)
    swap_halves = (src == partner).astype(BF16)

    def head_rms(x, gain):
        ss = _mm_split(x * x, ones_hd, parts=2, b_exact=True)
        return x * lax.rsqrt(ss * (1.0 / hd) + EPS) * gain

    def rope(x, cos, sin):
        return x * cos + _mm_split(x, swap_halves, parts=2, b_exact=True) * sin

    def lane_fold(op, blocks):
        parts = [b[:, c0:c0 + blk] for b in blocks for c0 in range(0, b.shape[1], blk)]
        out = parts[0]
        for part in parts[1:]:
            out = op(out, part)
        return out

    def attend(is_lat):
        for g in range(SWA_KV_HEADS):
            kcols = slice(g * hd, (g + 1) * hd)
            vcols = slice(kvw + g * hd, kvw + (g + 1) * hd)
            keys = [head_rms(kx_ref[:, kcols].astype(F32), kg).astype(BF16)]
            values = [kx_ref[:, vcols]]
            valid = [None]
            if is_lat:
                wins = ((kp_ref, cp_ref, sp_ref), (kc_ref, cc_ref, sc_ref), (kn_ref, cn_ref, sn_ref))
                keys += [rope(head_rms(r[:, kcols].astype(F32), kg), c[...], s[...]).astype(BF16)
                         for r, c, s in wins]
                values += [r[:, vcols] for r, _, _ in wins]
                valid += [(col >= row) & (col + (n - 1) * blk >= 0), None,
                          (col <= row) & (col + (n + 1) * blk < lat_blocks * blk)]
            qs, sinks = [], []
            for hh in range(SWA_GROUP):
                h = g * SWA_GROUP + hh
                q = head_rms(q_ref[:, h * hd:(h + 1) * hd].astype(F32), qg)
                if is_lat:
                    q = rope(q, cc_ref[...], sc_ref[...])
                qs.append((q * (scale * LOG2E)).astype(BF16))
                sinks.append(jnp.full((blk, 1), sink_ref[h] * LOG2E, F32))
            q4 = jnp.concatenate(qs, axis=0)
            sink = jnp.concatenate(sinks, axis=0)
            scores = []
            for kk, ok in zip(keys, valid):
                s = _mm_nt(q4, kk)
                scores.append(s if ok is None else jnp.where(ok, s, NEG_INF))
            m = jnp.maximum(jnp.max(lane_fold(jnp.maximum, scores), axis=-1, keepdims=True), sink)
            probs = [jnp.exp2(s - m).astype(BF16) for s in scores]
            den = jnp.exp2(sink - m)
            for p in probs:
                for c0 in range(0, p.shape[1], blk):
                    den = den + _mm(p[:, c0:c0 + blk], ones_hd)
            acc = None
            for p, vv in zip(probs, values):
                pv = _mm(p, vv)
                acc = pv if acc is None else acc + pv
            out = acc * (1.0 / den)
            for hh in range(SWA_GROUP):
                h = g * SWA_GROUP + hh
                o_ref[:, h * hd:(h + 1) * hd] = out[hh * blk:(hh + 1) * blk, :].astype(o_ref.dtype)

    @pl.when(n < lat_blocks)
    def _lat():
        attend(True)

    @pl.when(n >= lat_blocks)
    def _ctx():
        attend(False)


def _swa_attend(q, kv, cos, sin, q_gain, k_gain, sink, n_batch, seq_len, ctx_len):
    rows = q.shape[0]
    blk = SWA_BLOCK
    lat_blocks = seq_len // blk
    ctx_blocks = ctx_len // blk
    bpb = lat_blocks + ctx_blocks
    hd = SWA_HEAD_DIM

    def q_map(b, n, s):
        return (b * bpb + jnp.where(n < lat_blocks, ctx_blocks + n, n - lat_blocks), 0)

    def win(off):
        def kv_map(b, n, s):
            return (b * bpb + ctx_blocks + jnp.clip(n + off, 0, lat_blocks - 1), 0)

        def tab_map(b, n, s):
            return (jnp.clip(n + off, 0, lat_blocks - 1), 0)
        return kv_map, tab_map

    (kv_p, tab_p), (kv_c, tab_c), (kv_n, tab_n) = win(-1), win(0), win(1)
    kvs = lambda m: pl.BlockSpec((blk, kv.shape[1]), m)
    tab = lambda m: pl.BlockSpec((blk, hd), m)
    grid_spec = pltpu.PrefetchScalarGridSpec(
        num_scalar_prefetch=1,
        grid=(n_batch, bpb),
        in_specs=[pl.BlockSpec((blk, q.shape[1]), q_map),
                  kvs(kv_p), kvs(kv_c), kvs(kv_n),
                  pl.BlockSpec((ctx_len, kv.shape[1]), lambda b, n, s: (b * (bpb * blk // ctx_len), 0)),
                  tab(tab_p), tab(tab_p), tab(tab_c), tab(tab_c), tab(tab_n), tab(tab_n),
                  pl.BlockSpec((1, hd), lambda b, n, s: (0, 0)),
                  pl.BlockSpec((1, hd), lambda b, n, s: (0, 0))],
        out_specs=pl.BlockSpec((blk, q.shape[1]), q_map),
    )
    return pl.pallas_call(
        functools.partial(_swa_body, lat_blocks=lat_blocks),
        grid_spec=grid_spec,
        out_shape=jax.ShapeDtypeStruct((rows, q.shape[1]), BF16),
        compiler_params=_cparams(("arbitrary", "arbitrary"), 32),
        name="swa_attend",
    )(sink.astype(F32), q, kv, kv, kv, kv, cos, sin, cos, sin, cos, sin,
      q_gain.reshape(1, hd).astype(F32), k_gain.reshape(1, hd).astype(F32))


def _route_rows(logits_t, bias_col):
    aff = _sigmoid(logits_t)
    biased = aff + bias_col
    v = [biased[e:e + 1, :] for e in range(N_EXPERTS)]
    a = [aff[e:e + 1, :] for e in range(N_EXPERTS)]
    gsz = EXPERTS_PER_GROUP
    best_g = best_s = None
    for g in range(N_EXPERT_GROUPS):
        vg = v[g * gsz:(g + 1) * gsz]
        score = None
        for i in range(gsz):
            for k in range(i + 1, gsz):
                pair = vg[i] + vg[k]
                score = pair if score is None else jnp.maximum(score, pair)
        if best_g is None:
            best_g, best_s = jnp.zeros_like(score, dtype=I32), score
        else:
            upd = score > best_s
            best_g = jnp.where(upd, g, best_g)
            best_s = jnp.where(upd, score, best_s)

    def pick(rows):
        out = []
        for i in range(gsz):
            x = rows[i]
            for g in range(1, N_EXPERT_GROUPS):
                x = jnp.where(best_g == g, rows[g * gsz + i], x)
            out.append(x)
        return out

    vb, ab = pick(v), pick(a)
    i1, v1, a1 = jnp.zeros_like(best_g), vb[0], ab[0]
    for i in range(1, gsz):
        upd = vb[i] > v1
        i1, v1, a1 = jnp.where(upd, i, i1), jnp.where(upd, vb[i], v1), jnp.where(upd, ab[i], a1)
    i2 = v2 = a2 = None
    for i in range(gsz):
        cand = jnp.where(i1 == i, -jnp.inf, vb[i])
        if i2 is None:
            i2, v2, a2 = jnp.zeros_like(best_g), cand, ab[0]
        else:
            upd = cand > v2
            i2, v2, a2 = jnp.where(upd, i, i2), jnp.where(upd, cand, v2), jnp.where(upd, ab[i], a2)
    tot = a1 + a2
    e1 = (best_g * gsz + i1).astype(F32)
    e2 = (best_g * gsz + i2).astype(F32)
    zeros = jnp.zeros((4, e1.shape[1]), F32)
    return jnp.concatenate([e1, e2, a1 / tot, a2 / tot, zeros], axis=0)


def _out_body(of_ref, ob_ref, r_ref, ys_ref, sw_ref, x_ref, g1_ref, sh_ref, sc_ref, n2_ref,
              og_ref, wglu_ref, bglu_ref, wo_ref, rw_ref, rb_ref, xo_ref, h_ref, rt_ref):
    o = of_ref[...] + ob_ref[...]
    r = r_ref[...].astype(F32)
    gain = og_ref[...]
    heads = []
    for h in range(GLA_HEADS):
        cols = slice(h * GLA_DV, (h + 1) * GLA_DV)
        heads.append(_rms(o[:, cols], gain[:, cols]))
    gla_y = jnp.concatenate(heads, axis=1) * _silu(r)
    z = _gelu_tanh(ys_ref[...].astype(F32))
    s5_y = z * _sigmoid(_mm(z, wglu_ref[...]) + bglu_ref[...])
    w1 = gla_y.shape[1]
    w2 = w1 + s5_y.shape[1]
    y = (_mm(gla_y, wo_ref[0:w1, :]) + _mm(s5_y, wo_ref[w1:w2, :])
         + jnp.dot(sw_ref[...], wo_ref[w2:, :], preferred_element_type=F32))
    x_new = x_ref[...] + g1_ref[0] * y
    xo_ref[...] = x_new
    h2 = _rms(x_new, n2_ref[...]) * (1.0 + sc_ref[0]) + sh_ref[0]
    h_ref[...] = h2
    logits_t = _mm_split(rw_ref[...], h2, dims=(((1,), (1,)), ((), ())), parts=2)
    rt_ref[0] = _route_rows(logits_t, rb_ref[...])


def _out_proj(o_f, o_b, r, ys, sw, xall, mod3, norm2, o_gain, w_glu, b_glu, w_out, rw_t, rb_col,
              tiles_per_batch, ctx_tiles, n_batch, layer):
    rows, d = xall.shape
    tm = ROW_TILE
    nt = rows // tm
    mrow = functools.partial(_mod_row, tiles_per_batch=tiles_per_batch, ctx_tiles=ctx_tiles,
                             n_batch=n_batch, layer=layer)
    rowspec = lambda a: pl.BlockSpec((tm, a.shape[1]), lambda i: (i, 0))
    const = lambda a: pl.BlockSpec(a.shape, lambda i: (0,) * a.ndim)
    layered = lambda a, **kw: pl.BlockSpec((None,) + a.shape[1:], lambda i: (layer,) + (0,) * (a.ndim - 1), **kw)
    modspec = lambda k: pl.BlockSpec((1, 1, d), lambda i: (mrow(i) * 6 + k, 0, 0))
    consts = [norm2.reshape(1, d), o_gain.reshape(1, -1)]
    return pl.pallas_call(
        _out_body,
        grid=(nt,),
        in_specs=[rowspec(o_f), rowspec(o_b), rowspec(r), rowspec(ys), rowspec(sw), rowspec(xall),
                  modspec(2), modspec(3), modspec(4)]
                 + [const(a) for a in consts]
                 + [layered(w_glu), const(b_glu.reshape(1, -1)),
                    layered(w_out, pipeline_mode=pl.Buffered(1)), const(rw_t), const(rb_col)],
        out_specs=[pl.BlockSpec((tm, d), lambda i: (i, 0)), pl.BlockSpec((tm, d), lambda i: (i, 0)),
                   pl.BlockSpec((1, 8, tm), lambda i: (i, 0, 0))],
        out_shape=[jax.ShapeDtypeStruct((rows, d), F32), jax.ShapeDtypeStruct((rows, d), F32),
                   jax.ShapeDtypeStruct((nt, 8, tm), F32)],
        compiler_params=_cparams(("arbitrary",), 48),
        name="out_proj",
    )(o_f, o_b, r, ys, sw, xall, mod3, mod3, mod3, *consts, w_glu, b_glu.reshape(1, -1), w_out, rw_t, rb_col)


def _row_copy(src_hbm, row, dst, slot, sem):
    return pltpu.make_async_copy(src_hbm.at[pl.ds(row, 1)], dst.at[pl.ds(slot, 1)], sem)


def _gather_start(src_hbm, idx_ref, dst, sem, n_rows, unrolled=False):
    if unrolled:
        for r in range(n_rows):
            _row_copy(src_hbm, idx_ref[0, 0, r], dst, r, sem).start()
        return

    def body(g, carry):
        for u in range(DMA_UNROLL):
            r = g * DMA_UNROLL + u
            _row_copy(src_hbm, idx_ref[0, 0, r], dst, r, sem).start()
        return carry
    lax.fori_loop(0, n_rows // DMA_UNROLL, body, 0)


def _gather_wait(src_hbm, dst, sem, n_rows):
    for r in range(n_rows):
        _row_copy(src_hbm, 0, dst, r, sem).wait()


def _ring_gather(step, n_steps, src_hbm, idx_refs, ring, sem, work):
    depth = GATHER_RING
    n_rows = ring.shape[1]

    @pl.when(step == 0)
    def _prime():
        for k in range(depth - 1):
            _gather_start(src_hbm, idx_refs[k], ring.at[k], sem.at[k], n_rows)

    slot = step % depth
    ahead = (step + depth - 1) % depth
    _gather_wait(src_hbm, ring.at[slot], sem.at[slot], n_rows)
    work(ring.at[slot],
         lambda unrolled: _gather_start(src_hbm, idx_refs[depth - 1], ring.at[ahead], sem.at[ahead], n_rows,
                                        unrolled))

    @pl.when(step == n_steps - 1)
    def _drain():
        for k in range(1, depth):
            late = (step + k) % depth
            _gather_wait(src_hbm, ring.at[late], sem.at[late], n_rows)


def _moe_body(te_ref, tv_ref, idx0_ref, idx1_ref, idx2_ref, x_hbm, wg_ref, wu_ref, wd_ref, o_ref, ring, sem,
              *, n_steps):
    i = pl.program_id(0)

    def work(cur, start_next):
        @pl.when(tv_ref[i] > 0)
        def _active():
            start_next(True)
            x = cur[...].astype(BF16)
            gate = jnp.dot(x, wg_ref[...], preferred_element_type=F32)
            up = jnp.dot(x, wu_ref[...], preferred_element_type=F32)
            act = (_silu(gate) * up).astype(BF16)
            o_ref[...] = jnp.dot(act, wd_ref[...], preferred_element_type=F32)

        @pl.when(tv_ref[i] == 0)
        def _idle():
            start_next(False)
            o_ref[...] = jnp.zeros_like(o_ref)

    _ring_gather(i, n_steps, x_hbm, (idx0_ref, idx1_ref, idx2_ref), ring, sem, work)


def _moe_experts(tile_expert, tile_valid, slot_token, tokens, w_gate, w_up, w_down, layer):
    tm = MOE_TILE
    nt = tile_expert.shape[0]
    d = tokens.shape[1]
    de = w_gate.shape[3]
    wspec = lambda k, n: pl.BlockSpec((None, None, k, n), lambda i, te, tv: (layer, te[i], 0, 0))
    idx = lambda off: pl.BlockSpec((1, 1, tm), lambda i, te, tv: (jnp.minimum(i + off, nt - 1), 0, 0),
                                   memory_space=pltpu.SMEM)
    grid_spec = pltpu.PrefetchScalarGridSpec(
        num_scalar_prefetch=2,
        grid=(nt,),
        in_specs=[idx(k) for k in range(GATHER_RING)]
                 + [pl.BlockSpec(memory_space=pl.ANY), wspec(d, de), wspec(d, de), wspec(de, d)],
        out_specs=pl.BlockSpec((tm, d), lambda i, te, tv: (i, 0)),
        scratch_shapes=[pltpu.VMEM((GATHER_RING, tm, d), F32), pltpu.SemaphoreType.DMA((GATHER_RING,))],
    )
    slots = slot_token.reshape(nt, 1, tm)
    return pl.pallas_call(
        functools.partial(_moe_body, n_steps=nt),
        grid_spec=grid_spec,
        out_shape=jax.ShapeDtypeStruct((nt * tm, d), F32),
        compiler_params=_cparams(("arbitrary",), 56),
        name="moe_experts",
    )(tile_expert, tile_valid, *([slots] * GATHER_RING), tokens, w_gate, w_up, w_down)


def _combine_body(pos0_ref, pos1_ref, pos2_ref, x_ref, g2_ref, w_ref, y_hbm, o_ref, ring, sem, *, n_steps):
    tm = x_ref.shape[0]

    def work(cur, start_next):
        start_next(True)
        w = w_ref[...]
        f = w[:, 0:1] * cur[0:tm, :] + w[:, 1:2] * cur[tm:2 * tm, :]
        o_ref[...] = x_ref[...] + g2_ref[0] * f

    _ring_gather(pl.program_id(0), n_steps, y_hbm, (pos0_ref, pos1_ref, pos2_ref), ring, sem, work)


def _moe_combine(pos, x_new, mod3, wts, y_sorted, n_batch, tiles_per_batch, ctx_tiles, keep_ctx, layer):
    rows, d = x_new.shape
    tm = ROW_TILE
    tpb = tiles_per_batch
    first = 0 if keep_ctx else ctx_tiles
    out_tpb = tpb - first
    ns = n_batch * out_tpb
    in_tile = lambda s: (s // out_tpb) * tpb + first + s % out_tpb
    mrow = lambda s: layer * MOD_ROWS + jnp.where(first + s % out_tpb < ctx_tiles, n_batch, s // out_tpb)
    pos3 = pos.reshape(rows // tm, tm, 2).transpose(0, 2, 1).reshape(rows // tm, 1, 2 * tm)
    idx = lambda off: pl.BlockSpec((1, 1, 2 * tm), lambda s: (in_tile(jnp.minimum(s + off, ns - 1)), 0, 0),
                                   memory_space=pltpu.SMEM)
    return pl.pallas_call(
        functools.partial(_combine_body, n_steps=ns),
        grid=(ns,),
        in_specs=[idx(k) for k in range(GATHER_RING)] + [
                  pl.BlockSpec((tm, d), lambda s: (in_tile(s), 0)),
                  pl.BlockSpec((1, 1, d), lambda s: (mrow(s) * 6 + 5, 0, 0)),
                  pl.BlockSpec((tm, 2), lambda s: (in_tile(s), 0)),
                  pl.BlockSpec(memory_space=pl.ANY)],
        out_specs=pl.BlockSpec((tm, d), lambda s: (s, 0)),
        out_shape=jax.ShapeDtypeStruct((ns * tm, d), F32),
        scratch_shapes=[pltpu.VMEM((GATHER_RING, 2 * tm, d), F32), pltpu.SemaphoreType.DMA((GATHER_RING,))],
        compiler_params=_cparams(("arbitrary",), 40),
        name="moe_combine",
    )(*([pos3] * GATHER_RING), x_new, mod3, wts, y_sorted)


def _moe_plan(route, n_rows):
    tm = MOE_TILE
    e = route[:, 0:2, :].astype(I32).transpose(0, 2, 1).reshape(n_rows, 2)
    wts = route[:, 2:4, :].transpose(0, 2, 1).reshape(n_rows, 2)
    e_flat = e.reshape(-1)
    onehot = (e_flat[:, None] == jnp.arange(N_EXPERTS, dtype=I32)[None, :]).astype(I32)
    csum = jnp.cumsum(onehot, axis=0)
    rank = jnp.sum(onehot * csum, axis=1) - 1
    counts = csum[-1]
    tiles_e = (counts + tm - 1) // tm
    tile_end = jnp.cumsum(tiles_e)
    tile_start = tile_end - tiles_e
    pos = tile_start[e_flat] * tm + rank
    nt = (2 * n_rows) // tm + N_EXPERTS
    slot_token = jnp.zeros((nt * tm,), I32).at[pos].set(jnp.arange(2 * n_rows, dtype=I32) // 2,
                                                        unique_indices=True)
    n_active = tile_end[-1]
    tile_id = jnp.minimum(jnp.arange(nt, dtype=I32), n_active - 1)
    tile_expert = jnp.minimum(jnp.sum((tile_id[:, None] >= tile_end[None, :]).astype(I32), axis=1),
                              N_EXPERTS - 1).astype(I32)
    tile_valid = (jnp.arange(nt, dtype=I32) < n_active).astype(I32)
    return tile_expert, tile_valid, slot_token, pos.reshape(n_rows, 2), wts


def kernel(x, c, ctx, c_ctx, ada_w, ada_b, norm1, norm2, w_in, gla_w_alpha, gla_b_alpha, gla_o_norm,
           s5_a_re, s5_a_im, s5_log_dt, s5_b_re, s5_b_im, s5_c_re, s5_c_im, s5_d, s5_w_glu, s5_b_glu,
           swa_q_norm, swa_k_norm, swa_sink, w_out, router_w, router_b, exp_w_gate, exp_w_up, exp_w_down):
    n_batch, seq_len, d = x.shape
    ctx_len = ctx.shape[1]
    depth = ada_w.shape[0]
    tm = ROW_TILE
    per_batch = ctx_len + seq_len
    assert ctx_len % tm == 0 and seq_len % tm == 0
    tpb, ctx_tiles = per_batch // tm, ctx_len // tm
    rows = n_batch * per_batch

    hk = GLA_HEADS * GLA_DK
    gla_w = GLA_HEADS * GLA_DV
    s5_w = s5_d.shape[1]
    swa_w = SWA_HEADS * SWA_HEAD_DIM
    kv_w = SWA_KV_HEADS * SWA_HEAD_DIM
    o_q, o_k, o_v = 0, hk, 2 * hk
    o_lr = o_v + gla_w
    o_r = o_lr + 2 * GLA_RANK
    o_u = o_r + gla_w
    o_sq = o_u + s5_w
    o_sk = o_sq + swa_w
    outs = [(2 * hk, BF16), (gla_w, BF16), (gla_w, BF16), (s5_w, BF16), (swa_w, BF16), (2 * kv_w, BF16),
            (LANES, F32)]

    xall = jnp.concatenate([ctx, x], axis=1).reshape(rows, d)
    cvec = jnp.concatenate([c, c_ctx[None], jnp.zeros((8 - n_batch - 1, d), c.dtype)], axis=0)
    cos, sin = _rope_tables(seq_len)
    rw_t = jnp.zeros((LANES, d), F32).at[:N_EXPERTS].set(router_w.astype(F32).T)
    rb_col = jnp.zeros((LANES, 1), F32).at[:N_EXPERTS, 0].set(router_b.astype(F32))

    mod3 = _ada_mod(cvec, ada_w, ada_b).reshape(depth * MOD_ROWS * 6, 1, d)
    w_prep = jnp.concatenate(
        [w_in[:, :, o_q:o_v], w_in[:, :, o_v:o_lr], w_in[:, :, o_r:o_u], w_in[:, :, o_u:o_sq],
         w_in[:, :, o_sq:o_sk], w_in[:, :, o_sk:], w_in[:, :, o_lr:o_r],
         jnp.zeros((depth, d, LANES - 2 * GLA_RANK), w_in.dtype)], axis=2).astype(BF16)
    w_glu16, w_out16 = s5_w_glu.astype(BF16), w_out.astype(BF16)
    wg16, wu16, wd16 = exp_w_gate.astype(BF16), exp_w_up.astype(BF16), exp_w_down.astype(BF16)
    s5_mats = jax.vmap(_s5_prep)(s5_a_re, s5_a_im, s5_log_dt, s5_b_re, s5_b_im, s5_c_re, s5_c_im, s5_d)

    out = None
    for layer in range(depth):
        last = layer == depth - 1
        qk, gv, gr, u, sq, skv, lr = _in_proj(xall, mod3, norm1[layer], w_prep, outs, tpb, ctx_tiles, n_batch,
                                              layer)

        wa = jnp.zeros((LANES, 2 * hk), F32)
        wa = wa.at[0:GLA_RANK, 0:hk].set(gla_w_alpha[layer, 0].astype(F32))
        wa = wa.at[GLA_RANK:2 * GLA_RANK, hk:].set(gla_w_alpha[layer, 1].astype(F32))
        ba = gla_b_alpha[layer].astype(F32).reshape(1, 2 * hk)
        o_f, o_b = _gla_scan(qk, gv, lr, wa, ba, n_batch, tpb, ctx_tiles)

        ys = _s5_scan(u, s5_mats, n_batch, ctx_len, layer)

        sw = _swa_attend(sq, skv, cos, sin, swa_q_norm[layer], swa_k_norm[layer], swa_sink[layer],
                         n_batch, seq_len, ctx_len)

        x_new, h2, route = _out_proj(o_f, o_b, gr, ys, sw, xall, mod3, norm2[layer], gla_o_norm[layer],
                                     w_glu16, s5_b_glu[layer].astype(F32), w_out16, rw_t, rb_col,
                                     tpb, ctx_tiles, n_batch, layer)

        tile_expert, tile_valid, slot_token, pos, wts = _moe_plan(route, rows)
        y_sorted = _moe_experts(tile_expert, tile_valid, slot_token, h2, wg16, wu16, wd16, layer)
        out = _moe_combine(pos, x_new, mod3, wts, y_sorted, n_batch, tpb, ctx_tiles, keep_ctx=not last,
                           layer=layer)
        xall = out
    return out.reshape(n_batch, seq_len, d)
```

```python
import functools
import math

import jax
import jax.numpy as jnp
from jax import lax
from jax.experimental import pallas as pl
from jax.experimental.pallas import tpu as pltpu

F32 = jnp.float32
BF16 = jnp.bfloat16
I32 = jnp.int32

EPS = 1e-6
NEG_INF = -1e30
LOG2E = math.log2(math.e)

LANES = 128
VMEM_BYTES = 64 * 1024 * 1024

ROW_TILE = 256
MOD_ROWS = 8

GLA_HEADS = 4
GLA_DK = 64
GLA_DV = 128
GLA_RANK = 16
GLA_GATE_NORM = 16.0
GLA_CHUNK = 64

S5_GROUP = 16
S5_STATE = 64
S5_CHUNK = 8
S5_BLOCK_GROUPS = LANES // S5_GROUP

SWA_HEAD_DIM = 128
SWA_HEADS = 8
SWA_KV_HEADS = 2
SWA_GROUP = SWA_HEADS // SWA_KV_HEADS
SWA_BLOCK = 128
ROPE_PAIRS = 32
ROPE_BASE = 10000.0
GRID_W = 64

N_EXPERTS = 16
N_EXPERT_GROUPS = 4
EXPERTS_PER_GROUP = 4
MOE_TILE = 256
DMA_UNROLL = 8
GATHER_RING = 3
CAST_BLOCKS = 64


def _cparams(semantics, vmem_mb):
    return pltpu.CompilerParams(dimension_semantics=semantics,
                                vmem_limit_bytes=vmem_mb * 1024 * 1024)


def _mm(a, b):
    return jnp.dot(a.astype(BF16), b.astype(BF16), preferred_element_type=F32)


def _mm_nt(a, b):
    return lax.dot_general(a.astype(BF16), b.astype(BF16), (((1,), (1,)), ((), ())),
                           preferred_element_type=F32)


def _mm_tn(a, b):
    return lax.dot_general(a.astype(BF16), b.astype(BF16), (((0,), (0,)), ((), ())),
                           preferred_element_type=F32)


def _split(a, parts):
    out, rest = [], a
    for p in range(parts):
        piece = rest.astype(BF16)
        out.append(piece)
        if p + 1 < parts:
            rest = rest - piece.astype(F32)
    return out


def _mm_split(a, b, dims=(((1,), (0,)), ((), ())), parts=3, a_exact=False, b_exact=False):
    pa = [a.astype(BF16)] if a_exact else _split(a, parts)
    pb = [b.astype(BF16)] if b_exact else _split(b, parts)
    out = None
    for i, ai in enumerate(pa):
        for j, bj in enumerate(pb):
            if i + j < parts:
                t = lax.dot_general(ai, bj, dims, preferred_element_type=F32)
                out = t if out is None else out + t
    return out


def _sigmoid(x):
    return 1.0 / (1.0 + jnp.exp(-x))


def _silu(x):
    return x * _sigmoid(x)


def _log_sigmoid(x):
    return jnp.minimum(x, 0.0) - jnp.log(1.0 + jnp.exp(-jnp.abs(x)))


def _gelu_tanh(x):
    return 0.5 * x * (1.0 + jnp.tanh(math.sqrt(2.0 / math.pi) * (x + 0.044715 * (x * x * x))))


def _rms(x, gain):
    return x * lax.rsqrt(jnp.mean(x * x, axis=-1, keepdims=True) + EPS) * gain


def _ada_body(c_ref, w_ref, b_ref, o_ref):
    s = _silu(c_ref[...])
    o_ref[...] = _mm(s, w_ref[...]) + b_ref[...]


def _ada_mod(cvec, w, b):
    rows, d = cvec.shape
    depth, _, n = w.shape
    tn = 1024
    return pl.pallas_call(
        _ada_body,
        grid=(depth, n // tn),
        in_specs=[pl.BlockSpec((rows, d), lambda l, i: (0, 0)),
                  pl.BlockSpec((None, d, tn), lambda l, i: (l, 0, i)),
                  pl.BlockSpec((None, 1, tn), lambda l, i: (l, 0, i))],
        out_specs=pl.BlockSpec((None, rows, tn), lambda l, i: (l, 0, i)),
        out_shape=jax.ShapeDtypeStruct((depth, rows, n), F32),
        compiler_params=_cparams(("arbitrary", "arbitrary"), 40),
        name="ada_mod",
    )(cvec, w, b.reshape(depth, 1, n))


def _mod_row(i, tiles_per_batch, ctx_tiles, n_batch, layer):
    return layer * MOD_ROWS + jnp.where(i % tiles_per_batch < ctx_tiles, n_batch, i // tiles_per_batch)


def _cast_specs(casts, layer, n_steps):
    in_specs, out_specs, out_shapes = [], [], []
    for a in casts:
        nblk, r, c = a.shape[1:]
        assert n_steps >= nblk
        in_specs.append(pl.BlockSpec((None, None, r, c),
                                     lambda i, nblk=nblk: (layer, jnp.minimum(i, nblk - 1), 0, 0)))
        out_specs.append(pl.BlockSpec((None, r, c), lambda i, nblk=nblk: (jnp.minimum(i, nblk - 1), 0, 0)))
        out_shapes.append(jax.ShapeDtypeStruct((nblk, r, c), BF16))
    return in_specs, out_specs, out_shapes


def _cast_blocks(src_refs, dst_refs):
    for src, dst in zip(src_refs, dst_refs):
        dst[...] = src[...].astype(dst.dtype)


def _proj_body(x_ref, sh_ref, sc_ref, g_ref, w_ref, *refs, splits, n_cast):
    cast_in, out_refs, cast_out = refs[:n_cast], refs[n_cast:len(refs) - n_cast], refs[len(refs) - n_cast:]
    y = _rms(x_ref[...], g_ref[...])
    h = (y * (1.0 + sc_ref[0]) + sh_ref[0]).astype(BF16)
    for o_ref, (c0, width) in zip(out_refs, splits):
        o_ref[...] = jnp.dot(h, w_ref[:, c0:c0 + width],
                             preferred_element_type=F32).astype(o_ref.dtype)
    _cast_blocks(cast_in, cast_out)


def _in_proj(xall, mod3, gain, w, outs, tiles_per_batch, ctx_tiles, n_batch, layer, casts):
    rows, d = xall.shape
    tm = ROW_TILE
    mrow = functools.partial(_mod_row, tiles_per_batch=tiles_per_batch, ctx_tiles=ctx_tiles,
                             n_batch=n_batch, layer=layer)
    splits, out_specs, out_shapes = [], [], []
    c0 = 0
    for width, dtype in outs:
        splits.append((c0, width))
        out_specs.append(pl.BlockSpec((tm, width), lambda i: (i, 0)))
        out_shapes.append(jax.ShapeDtypeStruct((rows, width), dtype))
        c0 += width
    cast_in, cast_out, cast_shapes = _cast_specs(casts, layer, rows // tm)
    res = pl.pallas_call(
        functools.partial(_proj_body, splits=tuple(splits), n_cast=len(casts)),
        grid=(rows // tm,),
        in_specs=[pl.BlockSpec((tm, d), lambda i: (i, 0)),
                  pl.BlockSpec((1, 1, d), lambda i: (mrow(i) * 6 + 0, 0, 0)),
                  pl.BlockSpec((1, 1, d), lambda i: (mrow(i) * 6 + 1, 0, 0)),
                  pl.BlockSpec((1, d), lambda i: (0, 0)),
                  pl.BlockSpec((None,) + w.shape[1:], lambda i: (layer, 0, 0),
                               pipeline_mode=pl.Buffered(1))] + cast_in,
        out_specs=out_specs + cast_out,
        out_shape=out_shapes + cast_shapes,
        compiler_params=_cparams(("arbitrary",), 56),
        name="in_proj",
    )(xall, mod3, mod3, gain.reshape(1, d), w, *casts)
    return res[:len(outs)], res[len(outs):]


def _gla_body(qkf, vf, lrf, qkb, vb, lrb, wa_ref, ba_ref, of_ref, ob_ref, sf_ref, sb_ref):
    t = pl.program_id(1)
    ck = GLA_CHUNK
    hk = GLA_HEADS * GLA_DK

    @pl.when(t == 0)
    def _init():
        sf_ref[...] = jnp.zeros_like(sf_ref)
        sb_ref[...] = jnp.zeros_like(sb_ref)

    tmr = qkf.shape[0]
    n_chunks = tmr // ck
    rowb = lax.broadcasted_iota(I32, (tmr, tmr), 0)
    colb = lax.broadcasted_iota(I32, (tmr, tmr), 1)
    same_chunk = rowb // ck == colb // ck
    row4 = lax.broadcasted_iota(I32, (GLA_HEADS * ck, ck), 0) % ck
    col4 = lax.broadcasted_iota(I32, (GLA_HEADS * ck, ck), 1)
    lane = lax.broadcasted_iota(I32, (1, hk), 1)
    head_masks = [(lane // GLA_DK == h).astype(F32) for h in range(GLA_HEADS)]
    sel = (lax.broadcasted_iota(I32, (tmr, n_chunks * GLA_DV), 0) // ck
           == lax.broadcasted_iota(I32, (tmr, n_chunks * GLA_DV), 1) // GLA_DV).astype(F32)

    def run(qk_ref, v_ref, lr_ref, w, bias, s_ref, o_ref, reverse):
        tri = (same_chunk & (rowb <= colb if reverse else rowb >= colb)).astype(F32)
        amask = row4 <= col4 if reverse else row4 >= col4
        la = _log_sigmoid(_mm_split(lr_ref[...], w, parts=2) + bias) * (1.0 / GLA_GATE_NORM)
        b = _mm_split(tri, la, a_exact=True)
        decay_all = jnp.exp(_mm_split(la, sel, dims=(((0,), (0,)), ((), ())), b_exact=True))
        q = qk_ref[:, 0:hk].astype(F32)
        k = qk_ref[:, hk:2 * hk].astype(F32)
        qd_all = q * (GLA_DK ** -0.5) * jnp.exp(b)
        kd_all = k * jnp.exp(-b)
        state = s_ref[...]
        order = range(n_chunks - 1, -1, -1) if reverse else range(n_chunks)
        for ci in order:
            rows = slice(ci * ck, (ci + 1) * ck)
            v = v_ref[rows, :]
            last = ci * ck if reverse else (ci + 1) * ck - 1
            ke = k[rows, :] * jnp.exp(b[last:last + 1, :] - b[rows, :])
            qd = qd_all[rows, :]
            qs = jnp.concatenate([qd * m for m in head_masks], axis=0).astype(BF16)
            att = jnp.where(amask, _mm_nt(qs, kd_all[rows, :]), 0.0)
            inter = _mm(qs, state)
            outs = []
            for h in range(GLA_HEADS):
                hr = slice(h * ck, (h + 1) * ck)
                outs.append(_mm(att[hr, :], v[:, h * GLA_DV:(h + 1) * GLA_DV]) + inter[hr, :])
            o_ref[rows, :] = jnp.concatenate(outs, axis=1)
            ds = _mm_tn(ke, v)
            ds = jnp.concatenate([ds[h * GLA_DK:(h + 1) * GLA_DK, h * GLA_DV:(h + 1) * GLA_DV]
                                  for h in range(GLA_HEADS)], axis=0)
            state = decay_all[:, ci * GLA_DV:(ci + 1) * GLA_DV] * state + ds
        s_ref[...] = state

    run(qkf, vf, lrf, wa_ref[:, 0:hk], ba_ref[:, 0:hk], sf_ref, of_ref, False)
    run(qkb, vb, lrb, wa_ref[:, hk:2 * hk], ba_ref[:, hk:2 * hk], sb_ref, ob_ref, True)


def _gla_scan(qk, v, lr, wa, ba, n_batch, tiles_per_batch, ctx_tiles):
    rows = qk.shape[0]
    tm = ROW_TILE
    tpb = tiles_per_batch

    def fwd(b, t):
        return (b * tpb + t, 0)

    def bwd(b, t):
        return (b * tpb + jnp.where(t < ctx_tiles, ctx_tiles - 1 - t, tpb - 1 - (t - ctx_tiles)), 0)

    wv = v.shape[1]
    return pl.pallas_call(
        _gla_body,
        grid=(n_batch, tpb),
        in_specs=[pl.BlockSpec((tm, qk.shape[1]), fwd), pl.BlockSpec((tm, wv), fwd),
                  pl.BlockSpec((tm, lr.shape[1]), fwd),
                  pl.BlockSpec((tm, qk.shape[1]), bwd), pl.BlockSpec((tm, wv), bwd),
                  pl.BlockSpec((tm, lr.shape[1]), bwd),
                  pl.BlockSpec(wa.shape, lambda b, t: (0, 0)),
                  pl.BlockSpec(ba.shape, lambda b, t: (0, 0))],
        out_specs=[pl.BlockSpec((tm, wv), fwd), pl.BlockSpec((tm, wv), bwd)],
        out_shape=[jax.ShapeDtypeStruct((rows, wv), F32)] * 2,
        scratch_shapes=[pltpu.VMEM((GLA_HEADS * GLA_DK, GLA_DV), F32)] * 2,
        compiler_params=_cparams(("arbitrary", "arbitrary"), 32),
        name="gla_scan",
    )(qk, v, lr, qk, v, lr, wa, ba)


def _s5_prep(a_re, a_im, log_dt, b_re, b_im, c_re, c_im, d):
    hp = lax.Precision.HIGHEST
    tt = S5_CHUNK
    n_groups = a_re.shape[1]
    bg = S5_BLOCK_GROUPS
    nb = n_groups // bg
    a_re, a_im, b_re, b_im = a_re.astype(F32), a_im.astype(F32), b_re.astype(F32), b_im.astype(F32)
    c_re, c_im = c_re.astype(F32), c_im.astype(F32)
    dt = jnp.exp(log_dt.astype(F32))[..., None]
    mag = jnp.exp(dt * a_re)
    ab_re, ab_im = mag * jnp.cos(dt * a_im), mag * jnp.sin(dt * a_im)
    den = a_re * a_re + a_im * a_im
    nr = ab_re - 1.0
    coef_re = (nr * a_re + ab_im * a_im) / den
    coef_im = (ab_im * a_re - nr * a_im) / den
    bb_re = coef_re[..., None] * b_re - coef_im[..., None] * b_im
    bb_im = coef_re[..., None] * b_im + coef_im[..., None] * b_re
    pr, pi = [jnp.ones_like(ab_re)], [jnp.zeros_like(ab_im)]
    for _ in range(tt):
        pr, pi = pr + [pr[-1] * ab_re - pi[-1] * ab_im], pi + [pr[-1] * ab_im + pi[-1] * ab_re]
    pr, pi = jnp.stack(pr), jnp.stack(pi)
    cpr = c_re[None] * pr[:, :, :, None, :] - c_im[None] * pi[:, :, :, None, :]
    cpi = c_re[None] * pi[:, :, :, None, :] + c_im[None] * pr[:, :, :, None, :]
    kern = (jnp.einsum('kdgcn,dgnm->kdgcm', cpr, bb_re, precision=hp)
            - jnp.einsum('kdgcn,dgnm->kdgcm', cpi, bb_im, precision=hp))
    kdim, half = tt * LANES, bg * S5_STATE
    ax_k, ax_s, ax_tc = jnp.arange(kdim), jnp.arange(half), jnp.arange(tt * S5_GROUP)
    k_group, k_step, k_chan = (ax_k // S5_GROUP) % bg, ax_k // LANES, ax_k % S5_GROUP
    rep_tc = ((ax_tc[:, None] // S5_GROUP == k_step[None, :])
              & (ax_tc[:, None] % S5_GROUP == k_chan[None, :])).astype(F32)
    rep_n = (jnp.arange(S5_STATE)[:, None] == ax_s[None, :] % S5_STATE).astype(F32)
    mask_kk = k_group[:, None] == k_group[None, :]
    mask_ks = k_group[:, None] == ax_s[None, :] // S5_STATE
    expand = lambda table, rep: jnp.einsum('jrk,kc->jrc', table, rep, precision=hp)

    s_idx = jnp.arange(tt)[:, None]
    t_idx = jnp.arange(tt)[None, :]
    lag_f = jnp.clip(t_idx - s_idx, 0, tt)
    lag_b = jnp.clip(s_idx - t_idx, 0, tt)
    toe = (jnp.where((t_idx >= s_idx)[:, :, None, None, None], kern[lag_f, 0], 0.0)
           + jnp.where((s_idx >= t_idx)[:, :, None, None, None], kern[lag_b, 1], 0.0))
    skip = (jnp.eye(tt, dtype=F32)[:, :, None, None, None]
            * d.astype(F32).reshape(n_groups, S5_GROUP)[None, None, :, :, None]
            * jnp.eye(S5_GROUP, dtype=F32)[None, None, None])
    toe = (toe + skip).reshape(tt, tt, nb, bg, S5_GROUP, S5_GROUP)
    toe = toe.transpose(2, 0, 3, 5, 1, 4).reshape(nb, kdim, tt * S5_GROUP)
    m_mat = jnp.where(mask_kk, expand(toe, rep_tc), 0.0)

    def inject(direction, power_of_s):
        p_re, p_im = pr[power_of_s, direction], pi[power_of_s, direction]
        v_re = p_re[..., None] * bb_re[direction][None] - p_im[..., None] * bb_im[direction][None]
        v_im = p_re[..., None] * bb_im[direction][None] + p_im[..., None] * bb_re[direction][None]
        def blk(v):
            v = v.reshape(tt, nb, bg, S5_STATE, S5_GROUP).transpose(1, 0, 2, 4, 3).reshape(nb, kdim, S5_STATE)
            return jnp.where(mask_ks, expand(v, rep_n), 0.0)
        return jnp.concatenate([blk(v_re), blk(v_im)], axis=-1)

    def readout(direction, power_of_t):
        e_re, e_im = cpr[power_of_t, direction], -cpi[power_of_t, direction]
        def blk(v):
            v = v.reshape(tt, nb, bg, S5_GROUP, S5_STATE).transpose(1, 2, 4, 0, 3).reshape(nb, half, tt * S5_GROUP)
            return jnp.where(mask_ks.T, expand(v, rep_tc), 0.0)
        return jnp.concatenate([blk(e_re), blk(e_im)], axis=1)

    steps = jnp.arange(tt)
    g_f = inject(0, tt - 1 - steps)
    g_b = inject(1, steps)
    e_f = readout(0, steps + 1)
    e_b = readout(1, tt - steps)
    a_t = jnp.stack([pr[tt, 0].reshape(nb, half), pi[tt, 0].reshape(nb, half),
                     pr[tt, 1].reshape(nb, half), pi[tt, 1].reshape(nb, half)], axis=1)
    cast = lambda z: z.astype(BF16)
    return cast(m_mat), cast(g_f), cast(g_b), cast(e_f), cast(e_b), a_t


def _s5_body(u_ref, m_ref, gf_ref, gb_ref, ef_ref, eb_ref, a_ref, y_ref, sf_ref, sb_ref,
             *, ctx_rows, n_rows, n_batch):
    half = S5_BLOCK_GROUPS * S5_STATE
    row_blocks = [slice(r * n_rows, (r + 1) * n_rows) for r in range(n_batch)]

    for rows in row_blocks:
        sf_ref[rows, :] = jnp.dot(u_ref[rows, :], gf_ref[0], preferred_element_type=F32)
        sb_ref[rows, :] = jnp.dot(u_ref[rows, :], gb_ref[0], preferred_element_type=F32)

    a = a_ref[0]
    afr, afi, abr, abi = a[0:1, :], a[1:2, :], a[2:3, :], a[3:4, :]

    sub = 8
    cpt = sub // n_batch
    n_tiles, ctx_tiles = n_rows // cpt, ctx_rows // cpt

    def advance(a_re, a_im, s_re, s_im, g):
        return a_re * s_re - a_im * s_im + g[:, 0:half], a_re * s_im + a_im * s_re + g[:, half:2 * half]

    def step(k, carry):
        fr, fi, br, bi = carry
        rf = pl.ds(pl.multiple_of(k * sub, sub), sub)
        g = sf_ref[rf, :]
        ins_r, ins_i = [], []
        for c in range(cpt):
            ins_r.append(fr)
            ins_i.append(fi)
            fr, fi = advance(afr, afi, fr, fi, g[c * n_batch:(c + 1) * n_batch, :])
        sf_ref[rf, 0:half] = jnp.concatenate(ins_r, axis=0)
        sf_ref[rf, half:2 * half] = jnp.concatenate(ins_i, axis=0)
        kb = jnp.where(k < ctx_tiles, ctx_tiles - 1 - k, n_tiles - 1 - (k - ctx_tiles))
        rb = pl.ds(pl.multiple_of(kb * sub, sub), sub)
        g = sb_ref[rb, :]
        ins_r, ins_i = [None] * cpt, [None] * cpt
        for c in range(cpt - 1, -1, -1):
            ins_r[c], ins_i[c] = br, bi
            br, bi = advance(abr, abi, br, bi, g[c * n_batch:(c + 1) * n_batch, :])
        sb_ref[rb, 0:half] = jnp.concatenate(ins_r, axis=0)
        sb_ref[rb, half:2 * half] = jnp.concatenate(ins_i, axis=0)
        return fr, fi, br, bi

    zero = jnp.zeros((n_batch, half), F32)
    lax.fori_loop(0, n_tiles, step, (zero, zero, zero, zero))

    for rows in row_blocks:
        y = (jnp.dot(u_ref[rows, :], m_ref[0], preferred_element_type=F32)
             + jnp.dot(sf_ref[rows, :].astype(BF16), ef_ref[0], preferred_element_type=F32)
             + jnp.dot(sb_ref[rows, :].astype(BF16), eb_ref[0], preferred_element_type=F32))
        y_ref[rows, :] = y.astype(y_ref.dtype)


def _s5_scan(u, mats, n_batch, ctx_len, layer):
    m_mat, g_f, g_b, e_f, e_b, a_t = mats
    rows, width = u.shape
    tt = S5_CHUNK
    nb = width // LANES
    srows = rows // tt // n_batch
    kdim = tt * LANES
    sdim = 2 * S5_BLOCK_GROUPS * S5_STATE
    u5 = u.reshape(n_batch, srows, tt, nb, LANES).transpose(3, 1, 0, 2, 4).reshape(nb, srows * n_batch, kdim)
    once = dict(pipeline_mode=pl.Buffered(1))
    wspec = lambda shape: pl.BlockSpec((None, 1) + shape, lambda j: (layer, j, 0, 0), **once)
    y5 = pl.pallas_call(
        functools.partial(_s5_body, ctx_rows=ctx_len // tt, n_rows=srows, n_batch=n_batch),
        grid=(nb,),
        in_specs=[pl.BlockSpec((None, srows * n_batch, kdim), lambda j: (j, 0, 0), **once),
                  wspec((kdim, kdim)), wspec((kdim, sdim)), wspec((kdim, sdim)),
                  wspec((sdim, kdim)), wspec((sdim, kdim)), wspec((4, sdim // 2))],
        out_specs=pl.BlockSpec((None, srows * n_batch, kdim), lambda j: (j, 0, 0)),
        out_shape=jax.ShapeDtypeStruct((nb, srows * n_batch, kdim), BF16),
        scratch_shapes=[pltpu.VMEM((srows * n_batch, sdim), F32), pltpu.VMEM((srows * n_batch, sdim), F32)],
        compiler_params=_cparams(("arbitrary",), 56),
        name="s5_scan",
    )(u5, m_mat, g_f, g_b, e_f, e_b, a_t)
    return y5.reshape(nb, srows, n_batch, tt, LANES).transpose(2, 1, 3, 0, 4).reshape(rows, width)


def _rope_tables(seq_len):
    pos = jnp.arange(seq_len)
    row = (pos // GRID_W).astype(F32)
    col = (pos % GRID_W).astype(F32)
    inv_freq = ROPE_BASE ** (-jnp.arange(ROPE_PAIRS, dtype=F32) / ROPE_PAIRS)
    ar, ac = row[:, None] * inv_freq, col[:, None] * inv_freq
    cos = jnp.concatenate([jnp.cos(ar), jnp.cos(ar), jnp.cos(ac), jnp.cos(ac)], axis=1)
    sin = jnp.concatenate([-jnp.sin(ar), jnp.sin(ar), -jnp.sin(ac), jnp.sin(ac)], axis=1)
    return cos, sin


def _swa_body(sink_ref, q_ref, kp_ref, kc_ref, kn_ref, kx_ref, cp_ref, sp_ref, cc_ref, sc_ref,
              cn_ref, sn_ref, qg_ref, kg_ref, o_ref, *, lat_blocks):
    n = pl.program_id(1)
    hd = SWA_HEAD_DIM
    blk = SWA_BLOCK
    kvw = SWA_KV_HEADS * hd
    scale = hd ** -0.5
    row = lax.broadcasted_iota(I32, (SWA_GROUP * blk, blk), 0) % blk
    col = lax.broadcasted_iota(I32, (SWA_GROUP * blk, blk), 1)
    qg, kg = qg_ref[...], kg_ref[...]
    ones_hd = jnp.ones((hd, hd), BF16)
    src = lax.broadcasted_iota(I32, (hd, hd), 0)
    dst = lax.broadcasted_iota(I32, (hd, hd), 1)
    partner = jnp.where(dst % (2 * ROPE_PAIRS) < ROPE_PAIRS, dst + ROPE_PAIRS, dst - ROPE_PAIRS)
    swap_halves = (src == partner).astype(BF16)

    def head_rms(x, gain):
        ss = _mm_split(x * x, ones_hd, parts=2, b_exact=True)
        return x * lax.rsqrt(ss * (1.0 / hd) + EPS) * gain

    def rope(x, cos, sin):
        return x * cos + _mm_split(x, swap_halves, parts=2, b_exact=True) * sin

    def lane_fold(op, blocks):
        parts = [b[:, c0:c0 + blk] for b in blocks for c0 in range(0, b.shape[1], blk)]
        out = parts[0]
        for part in parts[1:]:
            out = op(out, part)
        return out

    def attend(is_lat):
        for g in range(SWA_KV_HEADS):
            kcols = slice(g * hd, (g + 1) * hd)
            vcols = slice(kvw + g * hd, kvw + (g + 1) * hd)
            keys = [head_rms(kx_ref[:, kcols].astype(F32), kg).astype(BF16)]
            values = [kx_ref[:, vcols]]
            valid = [None]
            if is_lat:
                wins = ((kp_ref, cp_ref, sp_ref), (kc_ref, cc_ref, sc_ref), (kn_ref, cn_ref, sn_ref))
                keys += [rope(head_rms(r[:, kcols].astype(F32), kg), c[...], s[...]).astype(BF16)
                         for r, c, s in wins]
                values += [r[:, vcols] for r, _, _ in wins]
                valid += [(col >= row) & (col + (n - 1) * blk >= 0), None,
                          (col <= row) & (col + (n + 1) * blk < lat_blocks * blk)]
            qs, sinks = [], []
            for hh in range(SWA_GROUP):
                h = g * SWA_GROUP + hh
                q = head_rms(q_ref[:, h * hd:(h + 1) * hd].astype(F32), qg)
                if is_lat:
                    q = rope(q, cc_ref[...], sc_ref[...])
                qs.append((q * (scale * LOG2E)).astype(BF16))
                sinks.append(jnp.full((blk, 1), sink_ref[h] * LOG2E, F32))
            q4 = jnp.concatenate(qs, axis=0)
            sink = jnp.concatenate(sinks, axis=0)
            scores = []
            for kk, ok in zip(keys, valid):
                s = _mm_nt(q4, kk)
                scores.append(s if ok is None else jnp.where(ok, s, NEG_INF))
            m = jnp.maximum(jnp.max(lane_fold(jnp.maximum, scores), axis=-1, keepdims=True), sink)
            probs = [jnp.exp2(s - m).astype(BF16) for s in scores]
            den = jnp.exp2(sink - m)
            for p in probs:
                for c0 in range(0, p.shape[1], blk):
                    den = den + _mm(p[:, c0:c0 + blk], ones_hd)
            acc = None
            for p, vv in zip(probs, values):
                pv = _mm(p, vv)
                acc = pv if acc is None else acc + pv
            out = acc * (1.0 / den)
            for hh in range(SWA_GROUP):
                h = g * SWA_GROUP + hh
                o_ref[:, h * hd:(h + 1) * hd] = out[hh * blk:(hh + 1) * blk, :].astype(o_ref.dtype)

    @pl.when(n < lat_blocks)
    def _lat():
        attend(True)

    @pl.when(n >= lat_blocks)
    def _ctx():
        attend(False)


def _swa_attend(q, kv, cos, sin, q_gain, k_gain, sink, n_batch, seq_len, ctx_len):
    rows = q.shape[0]
    blk = SWA_BLOCK
    lat_blocks = seq_len // blk
    ctx_blocks = ctx_len // blk
    bpb = lat_blocks + ctx_blocks
    hd = SWA_HEAD_DIM

    def q_map(b, n, s):
        return (b * bpb + jnp.where(n < lat_blocks, ctx_blocks + n, n - lat_blocks), 0)

    def win(off):
        def kv_map(b, n, s):
            return (b * bpb + ctx_blocks + jnp.clip(n + off, 0, lat_blocks - 1), 0)

        def tab_map(b, n, s):
            return (jnp.clip(n + off, 0, lat_blocks - 1), 0)
        return kv_map, tab_map

    (kv_p, tab_p), (kv_c, tab_c), (kv_n, tab_n) = win(-1), win(0), win(1)
    kvs = lambda m: pl.BlockSpec((blk, kv.shape[1]), m)
    tab = lambda m: pl.BlockSpec((blk, hd), m)
    grid_spec = pltpu.PrefetchScalarGridSpec(
        num_scalar_prefetch=1,
        grid=(n_batch, bpb),
        in_specs=[pl.BlockSpec((blk, q.shape[1]), q_map),
                  kvs(kv_p), kvs(kv_c), kvs(kv_n),
                  pl.BlockSpec((ctx_len, kv.shape[1]), lambda b, n, s: (b * (bpb * blk // ctx_len), 0)),
                  tab(tab_p), tab(tab_p), tab(tab_c), tab(tab_c), tab(tab_n), tab(tab_n),
                  pl.BlockSpec((1, hd), lambda b, n, s: (0, 0)),
                  pl.BlockSpec((1, hd), lambda b, n, s: (0, 0))],
        out_specs=pl.BlockSpec((blk, q.shape[1]), q_map),
    )
    return pl.pallas_call(
        functools.partial(_swa_body, lat_blocks=lat_blocks),
        grid_spec=grid_spec,
        out_shape=jax.ShapeDtypeStruct((rows, q.shape[1]), BF16),
        compiler_params=_cparams(("arbitrary", "arbitrary"), 32),
        name="swa_attend",
    )(sink.astype(F32), q, kv, kv, kv, kv, cos, sin, cos, sin, cos, sin,
      q_gain.reshape(1, hd).astype(F32), k_gain.reshape(1, hd).astype(F32))


def _route_rows(logits_t, bias_col):
    aff = _sigmoid(logits_t)
    biased = aff + bias_col
    v = [biased[e:e + 1, :] for e in range(N_EXPERTS)]
    a = [aff[e:e + 1, :] for e in range(N_EXPERTS)]
    gsz = EXPERTS_PER_GROUP
    best_g = best_s = None
    for g in range(N_EXPERT_GROUPS):
        vg = v[g * gsz:(g + 1) * gsz]
        score = None
        for i in range(gsz):
            for k in range(i + 1, gsz):
                pair = vg[i] + vg[k]
                score = pair if score is None else jnp.maximum(score, pair)
        if best_g is None:
            best_g, best_s = jnp.zeros_like(score, dtype=I32), score
        else:
            upd = score > best_s
            best_g = jnp.where(upd, g, best_g)
            best_s = jnp.where(upd, score, best_s)

    def pick(rows):
        out = []
        for i in range(gsz):
            x = rows[i]
            for g in range(1, N_EXPERT_GROUPS):
                x = jnp.where(best_g == g, rows[g * gsz + i], x)
            out.append(x)
        return out

    vb, ab = pick(v), pick(a)
    i1, v1, a1 = jnp.zeros_like(best_g), vb[0], ab[0]
    for i in range(1, gsz):
        upd = vb[i] > v1
        i1, v1, a1 = jnp.where(upd, i, i1), jnp.where(upd, vb[i], v1), jnp.where(upd, ab[i], a1)
    i2 = v2 = a2 = None
    for i in range(gsz):
        cand = jnp.where(i1 == i, -jnp.inf, vb[i])
        if i2 is None:
            i2, v2, a2 = jnp.zeros_like(best_g), cand, ab[0]
        else:
            upd = cand > v2
            i2, v2, a2 = jnp.where(upd, i, i2), jnp.where(upd, cand, v2), jnp.where(upd, ab[i], a2)
    tot = a1 + a2
    e1 = (best_g * gsz + i1).astype(F32)
    e2 = (best_g * gsz + i2).astype(F32)
    zeros = jnp.zeros((4, e1.shape[1]), F32)
    return jnp.concatenate([e1, e2, a1 / tot, a2 / tot, zeros], axis=0)


def _out_body(of_ref, ob_ref, r_ref, ys_ref, sw_ref, x_ref, g1_ref, sh_ref, sc_ref, n2_ref,
              og_ref, wglu_ref, bglu_ref, wo_ref, rw_ref, rb_ref, cast_in, xo_ref, h_ref, rt_ref, cast_out):
    _cast_blocks([cast_in], [cast_out])
    o = of_ref[...] + ob_ref[...]
    r = r_ref[...].astype(F32)
    gain = og_ref[...]
    heads = []
    for h in range(GLA_HEADS):
        cols = slice(h * GLA_DV, (h + 1) * GLA_DV)
        heads.append(_rms(o[:, cols], gain[:, cols]))
    gla_y = jnp.concatenate(heads, axis=1) * _silu(r)
    z = _gelu_tanh(ys_ref[...].astype(F32))
    s5_y = z * _sigmoid(_mm(z, wglu_ref[...]) + bglu_ref[...])
    w1 = gla_y.shape[1]
    w2 = w1 + s5_y.shape[1]
    y = (_mm(gla_y, wo_ref[0:w1, :]) + _mm(s5_y, wo_ref[w1:w2, :])
         + jnp.dot(sw_ref[...], wo_ref[w2:, :], preferred_element_type=F32))
    x_new = x_ref[...] + g1_ref[0] * y
    xo_ref[...] = x_new
    h2 = _rms(x_new, n2_ref[...]) * (1.0 + sc_ref[0]) + sh_ref[0]
    h_ref[...] = h2
    logits_t = _mm_split(rw_ref[...], h2, dims=(((1,), (1,)), ((), ())), parts=2)
    rt_ref[0] = _route_rows(logits_t, rb_ref[...])


def _out_proj(o_f, o_b, r, ys, sw, xall, mod3, norm2, o_gain, w_glu, b_glu, w_out, rw_t, rb_col,
              tiles_per_batch, ctx_tiles, n_batch, layer, cast):
    rows, d = xall.shape
    tm = ROW_TILE
    nt = rows // tm
    mrow = functools.partial(_mod_row, tiles_per_batch=tiles_per_batch, ctx_tiles=ctx_tiles,
                             n_batch=n_batch, layer=layer)
    rowspec = lambda a: pl.BlockSpec((tm, a.shape[1]), lambda i: (i, 0))
    const = lambda a: pl.BlockSpec(a.shape, lambda i: (0,) * a.ndim)
    layered = lambda a, **kw: pl.BlockSpec((None,) + a.shape[1:], lambda i: (layer,) + (0,) * (a.ndim - 1), **kw)
    modspec = lambda k: pl.BlockSpec((1, 1, d), lambda i: (mrow(i) * 6 + k, 0, 0))
    consts = [norm2.reshape(1, d), o_gain.reshape(1, -1)]
    cast_in, cast_out, cast_shapes = _cast_specs([cast], layer, nt)
    return pl.pallas_call(
        _out_body,
        grid=(nt,),
        in_specs=[rowspec(o_f), rowspec(o_b), rowspec(r), rowspec(ys), rowspec(sw), rowspec(xall),
                  modspec(2), modspec(3), modspec(4)]
                 + [const(a) for a in consts]
                 + [layered(w_glu), const(b_glu.reshape(1, -1)),
                    layered(w_out, pipeline_mode=pl.Buffered(1)), const(rw_t), const(rb_col)] + cast_in,
        out_specs=[pl.BlockSpec((tm, d), lambda i: (i, 0)), pl.BlockSpec((tm, d), lambda i: (i, 0)),
                   pl.BlockSpec((1, 8, tm), lambda i: (i, 0, 0))] + cast_out,
        out_shape=[jax.ShapeDtypeStruct((rows, d), F32), jax.ShapeDtypeStruct((rows, d), F32),
                   jax.ShapeDtypeStruct((nt, 8, tm), F32)] + cast_shapes,
        compiler_params=_cparams(("arbitrary",), 48),
        name="out_proj",
    )(o_f, o_b, r, ys, sw, xall, mod3, mod3, mod3, *consts, w_glu, b_glu.reshape(1, -1), w_out, rw_t, rb_col,
      cast)


def _row_copy(src_hbm, row, dst, slot, sem):
    return pltpu.make_async_copy(src_hbm.at[pl.ds(row, 1)], dst.at[pl.ds(slot, 1)], sem)


def _gather_start(src_hbm, idx_ref, dst, sem, n_rows, unrolled=False):
    if unrolled:
        for r in range(n_rows):
            _row_copy(src_hbm, idx_ref[0, 0, r], dst, r, sem).start()
        return

    def body(g, carry):
        for u in range(DMA_UNROLL):
            r = g * DMA_UNROLL + u
            _row_copy(src_hbm, idx_ref[0, 0, r], dst, r, sem).start()
        return carry
    lax.fori_loop(0, n_rows // DMA_UNROLL, body, 0)


def _gather_wait(src_hbm, dst, sem, n_rows):
    for r in range(n_rows):
        _row_copy(src_hbm, 0, dst, r, sem).wait()


def _ring_gather(step, n_steps, src_hbm, idx_refs, ring, sem, work):
    depth = GATHER_RING
    n_rows = ring.shape[1]

    @pl.when(step == 0)
    def _prime():
        for k in range(depth - 1):
            _gather_start(src_hbm, idx_refs[k], ring.at[k], sem.at[k], n_rows)

    slot = step % depth
    ahead = (step + depth - 1) % depth
    _gather_wait(src_hbm, ring.at[slot], sem.at[slot], n_rows)
    work(ring.at[slot],
         lambda unrolled: _gather_start(src_hbm, idx_refs[depth - 1], ring.at[ahead], sem.at[ahead], n_rows,
                                        unrolled))

    @pl.when(step == n_steps - 1)
    def _drain():
        for k in range(1, depth):
            late = (step + k) % depth
            _gather_wait(src_hbm, ring.at[late], sem.at[late], n_rows)


def _moe_body(te_ref, tv_ref, idx0_ref, idx1_ref, idx2_ref, x_hbm, wg_ref, wu_ref, wd_ref, o_ref, ring, sem,
              *, n_steps):
    i = pl.program_id(0)

    def work(cur, start_next):
        @pl.when(tv_ref[i] > 0)
        def _active():
            start_next(True)
            x = cur[...].astype(BF16)
            gate = jnp.dot(x, wg_ref[...], preferred_element_type=F32)
            up = jnp.dot(x, wu_ref[...], preferred_element_type=F32)
            act = (_silu(gate) * up).astype(BF16)
            o_ref[...] = jnp.dot(act, wd_ref[...], preferred_element_type=F32)

        @pl.when(tv_ref[i] == 0)
        def _idle():
            start_next(False)
            o_ref[...] = jnp.zeros_like(o_ref)

    _ring_gather(i, n_steps, x_hbm, (idx0_ref, idx1_ref, idx2_ref), ring, sem, work)


def _moe_experts(tile_expert, tile_valid, slot_token, tokens, w_gate, w_up, w_down):
    tm = MOE_TILE
    nt = tile_expert.shape[0]
    d = tokens.shape[1]
    de = w_gate.shape[2]
    wspec = lambda k, n: pl.BlockSpec((None, k, n), lambda i, te, tv: (te[i], 0, 0))
    idx = lambda off: pl.BlockSpec((1, 1, tm), lambda i, te, tv: (jnp.minimum(i + off, nt - 1), 0, 0),
                                   memory_space=pltpu.SMEM)
    grid_spec = pltpu.PrefetchScalarGridSpec(
        num_scalar_prefetch=2,
        grid=(nt,),
        in_specs=[idx(k) for k in range(GATHER_RING)]
                 + [pl.BlockSpec(memory_space=pl.ANY), wspec(d, de), wspec(d, de), wspec(de, d)],
        out_specs=pl.BlockSpec((tm, d), lambda i, te, tv: (i, 0)),
        scratch_shapes=[pltpu.VMEM((GATHER_RING, tm, d), F32), pltpu.SemaphoreType.DMA((GATHER_RING,))],
    )
    slots = slot_token.reshape(nt, 1, tm)
    return pl.pallas_call(
        functools.partial(_moe_body, n_steps=nt),
        grid_spec=grid_spec,
        out_shape=jax.ShapeDtypeStruct((nt * tm, d), F32),
        compiler_params=_cparams(("arbitrary",), 56),
        name="moe_experts",
    )(tile_expert, tile_valid, *([slots] * GATHER_RING), tokens, w_gate, w_up, w_down)


def _combine_body(pos0_ref, pos1_ref, pos2_ref, x_ref, g2_ref, w_ref, y_hbm, o_ref, ring, sem, *, n_steps):
    tm = x_ref.shape[0]

    def work(cur, start_next):
        start_next(True)
        w = w_ref[...]
        f = w[:, 0:1] * cur[0:tm, :] + w[:, 1:2] * cur[tm:2 * tm, :]
        o_ref[...] = x_ref[...] + g2_ref[0] * f

    _ring_gather(pl.program_id(0), n_steps, y_hbm, (pos0_ref, pos1_ref, pos2_ref), ring, sem, work)


def _moe_combine(pos, x_new, mod3, wts, y_sorted, n_batch, tiles_per_batch, ctx_tiles, keep_ctx, layer):
    rows, d = x_new.shape
    tm = ROW_TILE
    tpb = tiles_per_batch
    first = 0 if keep_ctx else ctx_tiles
    out_tpb = tpb - first
    ns = n_batch * out_tpb
    in_tile = lambda s: (s // out_tpb) * tpb + first + s % out_tpb
    mrow = lambda s: layer * MOD_ROWS + jnp.where(first + s % out_tpb < ctx_tiles, n_batch, s // out_tpb)
    pos3 = pos.reshape(rows // tm, tm, 2).transpose(0, 2, 1).reshape(rows // tm, 1, 2 * tm)
    idx = lambda off: pl.BlockSpec((1, 1, 2 * tm), lambda s: (in_tile(jnp.minimum(s + off, ns - 1)), 0, 0),
                                   memory_space=pltpu.SMEM)
    return pl.pallas_call(
        functools.partial(_combine_body, n_steps=ns),
        grid=(ns,),
        in_specs=[idx(k) for k in range(GATHER_RING)] + [
                  pl.BlockSpec((tm, d), lambda s: (in_tile(s), 0)),
                  pl.BlockSpec((1, 1, d), lambda s: (mrow(s) * 6 + 5, 0, 0)),
                  pl.BlockSpec((tm, 2), lambda s: (in_tile(s), 0)),
                  pl.BlockSpec(memory_space=pl.ANY)],
        out_specs=pl.BlockSpec((tm, d), lambda s: (s, 0)),
        out_shape=jax.ShapeDtypeStruct((ns * tm, d), F32),
        scratch_shapes=[pltpu.VMEM((GATHER_RING, 2 * tm, d), F32), pltpu.SemaphoreType.DMA((GATHER_RING,))],
        compiler_params=_cparams(("arbitrary",), 40),
        name="moe_combine",
    )(*([pos3] * GATHER_RING), x_new, mod3, wts, y_sorted)


def _moe_plan(route, n_rows):
    tm = MOE_TILE
    e = route[:, 0:2, :].astype(I32).transpose(0, 2, 1).reshape(n_rows, 2)
    wts = route[:, 2:4, :].transpose(0, 2, 1).reshape(n_rows, 2)
    e_flat = e.reshape(-1)
    onehot = (e_flat[:, None] == jnp.arange(N_EXPERTS, dtype=I32)[None, :]).astype(I32)
    csum = jnp.cumsum(onehot, axis=0)
    rank = jnp.sum(onehot * csum, axis=1) - 1
    counts = csum[-1]
    tiles_e = (counts + tm - 1) // tm
    tile_end = jnp.cumsum(tiles_e)
    tile_start = tile_end - tiles_e
    pos = tile_start[e_flat] * tm + rank
    nt = (2 * n_rows) // tm + N_EXPERTS
    slot_token = jnp.zeros((nt * tm,), I32).at[pos].set(jnp.arange(2 * n_rows, dtype=I32) // 2,
                                                        unique_indices=True)
    n_active = tile_end[-1]
    tile_id = jnp.minimum(jnp.arange(nt, dtype=I32), n_active - 1)
    tile_expert = jnp.minimum(jnp.sum((tile_id[:, None] >= tile_end[None, :]).astype(I32), axis=1),
                              N_EXPERTS - 1).astype(I32)
    tile_valid = (jnp.arange(nt, dtype=I32) < n_active).astype(I32)
    return tile_expert, tile_valid, slot_token, pos.reshape(n_rows, 2), wts


def kernel(x, c, ctx, c_ctx, ada_w, ada_b, norm1, norm2, w_in, gla_w_alpha, gla_b_alpha, gla_o_norm,
           s5_a_re, s5_a_im, s5_log_dt, s5_b_re, s5_b_im, s5_c_re, s5_c_im, s5_d, s5_w_glu, s5_b_glu,
           swa_q_norm, swa_k_norm, swa_sink, w_out, router_w, router_b, exp_w_gate, exp_w_up, exp_w_down):
    n_batch, seq_len, d = x.shape
    ctx_len = ctx.shape[1]
    depth = ada_w.shape[0]
    tm = ROW_TILE
    per_batch = ctx_len + seq_len
    assert ctx_len % tm == 0 and seq_len % tm == 0
    tpb, ctx_tiles = per_batch // tm, ctx_len // tm
    rows = n_batch * per_batch

    hk = GLA_HEADS * GLA_DK
    gla_w = GLA_HEADS * GLA_DV
    s5_w = s5_d.shape[1]
    swa_w = SWA_HEADS * SWA_HEAD_DIM
    kv_w = SWA_KV_HEADS * SWA_HEAD_DIM
    o_q, o_k, o_v = 0, hk, 2 * hk
    o_lr = o_v + gla_w
    o_r = o_lr + 2 * GLA_RANK
    o_u = o_r + gla_w
    o_sq = o_u + s5_w
    o_sk = o_sq + swa_w
    outs = [(2 * hk, BF16), (gla_w, BF16), (gla_w, BF16), (s5_w, BF16), (swa_w, BF16), (2 * kv_w, BF16),
            (LANES, F32)]

    xall = jnp.concatenate([ctx, x], axis=1).reshape(rows, d)
    cvec = jnp.concatenate([c, c_ctx[None], jnp.zeros((8 - n_batch - 1, d), c.dtype)], axis=0)
    cos, sin = _rope_tables(seq_len)
    rw_t = jnp.zeros((LANES, d), F32).at[:N_EXPERTS].set(router_w.astype(F32).T)
    rb_col = jnp.zeros((LANES, 1), F32).at[:N_EXPERTS, 0].set(router_b.astype(F32))

    mod3 = _ada_mod(cvec, ada_w, ada_b).reshape(depth * MOD_ROWS * 6, 1, d)
    w_prep = jnp.concatenate(
        [w_in[:, :, o_q:o_v], w_in[:, :, o_v:o_lr], w_in[:, :, o_r:o_u], w_in[:, :, o_u:o_sq],
         w_in[:, :, o_sq:o_sk], w_in[:, :, o_sk:], w_in[:, :, o_lr:o_r],
         jnp.zeros((depth, d, LANES - 2 * GLA_RANK), w_in.dtype)], axis=2).astype(BF16)
    w_glu16, w_out16 = s5_w_glu.astype(BF16), w_out.astype(BF16)
    n_exp, _, d_exp = exp_w_gate.shape[1:]
    wg_blk = exp_w_gate.reshape(depth, CAST_BLOCKS, n_exp * d // CAST_BLOCKS, d_exp)
    wu_blk = exp_w_up.reshape(depth, CAST_BLOCKS, n_exp * d // CAST_BLOCKS, d_exp)
    wd_blk = exp_w_down.reshape(depth, CAST_BLOCKS, n_exp * d_exp // CAST_BLOCKS, d)
    s5_mats = jax.vmap(_s5_prep)(s5_a_re, s5_a_im, s5_log_dt, s5_b_re, s5_b_im, s5_c_re, s5_c_im, s5_d)

    out = None
    for layer in range(depth):
        last = layer == depth - 1
        (qk, gv, gr, u, sq, skv, lr), (wg16, wu16) = _in_proj(
            xall, mod3, norm1[layer], w_prep, outs, tpb, ctx_tiles, n_batch, layer, [wg_blk, wu_blk])

        wa = jnp.zeros((LANES, 2 * hk), F32)
        wa = wa.at[0:GLA_RANK, 0:hk].set(gla_w_alpha[layer, 0].astype(F32))
        wa = wa.at[GLA_RANK:2 * GLA_RANK, hk:].set(gla_w_alpha[layer, 1].astype(F32))
        ba = gla_b_alpha[layer].astype(F32).reshape(1, 2 * hk)
        o_f, o_b = _gla_scan(qk, gv, lr, wa, ba, n_batch, tpb, ctx_tiles)

        ys = _s5_scan(u, s5_mats, n_batch, ctx_len, layer)

        sw = _swa_attend(sq, skv, cos, sin, swa_q_norm[layer], swa_k_norm[layer], swa_sink[layer],
                         n_batch, seq_len, ctx_len)

        x_new, h2, route, wd16 = _out_proj(o_f, o_b, gr, ys, sw, xall, mod3, norm2[layer], gla_o_norm[layer],
                                           w_glu16, s5_b_glu[layer].astype(F32), w_out16, rw_t, rb_col,
                                           tpb, ctx_tiles, n_batch, layer, wd_blk)

        tile_expert, tile_valid, slot_token, pos, wts = _moe_plan(route, rows)
        y_sorted = _moe_experts(tile_expert, tile_valid, slot_token, h2,
                                wg16.reshape(n_exp, d, d_exp), wu16.reshape(n_exp, d, d_exp),
                                wd16.reshape(n_exp, d_exp, d))
        out = _moe_combine(pos, x_new, mod3, wts, y_sorted, n_batch, tpb, ctx_tiles, keep_ctx=not last,
                           layer=layer)
        xall = out
    return out.reshape(n_batch, seq_len, d)
```

```python
import functools
import math

import jax
import jax.numpy as jnp
from jax import lax
from jax.experimental import pallas as pl
from jax.experimental.pallas import tpu as pltpu

F32 = jnp.float32
BF16 = jnp.bfloat16
I32 = jnp.int32

EPS = 1e-6
NEG_INF = -1e30
LOG2E = math.log2(math.e)

LANES = 128
VMEM_BYTES = 64 * 1024 * 1024

ROW_TILE = 256
MOD_ROWS = 8

GLA_HEADS = 4
GLA_DK = 64
GLA_DV = 128
GLA_RANK = 16
GLA_GATE_NORM = 16.0
GLA_CHUNK = 64

S5_GROUP = 16
S5_STATE = 64
S5_CHUNK = 8
S5_BLOCK_GROUPS = LANES // S5_GROUP

SWA_HEAD_DIM = 128
SWA_HEADS = 8
SWA_KV_HEADS = 2
SWA_GROUP = SWA_HEADS // SWA_KV_HEADS
SWA_BLOCK = 128
ROPE_PAIRS = 32
ROPE_BASE = 10000.0
GRID_W = 64

N_EXPERTS = 16
N_EXPERT_GROUPS = 4
EXPERTS_PER_GROUP = 4
MOE_TILE = 256
DMA_UNROLL = 8
GATHER_RING = 3
CAST_BLOCKS = 64


def _cparams(semantics, vmem_mb):
    return pltpu.CompilerParams(dimension_semantics=semantics,
                                vmem_limit_bytes=vmem_mb * 1024 * 1024)


def _mm(a, b):
    return jnp.dot(a.astype(BF16), b.astype(BF16), preferred_element_type=F32)


def _mm_nt(a, b):
    return lax.dot_general(a.astype(BF16), b.astype(BF16), (((1,), (1,)), ((), ())),
                           preferred_element_type=F32)


def _mm_tn(a, b):
    return lax.dot_general(a.astype(BF16), b.astype(BF16), (((0,), (0,)), ((), ())),
                           preferred_element_type=F32)


def _split(a, parts):
    out, rest = [], a
    for p in range(parts):
        piece = rest.astype(BF16)
        out.append(piece)
        if p + 1 < parts:
            rest = rest - piece.astype(F32)
    return out


def _mm_split(a, b, dims=(((1,), (0,)), ((), ())), parts=3, a_exact=False, b_exact=False):
    pa = [a.astype(BF16)] if a_exact else _split(a, parts)
    pb = [b.astype(BF16)] if b_exact else _split(b, parts)
    out = None
    for i, ai in enumerate(pa):
        for j, bj in enumerate(pb):
            if i + j < parts:
                t = lax.dot_general(ai, bj, dims, preferred_element_type=F32)
                out = t if out is None else out + t
    return out


def _sigmoid(x):
    return 1.0 / (1.0 + jnp.exp(-x))


def _silu(x):
    return x * _sigmoid(x)


def _log_sigmoid(x):
    return jnp.minimum(x, 0.0) - jnp.log(1.0 + jnp.exp(-jnp.abs(x)))


def _gelu_tanh(x):
    return 0.5 * x * (1.0 + jnp.tanh(math.sqrt(2.0 / math.pi) * (x + 0.044715 * (x * x * x))))


def _rms(x, gain):
    return x * lax.rsqrt(jnp.mean(x * x, axis=-1, keepdims=True) + EPS) * gain


def _ada_body(c_ref, w_ref, b_ref, o_ref):
    s = _silu(c_ref[...])
    o_ref[...] = _mm(s, w_ref[...]) + b_ref[...]


def _ada_mod(cvec, w, b):
    rows, d = cvec.shape
    depth, _, n = w.shape
    tn = 1024
    return pl.pallas_call(
        _ada_body,
        grid=(depth, n // tn),
        in_specs=[pl.BlockSpec((rows, d), lambda l, i: (0, 0)),
                  pl.BlockSpec((None, d, tn), lambda l, i: (l, 0, i)),
                  pl.BlockSpec((None, 1, tn), lambda l, i: (l, 0, i))],
        out_specs=pl.BlockSpec((None, rows, tn), lambda l, i: (l, 0, i)),
        out_shape=jax.ShapeDtypeStruct((depth, rows, n), F32),
        compiler_params=_cparams(("arbitrary", "arbitrary"), 40),
        name="ada_mod",
    )(cvec, w, b.reshape(depth, 1, n))


def _mod_row(i, tiles_per_batch, ctx_tiles, n_batch, layer):
    return layer * MOD_ROWS + jnp.where(i % tiles_per_batch < ctx_tiles, n_batch, i // tiles_per_batch)


def _cast_specs(casts, layer, n_steps):
    in_specs, out_specs, out_shapes = [], [], []
    for a in casts:
        nblk, r, c = a.shape[1:]
        assert n_steps >= nblk
        in_specs.append(pl.BlockSpec((None, None, r, c),
                                     lambda i, nblk=nblk: (layer, jnp.minimum(i, nblk - 1), 0, 0)))
        out_specs.append(pl.BlockSpec((None, r, c), lambda i, nblk=nblk: (jnp.minimum(i, nblk - 1), 0, 0)))
        out_shapes.append(jax.ShapeDtypeStruct((nblk, r, c), BF16))
    return in_specs, out_specs, out_shapes


def _cast_blocks(src_refs, dst_refs):
    for src, dst in zip(src_refs, dst_refs):
        dst[...] = src[...].astype(dst.dtype)


def _proj_body(x_ref, sh_ref, sc_ref, g_ref, w_ref, *refs, splits, n_cast):
    cast_in, out_refs, cast_out = refs[:n_cast], refs[n_cast:len(refs) - n_cast], refs[len(refs) - n_cast:]
    y = _rms(x_ref[...], g_ref[...])
    h = (y * (1.0 + sc_ref[0]) + sh_ref[0]).astype(BF16)
    for o_ref, (c0, width) in zip(out_refs, splits):
        o_ref[...] = jnp.dot(h, w_ref[:, c0:c0 + width],
                             preferred_element_type=F32).astype(o_ref.dtype)
    _cast_blocks(cast_in, cast_out)


def _in_proj(xall, mod3, gain, w, outs, tiles_per_batch, ctx_tiles, n_batch, layer, casts):
    rows, d = xall.shape
    tm = ROW_TILE
    mrow = functools.partial(_mod_row, tiles_per_batch=tiles_per_batch, ctx_tiles=ctx_tiles,
                             n_batch=n_batch, layer=layer)
    splits, out_specs, out_shapes = [], [], []
    c0 = 0
    for width, dtype in outs:
        splits.append((c0, width))
        out_specs.append(pl.BlockSpec((tm, width), lambda i: (i, 0)))
        out_shapes.append(jax.ShapeDtypeStruct((rows, width), dtype))
        c0 += width
    cast_in, cast_out, cast_shapes = _cast_specs(casts, layer, rows // tm)
    res = pl.pallas_call(
        functools.partial(_proj_body, splits=tuple(splits), n_cast=len(casts)),
        grid=(rows // tm,),
        in_specs=[pl.BlockSpec((tm, d), lambda i: (i, 0)),
                  pl.BlockSpec((1, 1, d), lambda i: (mrow(i) * 6 + 0, 0, 0)),
                  pl.BlockSpec((1, 1, d), lambda i: (mrow(i) * 6 + 1, 0, 0)),
                  pl.BlockSpec((1, d), lambda i: (0, 0)),
                  pl.BlockSpec((None,) + w.shape[1:], lambda i: (layer, 0, 0),
                               pipeline_mode=pl.Buffered(1))] + cast_in,
        out_specs=out_specs + cast_out,
        out_shape=out_shapes + cast_shapes,
        compiler_params=_cparams(("arbitrary",), 56),
        name="in_proj",
    )(xall, mod3, mod3, gain.reshape(1, d), w, *casts)
    return res[:len(outs)], res[len(outs):]


def _gla_body(qkf, vf, lrf, qkb, vb, lrb, wa_ref, ba_ref, of_ref, ob_ref, sf_ref, sb_ref):
    t = pl.program_id(1)
    ck = GLA_CHUNK
    hk = GLA_HEADS * GLA_DK

    @pl.when(t == 0)
    def _init():
        sf_ref[...] = jnp.zeros_like(sf_ref)
        sb_ref[...] = jnp.zeros_like(sb_ref)

    tmr = qkf.shape[0]
    n_chunks = tmr // ck
    rowb = lax.broadcasted_iota(I32, (tmr, tmr), 0)
    colb = lax.broadcasted_iota(I32, (tmr, tmr), 1)
    same_chunk = rowb // ck == colb // ck
    row4 = lax.broadcasted_iota(I32, (GLA_HEADS * ck, ck), 0) % ck
    col4 = lax.broadcasted_iota(I32, (GLA_HEADS * ck, ck), 1)
    lane = lax.broadcasted_iota(I32, (1, hk), 1)
    head_masks = [(lane // GLA_DK == h).astype(F32) for h in range(GLA_HEADS)]
    sel = (lax.broadcasted_iota(I32, (tmr, n_chunks * GLA_DV), 0) // ck
           == lax.broadcasted_iota(I32, (tmr, n_chunks * GLA_DV), 1) // GLA_DV).astype(F32)

    def run(qk_ref, v_ref, lr_ref, w, bias, s_ref, o_ref, reverse):
        tri = (same_chunk & (rowb <= colb if reverse else rowb >= colb)).astype(F32)
        amask = row4 <= col4 if reverse else row4 >= col4
        la = _log_sigmoid(_mm_split(lr_ref[...], w, parts=2) + bias) * (1.0 / GLA_GATE_NORM)
        b = _mm_split(tri, la, a_exact=True)
        decay_all = jnp.exp(_mm_split(la, sel, dims=(((0,), (0,)), ((), ())), b_exact=True))
        q = qk_ref[:, 0:hk].astype(F32)
        k = qk_ref[:, hk:2 * hk].astype(F32)
        qd_all = q * (GLA_DK ** -0.5) * jnp.exp(b)
        kd_all = k * jnp.exp(-b)
        state = s_ref[...]
        order = range(n_chunks - 1, -1, -1) if reverse else range(n_chunks)
        for ci in order:
            rows = slice(ci * ck, (ci + 1) * ck)
            v = v_ref[rows, :]
            last = ci * ck if reverse else (ci + 1) * ck - 1
            ke = k[rows, :] * jnp.exp(b[last:last + 1, :] - b[rows, :])
            qd = qd_all[rows, :]
            qs = jnp.concatenate([qd * m for m in head_masks], axis=0).astype(BF16)
            att = jnp.where(amask, _mm_nt(qs, kd_all[rows, :]), 0.0)
            inter = _mm(qs, state)
            outs = []
            for h in range(GLA_HEADS):
                hr = slice(h * ck, (h + 1) * ck)
                outs.append(_mm(att[hr, :], v[:, h * GLA_DV:(h + 1) * GLA_DV]) + inter[hr, :])
            o_ref[rows, :] = jnp.concatenate(outs, axis=1)
            ds = _mm_tn(ke, v)
            ds = jnp.concatenate([ds[h * GLA_DK:(h + 1) * GLA_DK, h * GLA_DV:(h + 1) * GLA_DV]
                                  for h in range(GLA_HEADS)], axis=0)
            state = decay_all[:, ci * GLA_DV:(ci + 1) * GLA_DV] * state + ds
        s_ref[...] = state

    run(qkf, vf, lrf, wa_ref[:, 0:hk], ba_ref[:, 0:hk], sf_ref, of_ref, False)
    run(qkb, vb, lrb, wa_ref[:, hk:2 * hk], ba_ref[:, hk:2 * hk], sb_ref, ob_ref, True)


def _gla_scan(qk, v, lr, wa, ba, n_batch, tiles_per_batch, ctx_tiles):
    rows = qk.shape[0]
    tm = ROW_TILE
    tpb = tiles_per_batch

    def fwd(b, t):
        return (b * tpb + t, 0)

    def bwd(b, t):
        return (b * tpb + jnp.where(t < ctx_tiles, ctx_tiles - 1 - t, tpb - 1 - (t - ctx_tiles)), 0)

    wv = v.shape[1]
    return pl.pallas_call(
        _gla_body,
        grid=(n_batch, tpb),
        in_specs=[pl.BlockSpec((tm, qk.shape[1]), fwd), pl.BlockSpec((tm, wv), fwd),
                  pl.BlockSpec((tm, lr.shape[1]), fwd),
                  pl.BlockSpec((tm, qk.shape[1]), bwd), pl.BlockSpec((tm, wv), bwd),
                  pl.BlockSpec((tm, lr.shape[1]), bwd),
                  pl.BlockSpec(wa.shape, lambda b, t: (0, 0)),
                  pl.BlockSpec(ba.shape, lambda b, t: (0, 0))],
        out_specs=[pl.BlockSpec((tm, wv), fwd), pl.BlockSpec((tm, wv), bwd)],
        out_shape=[jax.ShapeDtypeStruct((rows, wv), F32)] * 2,
        scratch_shapes=[pltpu.VMEM((GLA_HEADS * GLA_DK, GLA_DV), F32)] * 2,
        compiler_params=_cparams(("arbitrary", "arbitrary"), 32),
        name="gla_scan",
    )(qk, v, lr, qk, v, lr, wa, ba)


def _s5_prep(a_re, a_im, log_dt, b_re, b_im, c_re, c_im, d):
    hp = lax.Precision.HIGHEST
    tt = S5_CHUNK
    n_groups = a_re.shape[1]
    bg = S5_BLOCK_GROUPS
    nb = n_groups // bg
    a_re, a_im, b_re, b_im = a_re.astype(F32), a_im.astype(F32), b_re.astype(F32), b_im.astype(F32)
    c_re, c_im = c_re.astype(F32), c_im.astype(F32)
    dt = jnp.exp(log_dt.astype(F32))[..., None]
    mag = jnp.exp(dt * a_re)
    ab_re, ab_im = mag * jnp.cos(dt * a_im), mag * jnp.sin(dt * a_im)
    den = a_re * a_re + a_im * a_im
    nr = ab_re - 1.0
    coef_re = (nr * a_re + ab_im * a_im) / den
    coef_im = (ab_im * a_re - nr * a_im) / den
    bb_re = coef_re[..., None] * b_re - coef_im[..., None] * b_im
    bb_im = coef_re[..., None] * b_im + coef_im[..., None] * b_re
    pr, pi = [jnp.ones_like(ab_re)], [jnp.zeros_like(ab_im)]
    for _ in range(tt):
        pr, pi = pr + [pr[-1] * ab_re - pi[-1] * ab_im], pi + [pr[-1] * ab_im + pi[-1] * ab_re]
    pr, pi = jnp.stack(pr), jnp.stack(pi)
    cpr = c_re[None] * pr[:, :, :, None, :] - c_im[None] * pi[:, :, :, None, :]
    cpi = c_re[None] * pi[:, :, :, None, :] + c_im[None] * pr[:, :, :, None, :]
    kern = (jnp.einsum('kdgcn,dgnm->kdgcm', cpr, bb_re, precision=hp)
            - jnp.einsum('kdgcn,dgnm->kdgcm', cpi, bb_im, precision=hp))
    kdim, half = tt * LANES, bg * S5_STATE
    ax_k, ax_s, ax_tc = jnp.arange(kdim), jnp.arange(half), jnp.arange(tt * S5_GROUP)
    k_group, k_step, k_chan = (ax_k // S5_GROUP) % bg, ax_k // LANES, ax_k % S5_GROUP
    rep_tc = ((ax_tc[:, None] // S5_GROUP == k_step[None, :])
              & (ax_tc[:, None] % S5_GROUP == k_chan[None, :])).astype(F32)
    rep_n = (jnp.arange(S5_STATE)[:, None] == ax_s[None, :] % S5_STATE).astype(F32)
    mask_kk = k_group[:, None] == k_group[None, :]
    mask_ks = k_group[:, None] == ax_s[None, :] // S5_STATE
    expand = lambda table, rep: jnp.einsum('jrk,kc->jrc', table, rep, precision=hp)

    s_idx = jnp.arange(tt)[:, None]
    t_idx = jnp.arange(tt)[None, :]
    lag_f = jnp.clip(t_idx - s_idx, 0, tt)
    lag_b = jnp.clip(s_idx - t_idx, 0, tt)
    toe = (jnp.where((t_idx >= s_idx)[:, :, None, None, None], kern[lag_f, 0], 0.0)
           + jnp.where((s_idx >= t_idx)[:, :, None, None, None], kern[lag_b, 1], 0.0))
    skip = (jnp.eye(tt, dtype=F32)[:, :, None, None, None]
            * d.astype(F32).reshape(n_groups, S5_GROUP)[None, None, :, :, None]
            * jnp.eye(S5_GROUP, dtype=F32)[None, None, None])
    toe = (toe + skip).reshape(tt, tt, nb, bg, S5_GROUP, S5_GROUP)
    toe = toe.transpose(2, 0, 3, 5, 1, 4).reshape(nb, kdim, tt * S5_GROUP)
    m_mat = jnp.where(mask_kk, expand(toe, rep_tc), 0.0)

    def inject(direction, power_of_s):
        p_re, p_im = pr[power_of_s, direction], pi[power_of_s, direction]
        v_re = p_re[..., None] * bb_re[direction][None] - p_im[..., None] * bb_im[direction][None]
        v_im = p_re[..., None] * bb_im[direction][None] + p_im[..., None] * bb_re[direction][None]
        def blk(v):
            v = v.reshape(tt, nb, bg, S5_STATE, S5_GROUP).transpose(1, 0, 2, 4, 3).reshape(nb, kdim, S5_STATE)
            return jnp.where(mask_ks, expand(v, rep_n), 0.0)
        return jnp.concatenate([blk(v_re), blk(v_im)], axis=-1)

    def readout(direction, power_of_t):
        e_re, e_im = cpr[power_of_t, direction], -cpi[power_of_t, direction]
        def blk(v):
            v = v.reshape(tt, nb, bg, S5_GROUP, S5_STATE).transpose(1, 2, 4, 0, 3).reshape(nb, half, tt * S5_GROUP)
            return jnp.where(mask_ks.T, expand(v, rep_tc), 0.0)
        return jnp.concatenate([blk(e_re), blk(e_im)], axis=1)

    steps = jnp.arange(tt)
    g_f = inject(0, tt - 1 - steps)
    g_b = inject(1, steps)
    e_f = readout(0, steps + 1)
    e_b = readout(1, tt - steps)
    a_t = jnp.stack([pr[tt, 0].reshape(nb, half), pi[tt, 0].reshape(nb, half),
                     pr[tt, 1].reshape(nb, half), pi[tt, 1].reshape(nb, half)], axis=1)
    cast = lambda z: z.astype(BF16)
    return cast(m_mat), cast(g_f), cast(g_b), cast(e_f), cast(e_b), a_t


def _s5_body(u_ref, m_ref, gf_ref, gb_ref, ef_ref, eb_ref, a_ref, y_ref, sf_ref, sb_ref,
             *, ctx_rows, n_rows, n_batch):
    half = S5_BLOCK_GROUPS * S5_STATE
    row_blocks = [slice(r * n_rows, (r + 1) * n_rows) for r in range(n_batch)]

    for rows in row_blocks:
        sf_ref[rows, :] = jnp.dot(u_ref[rows, :], gf_ref[0], preferred_element_type=F32)
        sb_ref[rows, :] = jnp.dot(u_ref[rows, :], gb_ref[0], preferred_element_type=F32)

    a = a_ref[0]
    afr, afi, abr, abi = a[0:1, :], a[1:2, :], a[2:3, :], a[3:4, :]

    sub = 8
    cpt = sub // n_batch
    n_tiles, ctx_tiles = n_rows // cpt, ctx_rows // cpt

    def advance(a_re, a_im, s_re, s_im, g):
        return a_re * s_re - a_im * s_im + g[:, 0:half], a_re * s_im + a_im * s_re + g[:, half:2 * half]

    def step(k, carry):
        fr, fi, br, bi = carry
        rf = pl.ds(pl.multiple_of(k * sub, sub), sub)
        g = sf_ref[rf, :]
        ins_r, ins_i = [], []
        for c in range(cpt):
            ins_r.append(fr)
            ins_i.append(fi)
            fr, fi = advance(afr, afi, fr, fi, g[c * n_batch:(c + 1) * n_batch, :])
        sf_ref[rf, 0:half] = jnp.concatenate(ins_r, axis=0)
        sf_ref[rf, half:2 * half] = jnp.concatenate(ins_i, axis=0)
        kb = jnp.where(k < ctx_tiles, ctx_tiles - 1 - k, n_tiles - 1 - (k - ctx_tiles))
        rb = pl.ds(pl.multiple_of(kb * sub, sub), sub)
        g = sb_ref[rb, :]
        ins_r, ins_i = [None] * cpt, [None] * cpt
        for c in range(cpt - 1, -1, -1):
            ins_r[c], ins_i[c] = br, bi
            br, bi = advance(abr, abi, br, bi, g[c * n_batch:(c + 1) * n_batch, :])
        sb_ref[rb, 0:half] = jnp.concatenate(ins_r, axis=0)
        sb_ref[rb, half:2 * half] = jnp.concatenate(ins_i, axis=0)
        return fr, fi, br, bi

    zero = jnp.zeros((n_batch, half), F32)
    lax.fori_loop(0, n_tiles, step, (zero, zero, zero, zero))

    for rows in row_blocks:
        y = (jnp.dot(u_ref[rows, :], m_ref[0], preferred_element_type=F32)
             + jnp.dot(sf_ref[rows, :].astype(BF16), ef_ref[0], preferred_element_type=F32)
             + jnp.dot(sb_ref[rows, :].astype(BF16), eb_ref[0], preferred_element_type=F32))
        y_ref[rows, :] = y.astype(y_ref.dtype)


def _s5_scan(u, mats, n_batch, ctx_len, layer):
    m_mat, g_f, g_b, e_f, e_b, a_t = mats
    rows, width = u.shape
    tt = S5_CHUNK
    nb = width // LANES
    srows = rows // tt // n_batch
    kdim = tt * LANES
    sdim = 2 * S5_BLOCK_GROUPS * S5_STATE
    u5 = u.reshape(n_batch, srows, tt, nb, LANES).transpose(3, 1, 0, 2, 4).reshape(nb, srows * n_batch, kdim)
    once = dict(pipeline_mode=pl.Buffered(1))
    wspec = lambda shape: pl.BlockSpec((None, 1) + shape, lambda j: (layer, j, 0, 0), **once)
    y5 = pl.pallas_call(
        functools.partial(_s5_body, ctx_rows=ctx_len // tt, n_rows=srows, n_batch=n_batch),
        grid=(nb,),
        in_specs=[pl.BlockSpec((None, srows * n_batch, kdim), lambda j: (j, 0, 0), **once),
                  wspec((kdim, kdim)), wspec((kdim, sdim)), wspec((kdim, sdim)),
                  wspec((sdim, kdim)), wspec((sdim, kdim)), wspec((4, sdim // 2))],
        out_specs=pl.BlockSpec((None, srows * n_batch, kdim), lambda j: (j, 0, 0)),
        out_shape=jax.ShapeDtypeStruct((nb, srows * n_batch, kdim), BF16),
        scratch_shapes=[pltpu.VMEM((srows * n_batch, sdim), F32), pltpu.VMEM((srows * n_batch, sdim), F32)],
        compiler_params=_cparams(("arbitrary",), 56),
        name="s5_scan",
    )(u5, m_mat, g_f, g_b, e_f, e_b, a_t)
    return y5.reshape(nb, srows, n_batch, tt, LANES).transpose(2, 1, 3, 0, 4).reshape(rows, width)


def _rope_tables(seq_len):
    pos = jnp.arange(seq_len)
    row = (pos // GRID_W).astype(F32)
    col = (pos % GRID_W).astype(F32)
    inv_freq = ROPE_BASE ** (-jnp.arange(ROPE_PAIRS, dtype=F32) / ROPE_PAIRS)
    ar, ac = row[:, None] * inv_freq, col[:, None] * inv_freq
    cos = jnp.concatenate([jnp.cos(ar), jnp.cos(ar), jnp.cos(ac), jnp.cos(ac)], axis=1)
    sin = jnp.concatenate([-jnp.sin(ar), jnp.sin(ar), -jnp.sin(ac), jnp.sin(ac)], axis=1)
    return cos, sin


def _swa_body(sink_ref, q_ref, kp_ref, kc_ref, kn_ref, kx_ref, cp_ref, sp_ref, cc_ref, sc_ref,
              cn_ref, sn_ref, qg_ref, kg_ref, o_ref, *, lat_blocks):
    n = pl.program_id(1)
    hd = SWA_HEAD_DIM
    blk = SWA_BLOCK
    kvw = SWA_KV_HEADS * hd
    scale = hd ** -0.5
    row = lax.broadcasted_iota(I32, (SWA_GROUP * blk, blk), 0) % blk
    col = lax.broadcasted_iota(I32, (SWA_GROUP * blk, blk), 1)
    qg, kg = qg_ref[...], kg_ref[...]
    ones_hd = jnp.ones((hd, hd), BF16)
    src = lax.broadcasted_iota(I32, (hd, hd), 0)
    dst = lax.broadcasted_iota(I32, (hd, hd), 1)
    partner = jnp.where(dst % (2 * ROPE_PAIRS) < ROPE_PAIRS, dst + ROPE_PAIRS, dst - ROPE_PAIRS)
    swap_halves = (src == partner).astype(BF16)

    def head_rms(x, gain):
        ss = _mm_split(x * x, ones_hd, parts=2, b_exact=True)
        return x * lax.rsqrt(ss * (1.0 / hd) + EPS) * gain

    def rope(x, cos, sin):
        return x * cos + _mm_split(x, swap_halves, parts=2, b_exact=True) * sin

    def lane_fold(op, blocks):
        parts = [b[:, c0:c0 + blk] for b in blocks for c0 in range(0, b.shape[1], blk)]
        out = parts[0]
        for part in parts[1:]:
            out = op(out, part)
        return out

    def attend(is_lat):
        for g in range(SWA_KV_HEADS):
            kcols = slice(g * hd, (g + 1) * hd)
            vcols = slice(kvw + g * hd, kvw + (g + 1) * hd)
            keys = [head_rms(kx_ref[:, kcols].astype(F32), kg).astype(BF16)]
            values = [kx_ref[:, vcols]]
            valid = [None]
            if is_lat:
                wins = ((kp_ref, cp_ref, sp_ref), (kc_ref, cc_ref, sc_ref), (kn_ref, cn_ref, sn_ref))
                keys += [rope(head_rms(r[:, kcols].astype(F32), kg), c[...], s[...]).astype(BF16)
                         for r, c, s in wins]
                values += [r[:, vcols] for r, _, _ in wins]
                valid += [(col >= row) & (col + (n - 1) * blk >= 0), None,
                          (col <= row) & (col + (n + 1) * blk < lat_blocks * blk)]
            qs, sinks = [], []
            for hh in range(SWA_GROUP):
                h = g * SWA_GROUP + hh
                q = head_rms(q_ref[:, h * hd:(h + 1) * hd].astype(F32), qg)
                if is_lat:
                    q = rope(q, cc_ref[...], sc_ref[...])
                qs.append((q * (scale * LOG2E)).astype(BF16))
                sinks.append(jnp.full((blk, 1), sink_ref[h] * LOG2E, F32))
            q4 = jnp.concatenate(qs, axis=0)
            sink = jnp.concatenate(sinks, axis=0)
            scores = []
            for kk, ok in zip(keys, valid):
                s = _mm_nt(q4, kk)
                scores.append(s if ok is None else jnp.where(ok, s, NEG_INF))
            m = jnp.maximum(jnp.max(lane_fold(jnp.maximum, scores), axis=-1, keepdims=True), sink)
            probs = [jnp.exp2(s - m).astype(BF16) for s in scores]
            den = jnp.exp2(sink - m)
            for p in probs:
                for c0 in range(0, p.shape[1], blk):
                    den = den + _mm(p[:, c0:c0 + blk], ones_hd)
            acc = None
            for p, vv in zip(probs, values):
                pv = _mm(p, vv)
                acc = pv if acc is None else acc + pv
            out = acc * (1.0 / den)
            for hh in range(SWA_GROUP):
                h = g * SWA_GROUP + hh
                o_ref[:, h * hd:(h + 1) * hd] = out[hh * blk:(hh + 1) * blk, :].astype(o_ref.dtype)

    @pl.when(n < lat_blocks)
    def _lat():
        attend(True)

    @pl.when(n >= lat_blocks)
    def _ctx():
        attend(False)


def _swa_attend(q, kv, cos, sin, q_gain, k_gain, sink, n_batch, seq_len, ctx_len):
    rows = q.shape[0]
    blk = SWA_BLOCK
    lat_blocks = seq_len // blk
    ctx_blocks = ctx_len // blk
    bpb = lat_blocks + ctx_blocks
    hd = SWA_HEAD_DIM

    def q_map(b, n, s):
        return (b * bpb + jnp.where(n < lat_blocks, ctx_blocks + n, n - lat_blocks), 0)

    def win(off):
        def kv_map(b, n, s):
            return (b * bpb + ctx_blocks + jnp.clip(n + off, 0, lat_blocks - 1), 0)

        def tab_map(b, n, s):
            return (jnp.clip(n + off, 0, lat_blocks - 1), 0)
        return kv_map, tab_map

    (kv_p, tab_p), (kv_c, tab_c), (kv_n, tab_n) = win(-1), win(0), win(1)
    kvs = lambda m: pl.BlockSpec((blk, kv.shape[1]), m)
    tab = lambda m: pl.BlockSpec((blk, hd), m)
    grid_spec = pltpu.PrefetchScalarGridSpec(
        num_scalar_prefetch=1,
        grid=(n_batch, bpb),
        in_specs=[pl.BlockSpec((blk, q.shape[1]), q_map),
                  kvs(kv_p), kvs(kv_c), kvs(kv_n),
                  pl.BlockSpec((ctx_len, kv.shape[1]), lambda b, n, s: (b * (bpb * blk // ctx_len), 0)),
                  tab(tab_p), tab(tab_p), tab(tab_c), tab(tab_c), tab(tab_n), tab(tab_n),
                  pl.BlockSpec((1, hd), lambda b, n, s: (0, 0)),
                  pl.BlockSpec((1, hd), lambda b, n, s: (0, 0))],
        out_specs=pl.BlockSpec((blk, q.shape[1]), q_map),
    )
    return pl.pallas_call(
        functools.partial(_swa_body, lat_blocks=lat_blocks),
        grid_spec=grid_spec,
        out_shape=jax.ShapeDtypeStruct((rows, q.shape[1]), BF16),
        compiler_params=_cparams(("arbitrary", "arbitrary"), 32),
        name="swa_attend",
    )(sink.astype(F32), q, kv, kv, kv, kv, cos, sin, cos, sin, cos, sin,
      q_gain.reshape(1, hd).astype(F32), k_gain.reshape(1, hd).astype(F32))


def _route_rows(logits_t, bias_col):
    aff = _sigmoid(logits_t)
    biased = aff + bias_col
    v = [biased[e:e + 1, :] for e in range(N_EXPERTS)]
    a = [aff[e:e + 1, :] for e in range(N_EXPERTS)]
    gsz = EXPERTS_PER_GROUP
    best_g = best_s = None
    for g in range(N_EXPERT_GROUPS):
        vg = v[g * gsz:(g + 1) * gsz]
        score = None
        for i in range(gsz):
            for k in range(i + 1, gsz):
                pair = vg[i] + vg[k]
                score = pair if score is None else jnp.maximum(score, pair)
        if best_g is None:
            best_g, best_s = jnp.zeros_like(score, dtype=I32), score
        else:
            upd = score > best_s
            best_g = jnp.where(upd, g, best_g)
            best_s = jnp.where(upd, score, best_s)

    def pick(rows):
        out = []
        for i in range(gsz):
            x = rows[i]
            for g in range(1, N_EXPERT_GROUPS):
                x = jnp.where(best_g == g, rows[g * gsz + i], x)
            out.append(x)
        return out

    vb, ab = pick(v), pick(a)
    i1, v1, a1 = jnp.zeros_like(best_g), vb[0], ab[0]
    for i in range(1, gsz):
        upd = vb[i] > v1
        i1, v1, a1 = jnp.where(upd, i, i1), jnp.where(upd, vb[i], v1), jnp.where(upd, ab[i], a1)
    i2 = v2 = a2 = None
    for i in range(gsz):
        cand = jnp.where(i1 == i, -jnp.inf, vb[i])
        if i2 is None:
            i2, v2, a2 = jnp.zeros_like(best_g), cand, ab[0]
        else:
            upd = cand > v2
            i2, v2, a2 = jnp.where(upd, i, i2), jnp.where(upd, cand, v2), jnp.where(upd, ab[i], a2)
    tot = a1 + a2
    e1 = (best_g * gsz + i1).astype(F32)
    e2 = (best_g * gsz + i2).astype(F32)
    zeros = jnp.zeros((4, e1.shape[1]), F32)
    return jnp.concatenate([e1, e2, a1 / tot, a2 / tot, zeros], axis=0)


def _out_body(of_ref, ob_ref, r_ref, ys_ref, sw_ref, x_ref, g1_ref, sh_ref, sc_ref, n2_ref,
              og_ref, wglu_ref, bglu_ref, wo_ref, rw_ref, rb_ref, cast_in, xo_ref, h_ref, rt_ref, cast_out):
    _cast_blocks([cast_in], [cast_out])
    o = of_ref[...] + ob_ref[...]
    r = r_ref[...].astype(F32)
    gain = og_ref[...]
    heads = []
    for h in range(GLA_HEADS):
        cols = slice(h * GLA_DV, (h + 1) * GLA_DV)
        heads.append(_rms(o[:, cols], gain[:, cols]))
    gla_y = jnp.concatenate(heads, axis=1) * _silu(r)
    z = _gelu_tanh(ys_ref[...].astype(F32))
    s5_y = z * _sigmoid(_mm(z, wglu_ref[...]) + bglu_ref[...])
    mixed = jnp.concatenate([gla_y.astype(BF16), s5_y.astype(BF16), sw_ref[...]], axis=1)
    y = jnp.dot(mixed, wo_ref[...], preferred_element_type=F32)
    x_new = x_ref[...] + g1_ref[0] * y
    xo_ref[...] = x_new
    h2 = _rms(x_new, n2_ref[...]) * (1.0 + sc_ref[0]) + sh_ref[0]
    h_ref[...] = h2
    logits_t = _mm_split(rw_ref[...], h2, dims=(((1,), (1,)), ((), ())), parts=2)
    rt_ref[0] = _route_rows(logits_t, rb_ref[...])


def _out_proj(o_f, o_b, r, ys, sw, xall, mod3, norm2, o_gain, w_glu, b_glu, w_out, rw_t, rb_col,
              tiles_per_batch, ctx_tiles, n_batch, layer, cast):
    rows, d = xall.shape
    tm = ROW_TILE
    nt = rows // tm
    mrow = functools.partial(_mod_row, tiles_per_batch=tiles_per_batch, ctx_tiles=ctx_tiles,
                             n_batch=n_batch, layer=layer)
    rowspec = lambda a: pl.BlockSpec((tm, a.shape[1]), lambda i: (i, 0))
    const = lambda a: pl.BlockSpec(a.shape, lambda i: (0,) * a.ndim)
    layered = lambda a, **kw: pl.BlockSpec((None,) + a.shape[1:], lambda i: (layer,) + (0,) * (a.ndim - 1), **kw)
    modspec = lambda k: pl.BlockSpec((1, 1, d), lambda i: (mrow(i) * 6 + k, 0, 0))
    consts = [norm2.reshape(1, d), o_gain.reshape(1, -1)]
    cast_in, cast_out, cast_shapes = _cast_specs([cast], layer, nt)
    return pl.pallas_call(
        _out_body,
        grid=(nt,),
        in_specs=[rowspec(o_f), rowspec(o_b), rowspec(r), rowspec(ys), rowspec(sw), rowspec(xall),
                  modspec(2), modspec(3), modspec(4)]
                 + [const(a) for a in consts]
                 + [layered(w_glu), const(b_glu.reshape(1, -1)),
                    layered(w_out, pipeline_mode=pl.Buffered(1)), const(rw_t), const(rb_col)] + cast_in,
        out_specs=[pl.BlockSpec((tm, d), lambda i: (i, 0)), pl.BlockSpec((tm, d), lambda i: (i, 0)),
                   pl.BlockSpec((1, 8, tm), lambda i: (i, 0, 0))] + cast_out,
        out_shape=[jax.ShapeDtypeStruct((rows, d), F32), jax.ShapeDtypeStruct((rows, d), F32),
                   jax.ShapeDtypeStruct((nt, 8, tm), F32)] + cast_shapes,
        compiler_params=_cparams(("arbitrary",), 48),
        name="out_proj",
    )(o_f, o_b, r, ys, sw, xall, mod3, mod3, mod3, *consts, w_glu, b_glu.reshape(1, -1), w_out, rw_t, rb_col,
      cast)


def _row_copy(src_hbm, row, dst, slot, sem):
    return pltpu.make_async_copy(src_hbm.at[pl.ds(row, 1)], dst.at[pl.ds(slot, 1)], sem)


def _gather_start(src_hbm, idx_ref, dst, sem, n_rows, unrolled=False):
    if unrolled:
        for r in range(n_rows):
            _row_copy(src_hbm, idx_ref[0, 0, r], dst, r, sem).start()
        return

    def body(g, carry):
        for u in range(DMA_UNROLL):
            r = g * DMA_UNROLL + u
            _row_copy(src_hbm, idx_ref[0, 0, r], dst, r, sem).start()
        return carry
    lax.fori_loop(0, n_rows // DMA_UNROLL, body, 0)


def _gather_wait(src_hbm, dst, sem, n_rows):
    for r in range(n_rows):
        _row_copy(src_hbm, 0, dst, r, sem).wait()


def _ring_gather(step, n_steps, src_hbm, idx_refs, ring, sem, work):
    depth = GATHER_RING
    n_rows = ring.shape[1]

    @pl.when(step == 0)
    def _prime():
        for k in range(depth - 1):
            _gather_start(src_hbm, idx_refs[k], ring.at[k], sem.at[k], n_rows)

    slot = step % depth
    ahead = (step + depth - 1) % depth
    _gather_wait(src_hbm, ring.at[slot], sem.at[slot], n_rows)
    work(ring.at[slot],
         lambda unrolled: _gather_start(src_hbm, idx_refs[depth - 1], ring.at[ahead], sem.at[ahead], n_rows,
                                        unrolled))

    @pl.when(step == n_steps - 1)
    def _drain():
        for k in range(1, depth):
            late = (step + k) % depth
            _gather_wait(src_hbm, ring.at[late], sem.at[late], n_rows)


PLAN_EXPERT, PLAN_VALID, PLAN_FIRST, PLAN_SLOT, PLAN_NEXT, PLAN_HAS_NEXT = range(6)


def _moe_body(plan_ref, idx0_ref, idx1_ref, idx2_ref, x_hbm, wg_hbm, wu_hbm, wd_hbm, o_ref,
              ring, sem, wg_buf, wu_buf, wd_buf, wsem, *, n_steps):
    i = pl.program_id(0)
    expert = plan_ref[PLAN_EXPERT, i]
    wslot = plan_ref[PLAN_SLOT, i]

    def weight_copies(e, slot):
        pairs = ((wg_hbm, wg_buf), (wu_hbm, wu_buf), (wd_hbm, wd_buf))
        return [pltpu.make_async_copy(src.at[e], dst.at[slot], wsem.at[slot]) for src, dst in pairs]

    @pl.when(i == 0)
    def _first_weights():
        for c in weight_copies(expert, 0):
            c.start()

    @pl.when(plan_ref[PLAN_FIRST, i] > 0)
    def _switch():
        for c in weight_copies(expert, wslot):
            c.wait()

        @pl.when(plan_ref[PLAN_HAS_NEXT, i] > 0)
        def _prefetch():
            for c in weight_copies(plan_ref[PLAN_NEXT, i], 1 - wslot):
                c.start()

    def work(cur, start_next):
        @pl.when(plan_ref[PLAN_VALID, i] > 0)
        def _active():
            start_next(True)
            x = cur[...].astype(BF16)
            gate = jnp.dot(x, wg_buf[wslot], preferred_element_type=F32)
            up = jnp.dot(x, wu_buf[wslot], preferred_element_type=F32)
            act = (_silu(gate) * up).astype(BF16)
            o_ref[...] = jnp.dot(act, wd_buf[wslot], preferred_element_type=F32)

        @pl.when(plan_ref[PLAN_VALID, i] == 0)
        def _idle():
            start_next(False)
            o_ref[...] = jnp.zeros_like(o_ref)

    _ring_gather(i, n_steps, x_hbm, (idx0_ref, idx1_ref, idx2_ref), ring, sem, work)


def _moe_experts(plan, slot_token, tokens, w_gate, w_up, w_down):
    tm = MOE_TILE
    nt = plan.shape[1]
    d = tokens.shape[1]
    de = w_gate.shape[2]
    idx = lambda off: pl.BlockSpec((1, 1, tm), lambda i, plan: (jnp.minimum(i + off, nt - 1), 0, 0),
                                   memory_space=pltpu.SMEM)
    hbm = pl.BlockSpec(memory_space=pl.ANY)
    grid_spec = pltpu.PrefetchScalarGridSpec(
        num_scalar_prefetch=1,
        grid=(nt,),
        in_specs=[idx(k) for k in range(GATHER_RING)] + [hbm, hbm, hbm, hbm],
        out_specs=pl.BlockSpec((tm, d), lambda i, plan: (i, 0)),
        scratch_shapes=[pltpu.VMEM((GATHER_RING, tm, d), F32), pltpu.SemaphoreType.DMA((GATHER_RING,)),
                        pltpu.VMEM((2, d, de), BF16), pltpu.VMEM((2, d, de), BF16), pltpu.VMEM((2, de, d), BF16),
                        pltpu.SemaphoreType.DMA((2,))],
    )
    slots = slot_token.reshape(nt, 1, tm)
    return pl.pallas_call(
        functools.partial(_moe_body, n_steps=nt),
        grid_spec=grid_spec,
        out_shape=jax.ShapeDtypeStruct((nt * tm, d), F32),
        compiler_params=_cparams(("arbitrary",), 56),
        name="moe_experts",
    )(plan, *([slots] * GATHER_RING), tokens, w_gate, w_up, w_down)


def _combine_body(pos0_ref, pos1_ref, pos2_ref, x_ref, g2_ref, w_ref, y_hbm, o_ref, ring, sem, *, n_steps):
    tm = x_ref.shape[0]

    def work(cur, start_next):
        start_next(True)
        w = w_ref[...]
        f = w[:, 0:1] * cur[0:tm, :] + w[:, 1:2] * cur[tm:2 * tm, :]
        o_ref[...] = x_ref[...] + g2_ref[0] * f

    _ring_gather(pl.program_id(0), n_steps, y_hbm, (pos0_ref, pos1_ref, pos2_ref), ring, sem, work)


def _moe_combine(pos, x_new, mod3, wts, y_sorted, n_batch, tiles_per_batch, ctx_tiles, keep_ctx, layer):
    rows, d = x_new.shape
    tm = ROW_TILE
    tpb = tiles_per_batch
    first = 0 if keep_ctx else ctx_tiles
    out_tpb = tpb - first
    ns = n_batch * out_tpb
    in_tile = lambda s: (s // out_tpb) * tpb + first + s % out_tpb
    mrow = lambda s: layer * MOD_ROWS + jnp.where(first + s % out_tpb < ctx_tiles, n_batch, s // out_tpb)
    pos3 = pos.reshape(rows // tm, tm, 2).transpose(0, 2, 1).reshape(rows // tm, 1, 2 * tm)
    idx = lambda off: pl.BlockSpec((1, 1, 2 * tm), lambda s: (in_tile(jnp.minimum(s + off, ns - 1)), 0, 0),
                                   memory_space=pltpu.SMEM)
    return pl.pallas_call(
        functools.partial(_combine_body, n_steps=ns),
        grid=(ns,),
        in_specs=[idx(k) for k in range(GATHER_RING)] + [
                  pl.BlockSpec((tm, d), lambda s: (in_tile(s), 0)),
                  pl.BlockSpec((1, 1, d), lambda s: (mrow(s) * 6 + 5, 0, 0)),
                  pl.BlockSpec((tm, 2), lambda s: (in_tile(s), 0)),
                  pl.BlockSpec(memory_space=pl.ANY)],
        out_specs=pl.BlockSpec((tm, d), lambda s: (s, 0)),
        out_shape=jax.ShapeDtypeStruct((ns * tm, d), F32),
        scratch_shapes=[pltpu.VMEM((GATHER_RING, 2 * tm, d), F32), pltpu.SemaphoreType.DMA((GATHER_RING,))],
        compiler_params=_cparams(("arbitrary",), 40),
        name="moe_combine",
    )(*([pos3] * GATHER_RING), x_new, mod3, wts, y_sorted)


def _moe_plan(route, n_rows):
    tm = MOE_TILE
    e = route[:, 0:2, :].astype(I32).transpose(0, 2, 1).reshape(n_rows, 2)
    wts = route[:, 2:4, :].transpose(0, 2, 1).reshape(n_rows, 2)
    e_flat = e.reshape(-1)
    onehot = (e_flat[:, None] == jnp.arange(N_EXPERTS, dtype=I32)[None, :]).astype(I32)
    csum = jnp.cumsum(onehot, axis=0)
    rank = jnp.sum(onehot * csum, axis=1) - 1
    counts = csum[-1]
    tiles_e = (counts + tm - 1) // tm
    tile_end = jnp.cumsum(tiles_e)
    tile_start = tile_end - tiles_e
    pos = tile_start[e_flat] * tm + rank
    nt = (2 * n_rows) // tm + N_EXPERTS
    slot_token = jnp.zeros((nt * tm,), I32).at[pos].set(jnp.arange(2 * n_rows, dtype=I32) // 2,
                                                        unique_indices=True)
    n_active = tile_end[-1]
    tile_id = jnp.minimum(jnp.arange(nt, dtype=I32), n_active - 1)
    tile_expert = jnp.minimum(jnp.sum((tile_id[:, None] >= tile_end[None, :]).astype(I32), axis=1),
                              N_EXPERTS - 1).astype(I32)
    tile_ids = jnp.arange(nt, dtype=I32)
    tile_valid = (tile_ids < n_active).astype(I32)
    first = jnp.concatenate([jnp.ones((1,), I32), (tile_expert[1:] != tile_expert[:-1]).astype(I32)])
    wslot = (jnp.cumsum(first) - 1) % 2
    later_first = (tile_ids[None, :] > tile_ids[:, None]) & (first[None, :] > 0)
    next_first = jnp.min(jnp.where(later_first, tile_ids[None, :], nt), axis=1)
    has_next = (next_first < nt).astype(I32)
    next_expert = tile_expert[jnp.minimum(next_first, nt - 1)]
    plan = jnp.stack([tile_expert, tile_valid, first, wslot.astype(I32), next_expert, has_next]).astype(I32)
    return plan, slot_token, pos.reshape(n_rows, 2), wts


def kernel(x, c, ctx, c_ctx, ada_w, ada_b, norm1, norm2, w_in, gla_w_alpha, gla_b_alpha, gla_o_norm,
           s5_a_re, s5_a_im, s5_log_dt, s5_b_re, s5_b_im, s5_c_re, s5_c_im, s5_d, s5_w_glu, s5_b_glu,
           swa_q_norm, swa_k_norm, swa_sink, w_out, router_w, router_b, exp_w_gate, exp_w_up, exp_w_down):
    n_batch, seq_len, d = x.shape
    ctx_len = ctx.shape[1]
    depth = ada_w.shape[0]
    tm = ROW_TILE
    per_batch = ctx_len + seq_len
    assert ctx_len % tm == 0 and seq_len % tm == 0
    tpb, ctx_tiles = per_batch // tm, ctx_len // tm
    rows = n_batch * per_batch

    hk = GLA_HEADS * GLA_DK
    gla_w = GLA_HEADS * GLA_DV
    s5_w = s5_d.shape[1]
    swa_w = SWA_HEADS * SWA_HEAD_DIM
    kv_w = SWA_KV_HEADS * SWA_HEAD_DIM
    o_q, o_k, o_v = 0, hk, 2 * hk
    o_lr = o_v + gla_w
    o_r = o_lr + 2 * GLA_RANK
    o_u = o_r + gla_w
    o_sq = o_u + s5_w
    o_sk = o_sq + swa_w
    outs = [(2 * hk, BF16), (gla_w, BF16), (gla_w, BF16), (s5_w, BF16), (swa_w, BF16), (2 * kv_w, BF16),
            (LANES, F32)]

    xall = jnp.concatenate([ctx, x], axis=1).reshape(rows, d)
    cvec = jnp.concatenate([c, c_ctx[None], jnp.zeros((8 - n_batch - 1, d), c.dtype)], axis=0)
    cos, sin = _rope_tables(seq_len)
    rw_t = jnp.zeros((LANES, d), F32).at[:N_EXPERTS].set(router_w.astype(F32).T)
    rb_col = jnp.zeros((LANES, 1), F32).at[:N_EXPERTS, 0].set(router_b.astype(F32))

    mod3 = _ada_mod(cvec, ada_w, ada_b).reshape(depth * MOD_ROWS * 6, 1, d)
    w_prep = jnp.concatenate(
        [w_in[:, :, o_q:o_v], w_in[:, :, o_v:o_lr], w_in[:, :, o_r:o_u], w_in[:, :, o_u:o_sq],
         w_in[:, :, o_sq:o_sk], w_in[:, :, o_sk:], w_in[:, :, o_lr:o_r],
         jnp.zeros((depth, d, LANES - 2 * GLA_RANK), w_in.dtype)], axis=2).astype(BF16)
    w_glu16, w_out16 = s5_w_glu.astype(BF16), w_out.astype(BF16)
    n_exp, _, d_exp = exp_w_gate.shape[1:]
    wg_blk = exp_w_gate.reshape(depth, CAST_BLOCKS, n_exp * d // CAST_BLOCKS, d_exp)
    wu_blk = exp_w_up.reshape(depth, CAST_BLOCKS, n_exp * d // CAST_BLOCKS, d_exp)
    wd_blk = exp_w_down.reshape(depth, CAST_BLOCKS, n_exp * d_exp // CAST_BLOCKS, d)
    s5_mats = jax.vmap(_s5_prep)(s5_a_re, s5_a_im, s5_log_dt, s5_b_re, s5_b_im, s5_c_re, s5_c_im, s5_d)

    out = None
    for layer in range(depth):
        last = layer == depth - 1
        (qk, gv, gr, u, sq, skv, lr), (wg16, wu16) = _in_proj(
            xall, mod3, norm1[layer], w_prep, outs, tpb, ctx_tiles, n_batch, layer, [wg_blk, wu_blk])

        wa = jnp.zeros((LANES, 2 * hk), F32)
        wa = wa.at[0:GLA_RANK, 0:hk].set(gla_w_alpha[layer, 0].astype(F32))
        wa = wa.at[GLA_RANK:2 * GLA_RANK, hk:].set(gla_w_alpha[layer, 1].astype(F32))
        ba = gla_b_alpha[layer].astype(F32).reshape(1, 2 * hk)
        o_f, o_b = _gla_scan(qk, gv, lr, wa, ba, n_batch, tpb, ctx_tiles)

        ys = _s5_scan(u, s5_mats, n_batch, ctx_len, layer)

        sw = _swa_attend(sq, skv, cos, sin, swa_q_norm[layer], swa_k_norm[layer], swa_sink[layer],
                         n_batch, seq_len, ctx_len)

        x_new, h2, route, wd16 = _out_proj(o_f, o_b, gr, ys, sw, xall, mod3, norm2[layer], gla_o_norm[layer],
                                           w_glu16, s5_b_glu[layer].astype(F32), w_out16, rw_t, rb_col,
                                           tpb, ctx_tiles, n_batch, layer, wd_blk)

        plan, slot_token, pos, wts = _moe_plan(route, rows)
        y_sorted = _moe_experts(plan, slot_token, h2,
                                wg16.reshape(n_exp, d, d_exp), wu16.reshape(n_exp, d, d_exp),
                                wd16.reshape(n_exp, d_exp, d))
        out = _moe_combine(pos, x_new, mod3, wts, y_sorted, n_batch, tpb, ctx_tiles, keep_ctx=not last,
                           layer=layer)
        xall = out
    return out.reshape(n_batch, seq_len, d)
```

```python
import functools
import math

import jax
import jax.numpy as jnp
from jax import lax
from jax.experimental import pallas as pl
from jax.experimental.pallas import tpu as pltpu

F32 = jnp.float32
BF16 = jnp.bfloat16
I32 = jnp.int32

EPS = 1e-6
NEG_INF = -1e30
LOG2E = math.log2(math.e)

LANES = 128
VMEM_BYTES = 64 * 1024 * 1024

ROW_TILE = 256
MOD_ROWS = 8

GLA_HEADS = 4
GLA_DK = 64
GLA_DV = 128
GLA_RANK = 16
GLA_GATE_NORM = 16.0
GLA_CHUNK = 64

S5_GROUP = 16
S5_STATE = 64
S5_CHUNK = 8
S5_BLOCK_GROUPS = LANES // S5_GROUP

SWA_HEAD_DIM = 128
SWA_HEADS = 8
SWA_KV_HEADS = 2
SWA_GROUP = SWA_HEADS // SWA_KV_HEADS
SWA_BLOCK = 128
ROPE_PAIRS = 32
ROPE_BASE = 10000.0
GRID_W = 64

N_EXPERTS = 16
N_EXPERT_GROUPS = 4
EXPERTS_PER_GROUP = 4
MOE_TILE = 256
DMA_UNROLL = 8
GATHER_RING = 3
CAST_BLOCKS = 64


def _cparams(semantics, vmem_mb):
    return pltpu.CompilerParams(dimension_semantics=semantics,
                                vmem_limit_bytes=vmem_mb * 1024 * 1024)


def _mm(a, b):
    return jnp.dot(a.astype(BF16), b.astype(BF16), preferred_element_type=F32)


def _mm_nt(a, b):
    return lax.dot_general(a.astype(BF16), b.astype(BF16), (((1,), (1,)), ((), ())),
                           preferred_element_type=F32)


def _mm_tn(a, b):
    return lax.dot_general(a.astype(BF16), b.astype(BF16), (((0,), (0,)), ((), ())),
                           preferred_element_type=F32)


def _split(a, parts):
    out, rest = [], a
    for p in range(parts):
        piece = rest.astype(BF16)
        out.append(piece)
        if p + 1 < parts:
            rest = rest - piece.astype(F32)
    return out


def _mm_split(a, b, dims=(((1,), (0,)), ((), ())), parts=3, a_exact=False, b_exact=False):
    pa = [a.astype(BF16)] if a_exact else _split(a, parts)
    pb = [b.astype(BF16)] if b_exact else _split(b, parts)
    out = None
    for i, ai in enumerate(pa):
        for j, bj in enumerate(pb):
            if i + j < parts:
                t = lax.dot_general(ai, bj, dims, preferred_element_type=F32)
                out = t if out is None else out + t
    return out


def _sigmoid(x):
    return 1.0 / (1.0 + jnp.exp(-x))


def _silu(x):
    return x * _sigmoid(x)


def _log_sigmoid(x):
    return jnp.minimum(x, 0.0) - jnp.log(1.0 + jnp.exp(-jnp.abs(x)))


def _gelu_tanh(x):
    return 0.5 * x * (1.0 + jnp.tanh(math.sqrt(2.0 / math.pi) * (x + 0.044715 * (x * x * x))))


def _rms(x, gain):
    return x * lax.rsqrt(jnp.mean(x * x, axis=-1, keepdims=True) + EPS) * gain


def _ada_body(c_ref, w_ref, b_ref, o_ref):
    s = _silu(c_ref[...])
    o_ref[...] = _mm(s, w_ref[...]) + b_ref[...]


def _ada_mod(cvec, w, b):
    rows, d = cvec.shape
    depth, _, n = w.shape
    tn = 1024
    return pl.pallas_call(
        _ada_body,
        grid=(depth, n // tn),
        in_specs=[pl.BlockSpec((rows, d), lambda l, i: (0, 0)),
                  pl.BlockSpec((None, d, tn), lambda l, i: (l, 0, i)),
                  pl.BlockSpec((None, 1, tn), lambda l, i: (l, 0, i))],
        out_specs=pl.BlockSpec((None, rows, tn), lambda l, i: (l, 0, i)),
        out_shape=jax.ShapeDtypeStruct((depth, rows, n), F32),
        compiler_params=_cparams(("arbitrary", "arbitrary"), 40),
        name="ada_mod",
    )(cvec, w, b.reshape(depth, 1, n))


def _mod_row(i, tiles_per_batch, ctx_tiles, n_batch, layer):
    return layer * MOD_ROWS + jnp.where(i % tiles_per_batch < ctx_tiles, n_batch, i // tiles_per_batch)


def _cast_specs(casts, layer, n_steps):
    in_specs, out_specs, out_shapes = [], [], []
    for a in casts:
        nblk, r, c = a.shape[1:]
        assert n_steps >= nblk
        in_specs.append(pl.BlockSpec((None, None, r, c),
                                     lambda i, nblk=nblk: (layer, jnp.minimum(i, nblk - 1), 0, 0)))
        out_specs.append(pl.BlockSpec((None, r, c), lambda i, nblk=nblk: (jnp.minimum(i, nblk - 1), 0, 0)))
        out_shapes.append(jax.ShapeDtypeStruct((nblk, r, c), BF16))
    return in_specs, out_specs, out_shapes


def _cast_blocks(src_refs, dst_refs):
    for src, dst in zip(src_refs, dst_refs):
        dst[...] = src[...].astype(dst.dtype)


def _proj_body(*refs, splits, n_cast, two_sources, ctx_tiles, tiles_per_batch, heads):
    refs = list(refs)
    x_ref = refs.pop(0)
    ctx_ref = refs.pop(0) if two_sources else None
    sh_ref, sc_ref, g_ref, w_ref, qg_ref, kg_ref, cos_ref, sin_ref = refs[:8]
    refs = refs[8:]
    cast_in, out_refs, cast_out = refs[:n_cast], refs[n_cast:len(refs) - n_cast], refs[len(refs) - n_cast:]
    x = x_ref[...]
    if two_sources:
        x = jnp.where(pl.program_id(0) % tiles_per_batch < ctx_tiles, ctx_ref[...], x)
    y = _rms(x, g_ref[...])
    h = (y * (1.0 + sc_ref[0]) + sh_ref[0]).astype(BF16)
    ones_hd, swap_halves = _head_mats()
    cos, sin = cos_ref[...], sin_ref[...]
    hd = SWA_HEAD_DIM
    for k, (o_ref, (c0, width)) in enumerate(zip(out_refs, splits)):
        acc = jnp.dot(h, w_ref[:, c0:c0 + width], preferred_element_type=F32)
        if k in heads:
            n_heads, gain_ref, scale = heads[k][0], (qg_ref, kg_ref)[heads[k][1]], heads[k][2]
            for hh in range(n_heads):
                cols = slice(hh * hd, (hh + 1) * hd)
                rot = _head_rms_rope(acc[:, cols], gain_ref[...], cos, sin, ones_hd, swap_halves)
                o_ref[:, cols] = (rot * scale).astype(o_ref.dtype)
            if n_heads * hd < width:
                o_ref[:, n_heads * hd:] = acc[:, n_heads * hd:].astype(o_ref.dtype)
        else:
            o_ref[...] = acc.astype(o_ref.dtype)
    _cast_blocks(cast_in, cast_out)


def _row_sources(x_src, tiles_per_batch, ctx_tiles):
    tm = ROW_TILE
    if not isinstance(x_src, tuple):
        return [x_src], [pl.BlockSpec((tm, x_src.shape[1]), lambda i: (i, 0))]
    lat, ctxa = x_src
    lat_tiles = tiles_per_batch - ctx_tiles

    def lat_map(i):
        return ((i // tiles_per_batch) * lat_tiles + jnp.maximum(i % tiles_per_batch - ctx_tiles, 0), 0)

    def ctx_map(i):
        return ((i // tiles_per_batch) * ctx_tiles + jnp.minimum(i % tiles_per_batch, ctx_tiles - 1), 0)
    return [lat, ctxa], [pl.BlockSpec((tm, lat.shape[1]), lat_map), pl.BlockSpec((tm, ctxa.shape[1]), ctx_map)]


def _in_proj(x_src, rows, mod3, gain, w, outs, heads, q_gain, k_gain, cos_tab, sin_tab,
             tiles_per_batch, ctx_tiles, n_batch, layer, casts):
    d = w.shape[1]
    tm = ROW_TILE
    hd = SWA_HEAD_DIM
    mrow = functools.partial(_mod_row, tiles_per_batch=tiles_per_batch, ctx_tiles=ctx_tiles,
                             n_batch=n_batch, layer=layer)
    splits, out_specs, out_shapes = [], [], []
    c0 = 0
    for width, dtype in outs:
        splits.append((c0, width))
        out_specs.append(pl.BlockSpec((tm, width), lambda i: (i, 0)))
        out_shapes.append(jax.ShapeDtypeStruct((rows, width), dtype))
        c0 += width
    cast_in, cast_out, cast_shapes = _cast_specs(casts, layer, rows // tm)
    x_args, x_specs = _row_sources(x_src, tiles_per_batch, ctx_tiles)
    tab = pl.BlockSpec((tm, hd), lambda i: (i % tiles_per_batch, 0))
    res = pl.pallas_call(
        functools.partial(_proj_body, splits=tuple(splits), n_cast=len(casts), two_sources=len(x_args) == 2,
                          ctx_tiles=ctx_tiles, tiles_per_batch=tiles_per_batch, heads=heads),
        grid=(rows // tm,),
        in_specs=x_specs + [
                  pl.BlockSpec((1, 1, d), lambda i: (mrow(i) * 6 + 0, 0, 0)),
                  pl.BlockSpec((1, 1, d), lambda i: (mrow(i) * 6 + 1, 0, 0)),
                  pl.BlockSpec((1, d), lambda i: (0, 0)),
                  pl.BlockSpec((None,) + w.shape[1:], lambda i: (layer, 0, 0),
                               pipeline_mode=pl.Buffered(1)),
                  pl.BlockSpec((1, hd), lambda i: (0, 0)), pl.BlockSpec((1, hd), lambda i: (0, 0)),
                  tab, tab] + cast_in,
        out_specs=out_specs + cast_out,
        out_shape=out_shapes + cast_shapes,
        compiler_params=_cparams(("arbitrary",), 56),
        name="in_proj",
    )(*x_args, mod3, mod3, gain.reshape(1, d), w, q_gain.reshape(1, hd).astype(F32),
      k_gain.reshape(1, hd).astype(F32), cos_tab, sin_tab, *casts)
    return res[:len(outs)], res[len(outs):]


def _gla_body(qkf, vf, lrf, qkb, vb, lrb, wa_ref, ba_ref, of_ref, ob_ref, sf_ref, sb_ref):
    t = pl.program_id(1)
    ck = GLA_CHUNK
    hk = GLA_HEADS * GLA_DK

    @pl.when(t == 0)
    def _init():
        sf_ref[...] = jnp.zeros_like(sf_ref)
        sb_ref[...] = jnp.zeros_like(sb_ref)

    tmr = qkf.shape[0]
    n_chunks = tmr // ck
    rowb = lax.broadcasted_iota(I32, (tmr, tmr), 0)
    colb = lax.broadcasted_iota(I32, (tmr, tmr), 1)
    same_chunk = rowb // ck == colb // ck
    row4 = lax.broadcasted_iota(I32, (GLA_HEADS * ck, ck), 0) % ck
    col4 = lax.broadcasted_iota(I32, (GLA_HEADS * ck, ck), 1)
    lane = lax.broadcasted_iota(I32, (1, hk), 1)
    head_masks = [(lane // GLA_DK == h).astype(F32) for h in range(GLA_HEADS)]
    sel = (lax.broadcasted_iota(I32, (tmr, n_chunks * GLA_DV), 0) // ck
           == lax.broadcasted_iota(I32, (tmr, n_chunks * GLA_DV), 1) // GLA_DV).astype(F32)

    def run(qk_ref, v_ref, lr_ref, w, bias, s_ref, o_ref, reverse):
        tri = (same_chunk & (rowb <= colb if reverse else rowb >= colb)).astype(F32)
        amask = row4 <= col4 if reverse else row4 >= col4
        la = _log_sigmoid(_mm_split(lr_ref[...], w, parts=2) + bias) * (1.0 / GLA_GATE_NORM)
        b = _mm_split(tri, la, a_exact=True)
        decay_all = jnp.exp(_mm_split(la, sel, dims=(((0,), (0,)), ((), ())), b_exact=True))
        q = qk_ref[:, 0:hk].astype(F32)
        k = qk_ref[:, hk:2 * hk].astype(F32)
        qd_all = q * (GLA_DK ** -0.5) * jnp.exp(b)
        kd_all = k * jnp.exp(-b)
        state = s_ref[...]
        order = range(n_chunks - 1, -1, -1) if reverse else range(n_chunks)
        for ci in order:
            rows = slice(ci * ck, (ci + 1) * ck)
            v = v_ref[rows, :]
            last = ci * ck if reverse else (ci + 1) * ck - 1
            ke = k[rows, :] * jnp.exp(b[last:last + 1, :] - b[rows, :])
            qd = qd_all[rows, :]
            qs = jnp.concatenate([qd * m for m in head_masks], axis=0).astype(BF16)
            att = jnp.where(amask, _mm_nt(qs, kd_all[rows, :]), 0.0)
            inter = _mm(qs, state)
            outs = []
            for h in range(GLA_HEADS):
                hr = slice(h * ck, (h + 1) * ck)
                outs.append(_mm(att[hr, :], v[:, h * GLA_DV:(h + 1) * GLA_DV]) + inter[hr, :])
            o_ref[rows, :] = jnp.concatenate(outs, axis=1)
            ds = _mm_tn(ke, v)
            ds = jnp.concatenate([ds[h * GLA_DK:(h + 1) * GLA_DK, h * GLA_DV:(h + 1) * GLA_DV]
                                  for h in range(GLA_HEADS)], axis=0)
            state = decay_all[:, ci * GLA_DV:(ci + 1) * GLA_DV] * state + ds
        s_ref[...] = state

    run(qkf, vf, lrf, wa_ref[:, 0:hk], ba_ref[:, 0:hk], sf_ref, of_ref, False)
    run(qkb, vb, lrb, wa_ref[:, hk:2 * hk], ba_ref[:, hk:2 * hk], sb_ref, ob_ref, True)


def _gla_scan(qk, v, lr, wa, ba, n_batch, tiles_per_batch, ctx_tiles):
    rows = qk.shape[0]
    tm = ROW_TILE
    tpb = tiles_per_batch

    def fwd(b, t):
        return (b * tpb + t, 0)

    def bwd(b, t):
        return (b * tpb + jnp.where(t < ctx_tiles, ctx_tiles - 1 - t, tpb - 1 - (t - ctx_tiles)), 0)

    wv = v.shape[1]
    return pl.pallas_call(
        _gla_body,
        grid=(n_batch, tpb),
        in_specs=[pl.BlockSpec((tm, qk.shape[1]), fwd), pl.BlockSpec((tm, wv), fwd),
                  pl.BlockSpec((tm, lr.shape[1]), fwd),
                  pl.BlockSpec((tm, qk.shape[1]), bwd), pl.BlockSpec((tm, wv), bwd),
                  pl.BlockSpec((tm, lr.shape[1]), bwd),
                  pl.BlockSpec(wa.shape, lambda b, t: (0, 0)),
                  pl.BlockSpec(ba.shape, lambda b, t: (0, 0))],
        out_specs=[pl.BlockSpec((tm, wv), fwd), pl.BlockSpec((tm, wv), bwd)],
        out_shape=[jax.ShapeDtypeStruct((rows, wv), F32)] * 2,
        scratch_shapes=[pltpu.VMEM((GLA_HEADS * GLA_DK, GLA_DV), F32)] * 2,
        compiler_params=_cparams(("arbitrary", "arbitrary"), 32),
        name="gla_scan",
    )(qk, v, lr, qk, v, lr, wa, ba)


def _s5_prep(a_re, a_im, log_dt, b_re, b_im, c_re, c_im, d):
    hp = lax.Precision.HIGHEST
    tt = S5_CHUNK
    n_groups = a_re.shape[1]
    bg = S5_BLOCK_GROUPS
    nb = n_groups // bg
    a_re, a_im, b_re, b_im = a_re.astype(F32), a_im.astype(F32), b_re.astype(F32), b_im.astype(F32)
    c_re, c_im = c_re.astype(F32), c_im.astype(F32)
    dt = jnp.exp(log_dt.astype(F32))[..., None]
    mag = jnp.exp(dt * a_re)
    ab_re, ab_im = mag * jnp.cos(dt * a_im), mag * jnp.sin(dt * a_im)
    den = a_re * a_re + a_im * a_im
    nr = ab_re - 1.0
    coef_re = (nr * a_re + ab_im * a_im) / den
    coef_im = (ab_im * a_re - nr * a_im) / den
    bb_re = coef_re[..., None] * b_re - coef_im[..., None] * b_im
    bb_im = coef_re[..., None] * b_im + coef_im[..., None] * b_re
    pr, pi = [jnp.ones_like(ab_re)], [jnp.zeros_like(ab_im)]
    for _ in range(tt):
        pr, pi = pr + [pr[-1] * ab_re - pi[-1] * ab_im], pi + [pr[-1] * ab_im + pi[-1] * ab_re]
    pr, pi = jnp.stack(pr), jnp.stack(pi)
    cpr = c_re[None] * pr[:, :, :, None, :] - c_im[None] * pi[:, :, :, None, :]
    cpi = c_re[None] * pi[:, :, :, None, :] + c_im[None] * pr[:, :, :, None, :]
    kern = (jnp.einsum('kdgcn,dgnm->kdgcm', cpr, bb_re, precision=hp)
            - jnp.einsum('kdgcn,dgnm->kdgcm', cpi, bb_im, precision=hp))
    kdim, half = tt * LANES, bg * S5_STATE
    ax_k, ax_s, ax_tc = jnp.arange(kdim), jnp.arange(half), jnp.arange(tt * S5_GROUP)
    k_group, k_step, k_chan = (ax_k // S5_GROUP) % bg, ax_k // LANES, ax_k % S5_GROUP
    rep_tc = ((ax_tc[:, None] // S5_GROUP == k_step[None, :])
              & (ax_tc[:, None] % S5_GROUP == k_chan[None, :])).astype(F32)
    rep_n = (jnp.arange(S5_STATE)[:, None] == ax_s[None, :] % S5_STATE).astype(F32)
    mask_kk = k_group[:, None] == k_group[None, :]
    mask_ks = k_group[:, None] == ax_s[None, :] // S5_STATE
    expand = lambda table, rep: jnp.einsum('jrk,kc->jrc', table, rep, precision=hp)

    s_idx = jnp.arange(tt)[:, None]
    t_idx = jnp.arange(tt)[None, :]
    lag_f = jnp.clip(t_idx - s_idx, 0, tt)
    lag_b = jnp.clip(s_idx - t_idx, 0, tt)
    toe = (jnp.where((t_idx >= s_idx)[:, :, None, None, None], kern[lag_f, 0], 0.0)
           + jnp.where((s_idx >= t_idx)[:, :, None, None, None], kern[lag_b, 1], 0.0))
    skip = (jnp.eye(tt, dtype=F32)[:, :, None, None, None]
            * d.astype(F32).reshape(n_groups, S5_GROUP)[None, None, :, :, None]
            * jnp.eye(S5_GROUP, dtype=F32)[None, None, None])
    toe = (toe + skip).reshape(tt, tt, nb, bg, S5_GROUP, S5_GROUP)
    toe = toe.transpose(2, 0, 3, 5, 1, 4).reshape(nb, kdim, tt * S5_GROUP)
    m_mat = jnp.where(mask_kk, expand(toe, rep_tc), 0.0)

    def inject(direction, power_of_s):
        p_re, p_im = pr[power_of_s, direction], pi[power_of_s, direction]
        v_re = p_re[..., None] * bb_re[direction][None] - p_im[..., None] * bb_im[direction][None]
        v_im = p_re[..., None] * bb_im[direction][None] + p_im[..., None] * bb_re[direction][None]
        def blk(v):
            v = v.reshape(tt, nb, bg, S5_STATE, S5_GROUP).transpose(1, 0, 2, 4, 3).reshape(nb, kdim, S5_STATE)
            return jnp.where(mask_ks, expand(v, rep_n), 0.0)
        return jnp.concatenate([blk(v_re), blk(v_im)], axis=-1)

    def readout(direction, power_of_t):
        e_re, e_im = cpr[power_of_t, direction], -cpi[power_of_t, direction]
        def blk(v):
            v = v.reshape(tt, nb, bg, S5_GROUP, S5_STATE).transpose(1, 2, 4, 0, 3).reshape(nb, half, tt * S5_GROUP)
            return jnp.where(mask_ks.T, expand(v, rep_tc), 0.0)
        return jnp.concatenate([blk(e_re), blk(e_im)], axis=1)

    steps = jnp.arange(tt)
    g_f = inject(0, tt - 1 - steps)
    g_b = inject(1, steps)
    e_f = readout(0, steps + 1)
    e_b = readout(1, tt - steps)
    a_t = jnp.stack([pr[tt, 0].reshape(nb, half), pi[tt, 0].reshape(nb, half),
                     pr[tt, 1].reshape(nb, half), pi[tt, 1].reshape(nb, half)], axis=1)
    cast = lambda z: z.astype(BF16)
    return cast(m_mat), cast(g_f), cast(g_b), cast(e_f), cast(e_b), a_t


def _s5_body(u_ref, m_ref, gf_ref, gb_ref, ef_ref, eb_ref, a_ref, y_ref, sf_ref, sb_ref,
             *, ctx_rows, n_rows, n_batch):
    half = S5_BLOCK_GROUPS * S5_STATE
    row_blocks = [slice(r * n_rows, (r + 1) * n_rows) for r in range(n_batch)]

    for rows in row_blocks:
        sf_ref[rows, :] = jnp.dot(u_ref[rows, :], gf_ref[0], preferred_element_type=F32)
        sb_ref[rows, :] = jnp.dot(u_ref[rows, :], gb_ref[0], preferred_element_type=F32)

    a = a_ref[0]
    afr, afi, abr, abi = a[0:1, :], a[1:2, :], a[2:3, :], a[3:4, :]

    sub = 8
    cpt = sub // n_batch
    n_tiles, ctx_tiles = n_rows // cpt, ctx_rows // cpt

    def advance(a_re, a_im, s_re, s_im, g):
        return a_re * s_re - a_im * s_im + g[:, 0:half], a_re * s_im + a_im * s_re + g[:, half:2 * half]

    def step(k, carry):
        fr, fi, br, bi = carry
        rf = pl.ds(pl.multiple_of(k * sub, sub), sub)
        g = sf_ref[rf, :]
        ins_r, ins_i = [], []
        for c in range(cpt):
            ins_r.append(fr)
            ins_i.append(fi)
            fr, fi = advance(afr, afi, fr, fi, g[c * n_batch:(c + 1) * n_batch, :])
        sf_ref[rf, 0:half] = jnp.concatenate(ins_r, axis=0)
        sf_ref[rf, half:2 * half] = jnp.concatenate(ins_i, axis=0)
        kb = jnp.where(k < ctx_tiles, ctx_tiles - 1 - k, n_tiles - 1 - (k - ctx_tiles))
        rb = pl.ds(pl.multiple_of(kb * sub, sub), sub)
        g = sb_ref[rb, :]
        ins_r, ins_i = [None] * cpt, [None] * cpt
        for c in range(cpt - 1, -1, -1):
            ins_r[c], ins_i[c] = br, bi
            br, bi = advance(abr, abi, br, bi, g[c * n_batch:(c + 1) * n_batch, :])
        sb_ref[rb, 0:half] = jnp.concatenate(ins_r, axis=0)
        sb_ref[rb, half:2 * half] = jnp.concatenate(ins_i, axis=0)
        return fr, fi, br, bi

    zero = jnp.zeros((n_batch, half), F32)
    lax.fori_loop(0, n_tiles, step, (zero, zero, zero, zero))

    for rows in row_blocks:
        y = (jnp.dot(u_ref[rows, :], m_ref[0], preferred_element_type=F32)
             + jnp.dot(sf_ref[rows, :].astype(BF16), ef_ref[0], preferred_element_type=F32)
             + jnp.dot(sb_ref[rows, :].astype(BF16), eb_ref[0], preferred_element_type=F32))
        y_ref[rows, :] = y.astype(y_ref.dtype)


def _s5_scan(u, mats, n_batch, ctx_len, layer):
    m_mat, g_f, g_b, e_f, e_b, a_t = mats
    rows, width = u.shape
    tt = S5_CHUNK
    nb = width // LANES
    srows = rows // tt // n_batch
    kdim = tt * LANES
    sdim = 2 * S5_BLOCK_GROUPS * S5_STATE
    u5 = u.reshape(n_batch, srows, tt, nb, LANES).transpose(3, 1, 0, 2, 4).reshape(nb, srows * n_batch, kdim)
    once = dict(pipeline_mode=pl.Buffered(1))
    wspec = lambda shape: pl.BlockSpec((None, 1) + shape, lambda j: (layer, j, 0, 0), **once)
    y5 = pl.pallas_call(
        functools.partial(_s5_body, ctx_rows=ctx_len // tt, n_rows=srows, n_batch=n_batch),
        grid=(nb,),
        in_specs=[pl.BlockSpec((None, srows * n_batch, kdim), lambda j: (j, 0, 0), **once),
                  wspec((kdim, kdim)), wspec((kdim, sdim)), wspec((kdim, sdim)),
                  wspec((sdim, kdim)), wspec((sdim, kdim)), wspec((4, sdim // 2))],
        out_specs=pl.BlockSpec((None, srows * n_batch, kdim), lambda j: (j, 0, 0)),
        out_shape=jax.ShapeDtypeStruct((nb, srows * n_batch, kdim), BF16),
        scratch_shapes=[pltpu.VMEM((srows * n_batch, sdim), F32), pltpu.VMEM((srows * n_batch, sdim), F32)],
        compiler_params=_cparams(("arbitrary",), 56),
        name="s5_scan",
    )(u5, m_mat, g_f, g_b, e_f, e_b, a_t)
    return y5.reshape(nb, srows, n_batch, tt, LANES).transpose(2, 1, 3, 0, 4).reshape(rows, width)


def _rope_tables(seq_len):
    pos = jnp.arange(seq_len)
    row = (pos // GRID_W).astype(F32)
    col = (pos % GRID_W).astype(F32)
    inv_freq = ROPE_BASE ** (-jnp.arange(ROPE_PAIRS, dtype=F32) / ROPE_PAIRS)
    ar, ac = row[:, None] * inv_freq, col[:, None] * inv_freq
    cos = jnp.concatenate([jnp.cos(ar), jnp.cos(ar), jnp.cos(ac), jnp.cos(ac)], axis=1)
    sin = jnp.concatenate([-jnp.sin(ar), jnp.sin(ar), -jnp.sin(ac), jnp.sin(ac)], axis=1)
    return cos, sin


def _head_mats():
    hd = SWA_HEAD_DIM
    src = lax.broadcasted_iota(I32, (hd, hd), 0)
    dst = lax.broadcasted_iota(I32, (hd, hd), 1)
    partner = jnp.where(dst % (2 * ROPE_PAIRS) < ROPE_PAIRS, dst + ROPE_PAIRS, dst - ROPE_PAIRS)
    return jnp.ones((hd, hd), BF16), (src == partner).astype(BF16)


def _head_rms_rope(x, gain, cos, sin, ones_hd, swap_halves):
    ss = _mm(x * x, ones_hd)
    xn = x * lax.rsqrt(ss * (1.0 / SWA_HEAD_DIM) + EPS) * gain
    return xn * cos + _mm(xn, swap_halves) * sin


def _swa_body(sink_ref, q_ref, kp_ref, kc_ref, kn_ref, kx_ref, o_ref, *, lat_blocks):
    n = pl.program_id(1)
    hd = SWA_HEAD_DIM
    blk = SWA_BLOCK
    kvw = SWA_KV_HEADS * hd
    row = lax.broadcasted_iota(I32, (SWA_GROUP * blk, blk), 0) % blk
    col = lax.broadcasted_iota(I32, (SWA_GROUP * blk, blk), 1)
    ones_hd = jnp.ones((hd, hd), BF16)

    def lane_fold(op, blocks):
        parts = [b[:, c0:c0 + blk] for b in blocks for c0 in range(0, b.shape[1], blk)]
        out = parts[0]
        for part in parts[1:]:
            out = op(out, part)
        return out

    def attend(is_lat):
        for g in range(SWA_KV_HEADS):
            kcols = slice(g * hd, (g + 1) * hd)
            vcols = slice(kvw + g * hd, kvw + (g + 1) * hd)
            keys = [kx_ref[:, kcols]]
            values = [kx_ref[:, vcols]]
            valid = [None]
            if is_lat:
                wins = (kp_ref, kc_ref, kn_ref)
                keys += [r[:, kcols] for r in wins]
                values += [r[:, vcols] for r in wins]
                valid += [(col >= row) & (col + (n - 1) * blk >= 0), None,
                          (col <= row) & (col + (n + 1) * blk < lat_blocks * blk)]
            heads = range(g * SWA_GROUP, (g + 1) * SWA_GROUP)
            q4 = jnp.concatenate([q_ref[:, h * hd:(h + 1) * hd] for h in heads], axis=0)
            sink = jnp.concatenate([jnp.full((blk, 1), sink_ref[h] * LOG2E, F32) for h in heads], axis=0)
            scores = []
            for kk, ok in zip(keys, valid):
                s = _mm_nt(q4, kk)
                scores.append(s if ok is None else jnp.where(ok, s, NEG_INF))
            m = jnp.maximum(jnp.max(lane_fold(jnp.maximum, scores), axis=-1, keepdims=True), sink)
            probs = [jnp.exp2(s - m).astype(BF16) for s in scores]
            den = jnp.exp2(sink - m)
            for p in probs:
                for c0 in range(0, p.shape[1], blk):
                    den = den + _mm(p[:, c0:c0 + blk], ones_hd)
            acc = None
            for p, vv in zip(probs, values):
                pv = _mm(p, vv)
                acc = pv if acc is None else acc + pv
            out = acc * (1.0 / den)
            for hh, h in enumerate(heads):
                o_ref[:, h * hd:(h + 1) * hd] = out[hh * blk:(hh + 1) * blk, :].astype(o_ref.dtype)

    @pl.when(n < lat_blocks)
    def _lat():
        attend(True)

    @pl.when(n >= lat_blocks)
    def _ctx():
        attend(False)


def _swa_attend(q, kv, sink, n_batch, seq_len, ctx_len):
    rows = q.shape[0]
    blk = SWA_BLOCK
    lat_blocks = seq_len // blk
    ctx_blocks = ctx_len // blk
    bpb = lat_blocks + ctx_blocks

    def q_map(b, n, s):
        return (b * bpb + jnp.where(n < lat_blocks, ctx_blocks + n, n - lat_blocks), 0)

    def win(off):
        return lambda b, n, s: (b * bpb + ctx_blocks + jnp.clip(n + off, 0, lat_blocks - 1), 0)

    kvs = lambda m: pl.BlockSpec((blk, kv.shape[1]), m)
    grid_spec = pltpu.PrefetchScalarGridSpec(
        num_scalar_prefetch=1,
        grid=(n_batch, bpb),
        in_specs=[pl.BlockSpec((blk, q.shape[1]), q_map),
                  kvs(win(-1)), kvs(win(0)), kvs(win(1)),
                  pl.BlockSpec((ctx_len, kv.shape[1]), lambda b, n, s: (b * (bpb * blk // ctx_len), 0))],
        out_specs=pl.BlockSpec((blk, q.shape[1]), q_map),
    )
    return pl.pallas_call(
        functools.partial(_swa_body, lat_blocks=lat_blocks),
        grid_spec=grid_spec,
        out_shape=jax.ShapeDtypeStruct((rows, q.shape[1]), BF16),
        compiler_params=_cparams(("arbitrary", "arbitrary"), 32),
        name="swa_attend",
    )(sink.astype(F32), q, kv, kv, kv, kv)


def _route_rows(logits_t, bias_col):
    aff = _sigmoid(logits_t)
    biased = aff + bias_col
    v = [biased[e:e + 1, :] for e in range(N_EXPERTS)]
    a = [aff[e:e + 1, :] for e in range(N_EXPERTS)]
    gsz = EXPERTS_PER_GROUP
    best_g = best_s = None
    for g in range(N_EXPERT_GROUPS):
        vg = v[g * gsz:(g + 1) * gsz]
        score = None
        for i in range(gsz):
            for k in range(i + 1, gsz):
                pair = vg[i] + vg[k]
                score = pair if score is None else jnp.maximum(score, pair)
        if best_g is None:
            best_g, best_s = jnp.zeros_like(score, dtype=I32), score
        else:
            upd = score > best_s
            best_g = jnp.where(upd, g, best_g)
            best_s = jnp.where(upd, score, best_s)

    def pick(rows):
        out = []
        for i in range(gsz):
            x = rows[i]
            for g in range(1, N_EXPERT_GROUPS):
                x = jnp.where(best_g == g, rows[g * gsz + i], x)
            out.append(x)
        return out

    vb, ab = pick(v), pick(a)
    i1, v1, a1 = jnp.zeros_like(best_g), vb[0], ab[0]
    for i in range(1, gsz):
        upd = vb[i] > v1
        i1, v1, a1 = jnp.where(upd, i, i1), jnp.where(upd, vb[i], v1), jnp.where(upd, ab[i], a1)
    i2 = v2 = a2 = None
    for i in range(gsz):
        cand = jnp.where(i1 == i, -jnp.inf, vb[i])
        if i2 is None:
            i2, v2, a2 = jnp.zeros_like(best_g), cand, ab[0]
        else:
            upd = cand > v2
            i2, v2, a2 = jnp.where(upd, i, i2), jnp.where(upd, cand, v2), jnp.where(upd, ab[i], a2)
    tot = a1 + a2
    e1 = (best_g * gsz + i1).astype(F32)
    e2 = (best_g * gsz + i2).astype(F32)
    zeros = jnp.zeros((4, e1.shape[1]), F32)
    return jnp.concatenate([e1, e2, a1 / tot, a2 / tot, zeros], axis=0)


def _out_body(of_ref, ob_ref, r_ref, ys_ref, sw_ref, *refs, two_sources, ctx_tiles, tiles_per_batch):
    refs = list(refs)
    x_ref = refs.pop(0)
    ctx_ref = refs.pop(0) if two_sources else None
    (g1_ref, sh_ref, sc_ref, n2_ref, og_ref, wglu_ref, bglu_ref, wo_ref, rw_ref, rb_ref, cast_in,
     xo_ref, h_ref, rt_ref, cast_out) = refs
    _cast_blocks([cast_in], [cast_out])
    x_res = x_ref[...]
    if two_sources:
        x_res = jnp.where(pl.program_id(0) % tiles_per_batch < ctx_tiles, ctx_ref[...], x_res)
    o = of_ref[...] + ob_ref[...]
    r = r_ref[...].astype(F32)
    gain = og_ref[...]
    heads = []
    for h in range(GLA_HEADS):
        cols = slice(h * GLA_DV, (h + 1) * GLA_DV)
        heads.append(_rms(o[:, cols], gain[:, cols]))
    gla_y = jnp.concatenate(heads, axis=1) * _silu(r)
    z = _gelu_tanh(ys_ref[...].astype(F32))
    s5_y = z * _sigmoid(_mm(z, wglu_ref[...]) + bglu_ref[...])
    mixed = jnp.concatenate([gla_y.astype(BF16), s5_y.astype(BF16), sw_ref[...]], axis=1)
    y = jnp.dot(mixed, wo_ref[...], preferred_element_type=F32)
    x_new = x_res + g1_ref[0] * y
    xo_ref[...] = x_new
    h2 = _rms(x_new, n2_ref[...]) * (1.0 + sc_ref[0]) + sh_ref[0]
    h_ref[...] = h2
    logits_t = _mm_split(rw_ref[...], h2, dims=(((1,), (1,)), ((), ())), parts=2)
    rt_ref[0] = _route_rows(logits_t, rb_ref[...])


def _out_proj(o_f, o_b, r, ys, sw, x_src, mod3, norm2, o_gain, w_glu, b_glu, w_out, rw_t, rb_col,
              tiles_per_batch, ctx_tiles, n_batch, layer, cast):
    rows, d = o_f.shape[0], w_out.shape[2]
    tm = ROW_TILE
    nt = rows // tm
    mrow = functools.partial(_mod_row, tiles_per_batch=tiles_per_batch, ctx_tiles=ctx_tiles,
                             n_batch=n_batch, layer=layer)
    x_args, x_specs = _row_sources(x_src, tiles_per_batch, ctx_tiles)
    rowspec = lambda a: pl.BlockSpec((tm, a.shape[1]), lambda i: (i, 0))
    const = lambda a: pl.BlockSpec(a.shape, lambda i: (0,) * a.ndim)
    layered = lambda a, **kw: pl.BlockSpec((None,) + a.shape[1:], lambda i: (layer,) + (0,) * (a.ndim - 1), **kw)
    modspec = lambda k: pl.BlockSpec((1, 1, d), lambda i: (mrow(i) * 6 + k, 0, 0))
    consts = [norm2.reshape(1, d), o_gain.reshape(1, -1)]
    cast_in, cast_out, cast_shapes = _cast_specs([cast], layer, nt)
    return pl.pallas_call(
        functools.partial(_out_body, two_sources=len(x_args) == 2, ctx_tiles=ctx_tiles,
                          tiles_per_batch=tiles_per_batch),
        grid=(nt,),
        in_specs=[rowspec(o_f), rowspec(o_b), rowspec(r), rowspec(ys), rowspec(sw)] + x_specs
                 + [modspec(2), modspec(3), modspec(4)]
                 + [const(a) for a in consts]
                 + [layered(w_glu), const(b_glu.reshape(1, -1)),
                    layered(w_out, pipeline_mode=pl.Buffered(1)), const(rw_t), const(rb_col)] + cast_in,
        out_specs=[pl.BlockSpec((tm, d), lambda i: (i, 0)), pl.BlockSpec((tm, d), lambda i: (i, 0)),
                   pl.BlockSpec((1, 8, tm), lambda i: (i, 0, 0))] + cast_out,
        out_shape=[jax.ShapeDtypeStruct((rows, d), F32), jax.ShapeDtypeStruct((rows, d), F32),
                   jax.ShapeDtypeStruct((nt, 8, tm), F32)] + cast_shapes,
        compiler_params=_cparams(("arbitrary",), 48),
        name="out_proj",
    )(o_f, o_b, r, ys, sw, *x_args, mod3, mod3, mod3, *consts, w_glu, b_glu.reshape(1, -1), w_out, rw_t, rb_col,
      cast)


def _row_copy(src_hbm, row, dst, slot, sem):
    return pltpu.make_async_copy(src_hbm.at[pl.ds(row, 1)], dst.at[pl.ds(slot, 1)], sem)


def _gather_start(src_hbm, idx_ref, dst, sem, n_rows, unrolled=False):
    if unrolled:
        for r in range(n_rows):
            _row_copy(src_hbm, idx_ref[0, 0, r], dst, r, sem).start()
        return

    def body(g, carry):
        for u in range(DMA_UNROLL):
            r = g * DMA_UNROLL + u
            _row_copy(src_hbm, idx_ref[0, 0, r], dst, r, sem).start()
        return carry
    lax.fori_loop(0, n_rows // DMA_UNROLL, body, 0)


def _gather_wait(src_hbm, dst, sem, n_rows):
    for r in range(n_rows):
        _row_copy(src_hbm, 0, dst, r, sem).wait()


def _ring_gather(step, n_steps, src_hbm, idx_refs, ring, sem, work):
    depth = GATHER_RING
    n_rows = ring.shape[1]

    @pl.when(step == 0)
    def _prime():
        for k in range(depth - 1):
            _gather_start(src_hbm, idx_refs[k], ring.at[k], sem.at[k], n_rows)

    slot = step % depth
    ahead = (step + depth - 1) % depth
    _gather_wait(src_hbm, ring.at[slot], sem.at[slot], n_rows)
    work(ring.at[slot],
         lambda unrolled: _gather_start(src_hbm, idx_refs[depth - 1], ring.at[ahead], sem.at[ahead], n_rows,
                                        unrolled))

    @pl.when(step == n_steps - 1)
    def _drain():
        for k in range(1, depth):
            late = (step + k) % depth
            _gather_wait(src_hbm, ring.at[late], sem.at[late], n_rows)


PLAN_EXPERT, PLAN_VALID, PLAN_FIRST, PLAN_SLOT, PLAN_NEXT, PLAN_HAS_NEXT = range(6)


def _moe_body(plan_ref, idx0_ref, idx1_ref, idx2_ref, x_hbm, wg_hbm, wu_hbm, wd_hbm, o_ref,
              ring, sem, wg_buf, wu_buf, wd_buf, wsem, *, n_steps):
    i = pl.program_id(0)
    expert = plan_ref[PLAN_EXPERT, i]
    wslot = plan_ref[PLAN_SLOT, i]

    def weight_copies(e, slot):
        pairs = ((wg_hbm, wg_buf), (wu_hbm, wu_buf), (wd_hbm, wd_buf))
        return [pltpu.make_async_copy(src.at[e], dst.at[slot], wsem.at[slot]) for src, dst in pairs]

    @pl.when(i == 0)
    def _first_weights():
        for c in weight_copies(expert, 0):
            c.start()

    @pl.when(plan_ref[PLAN_FIRST, i] > 0)
    def _switch():
        for c in weight_copies(expert, wslot):
            c.wait()

        @pl.when(plan_ref[PLAN_HAS_NEXT, i] > 0)
        def _prefetch():
            for c in weight_copies(plan_ref[PLAN_NEXT, i], 1 - wslot):
                c.start()

    def work(cur, start_next):
        @pl.when(plan_ref[PLAN_VALID, i] > 0)
        def _active():
            start_next(True)
            x = cur[...].astype(BF16)
            gate = jnp.dot(x, wg_buf[wslot], preferred_element_type=F32)
            up = jnp.dot(x, wu_buf[wslot], preferred_element_type=F32)
            act = (_silu(gate) * up).astype(BF16)
            o_ref[...] = jnp.dot(act, wd_buf[wslot], preferred_element_type=F32)

        @pl.when(plan_ref[PLAN_VALID, i] == 0)
        def _idle():
            start_next(False)
            o_ref[...] = jnp.zeros_like(o_ref)

    _ring_gather(i, n_steps, x_hbm, (idx0_ref, idx1_ref, idx2_ref), ring, sem, work)


def _moe_experts(plan, slot_token, tokens, w_gate, w_up, w_down):
    tm = MOE_TILE
    nt = plan.shape[1]
    d = tokens.shape[1]
    de = w_gate.shape[2]
    idx = lambda off: pl.BlockSpec((1, 1, tm), lambda i, plan: (jnp.minimum(i + off, nt - 1), 0, 0),
                                   memory_space=pltpu.SMEM)
    hbm = pl.BlockSpec(memory_space=pl.ANY)
    grid_spec = pltpu.PrefetchScalarGridSpec(
        num_scalar_prefetch=1,
        grid=(nt,),
        in_specs=[idx(k) for k in range(GATHER_RING)] + [hbm, hbm, hbm, hbm],
        out_specs=pl.BlockSpec((tm, d), lambda i, plan: (i, 0)),
        scratch_shapes=[pltpu.VMEM((GATHER_RING, tm, d), F32), pltpu.SemaphoreType.DMA((GATHER_RING,)),
                        pltpu.VMEM((2, d, de), BF16), pltpu.VMEM((2, d, de), BF16), pltpu.VMEM((2, de, d), BF16),
                        pltpu.SemaphoreType.DMA((2,))],
    )
    slots = slot_token.reshape(nt, 1, tm)
    return pl.pallas_call(
        functools.partial(_moe_body, n_steps=nt),
        grid_spec=grid_spec,
        out_shape=jax.ShapeDtypeStruct((nt * tm, d), F32),
        compiler_params=_cparams(("arbitrary",), 56),
        name="moe_experts",
    )(plan, *([slots] * GATHER_RING), tokens, w_gate, w_up, w_down)


def _combine_body(pos0_ref, pos1_ref, pos2_ref, x_ref, g2_ref, w_ref, y_hbm, o_ref, ring, sem, *, n_steps):
    tm = x_ref.shape[0]

    def work(cur, start_next):
        start_next(True)
        w = w_ref[...]
        f = w[:, 0:1] * cur[0:tm, :] + w[:, 1:2] * cur[tm:2 * tm, :]
        o_ref[...] = x_ref[...] + g2_ref[0] * f

    _ring_gather(pl.program_id(0), n_steps, y_hbm, (pos0_ref, pos1_ref, pos2_ref), ring, sem, work)


def _moe_combine(pos, x_new, mod3, wts, y_sorted, n_batch, tiles_per_batch, ctx_tiles, keep_ctx, layer):
    rows, d = x_new.shape
    tm = ROW_TILE
    tpb = tiles_per_batch
    first = 0 if keep_ctx else ctx_tiles
    out_tpb = tpb - first
    ns = n_batch * out_tpb
    in_tile = lambda s: (s // out_tpb) * tpb + first + s % out_tpb
    mrow = lambda s: layer * MOD_ROWS + jnp.where(first + s % out_tpb < ctx_tiles, n_batch, s // out_tpb)
    pos3 = pos.reshape(rows // tm, tm, 2).transpose(0, 2, 1).reshape(rows // tm, 1, 2 * tm)
    idx = lambda off: pl.BlockSpec((1, 1, 2 * tm), lambda s: (in_tile(jnp.minimum(s + off, ns - 1)), 0, 0),
                                   memory_space=pltpu.SMEM)
    return pl.pallas_call(
        functools.partial(_combine_body, n_steps=ns),
        grid=(ns,),
        in_specs=[idx(k) for k in range(GATHER_RING)] + [
                  pl.BlockSpec((tm, d), lambda s: (in_tile(s), 0)),
                  pl.BlockSpec((1, 1, d), lambda s: (mrow(s) * 6 + 5, 0, 0)),
                  pl.BlockSpec((tm, 2), lambda s: (in_tile(s), 0)),
                  pl.BlockSpec(memory_space=pl.ANY)],
        out_specs=pl.BlockSpec((tm, d), lambda s: (s, 0)),
        out_shape=jax.ShapeDtypeStruct((ns * tm, d), F32),
        scratch_shapes=[pltpu.VMEM((GATHER_RING, 2 * tm, d), F32), pltpu.SemaphoreType.DMA((GATHER_RING,))],
        compiler_params=_cparams(("arbitrary",), 40),
        name="moe_combine",
    )(*([pos3] * GATHER_RING), x_new, mod3, wts, y_sorted)


def _moe_plan(route, n_rows):
    tm = MOE_TILE
    e = route[:, 0:2, :].astype(I32).transpose(0, 2, 1).reshape(n_rows, 2)
    wts = route[:, 2:4, :].transpose(0, 2, 1).reshape(n_rows, 2)
    e_flat = e.reshape(-1)
    onehot = (e_flat[:, None] == jnp.arange(N_EXPERTS, dtype=I32)[None, :]).astype(I32)
    csum = jnp.cumsum(onehot, axis=0)
    rank = jnp.sum(onehot * csum, axis=1) - 1
    counts = csum[-1]
    tiles_e = (counts + tm - 1) // tm
    tile_end = jnp.cumsum(tiles_e)
    tile_start = tile_end - tiles_e
    pos = tile_start[e_flat] * tm + rank
    nt = (2 * n_rows) // tm + N_EXPERTS
    slot_token = jnp.zeros((nt * tm,), I32).at[pos].set(jnp.arange(2 * n_rows, dtype=I32) // 2,
                                                        unique_indices=True)
    n_active = tile_end[-1]
    tile_id = jnp.minimum(jnp.arange(nt, dtype=I32), n_active - 1)
    tile_expert = jnp.minimum(jnp.sum((tile_id[:, None] >= tile_end[None, :]).astype(I32), axis=1),
                              N_EXPERTS - 1).astype(I32)
    tile_ids = jnp.arange(nt, dtype=I32)
    tile_valid = (tile_ids < n_active).astype(I32)
    first = jnp.concatenate([jnp.ones((1,), I32), (tile_expert[1:] != tile_expert[:-1]).astype(I32)])
    wslot = (jnp.cumsum(first) - 1) % 2
    later_first = (tile_ids[None, :] > tile_ids[:, None]) & (first[None, :] > 0)
    next_first = jnp.min(jnp.where(later_first, tile_ids[None, :], nt), axis=1)
    has_next = (next_first < nt).astype(I32)
    next_expert = tile_expert[jnp.minimum(next_first, nt - 1)]
    plan = jnp.stack([tile_expert, tile_valid, first, wslot.astype(I32), next_expert, has_next]).astype(I32)
    return plan, slot_token, pos.reshape(n_rows, 2), wts


def kernel(x, c, ctx, c_ctx, ada_w, ada_b, norm1, norm2, w_in, gla_w_alpha, gla_b_alpha, gla_o_norm,
           s5_a_re, s5_a_im, s5_log_dt, s5_b_re, s5_b_im, s5_c_re, s5_c_im, s5_d, s5_w_glu, s5_b_glu,
           swa_q_norm, swa_k_norm, swa_sink, w_out, router_w, router_b, exp_w_gate, exp_w_up, exp_w_down):
    n_batch, seq_len, d = x.shape
    ctx_len = ctx.shape[1]
    depth = ada_w.shape[0]
    tm = ROW_TILE
    per_batch = ctx_len + seq_len
    assert ctx_len % tm == 0 and seq_len % tm == 0
    tpb, ctx_tiles = per_batch // tm, ctx_len // tm
    rows = n_batch * per_batch

    hk = GLA_HEADS * GLA_DK
    gla_w = GLA_HEADS * GLA_DV
    s5_w = s5_d.shape[1]
    swa_w = SWA_HEADS * SWA_HEAD_DIM
    kv_w = SWA_KV_HEADS * SWA_HEAD_DIM
    o_q, o_k, o_v = 0, hk, 2 * hk
    o_lr = o_v + gla_w
    o_r = o_lr + 2 * GLA_RANK
    o_u = o_r + gla_w
    o_sq = o_u + s5_w
    o_sk = o_sq + swa_w
    outs = [(2 * hk, BF16), (gla_w, BF16), (gla_w, BF16), (s5_w, BF16), (swa_w, BF16), (2 * kv_w, BF16),
            (LANES, F32)]

    x_src = (x.reshape(n_batch * seq_len, d), ctx.reshape(n_batch * ctx_len, d))
    cvec = jnp.concatenate([c, c_ctx[None], jnp.zeros((8 - n_batch - 1, d), c.dtype)], axis=0)
    cos, sin = _rope_tables(seq_len)
    cos_tab = jnp.concatenate([jnp.ones((ctx_len, SWA_HEAD_DIM), F32), cos], axis=0)
    sin_tab = jnp.concatenate([jnp.zeros((ctx_len, SWA_HEAD_DIM), F32), sin], axis=0)
    heads = {4: (SWA_HEADS, 0, SWA_HEAD_DIM ** -0.5 * LOG2E), 5: (SWA_KV_HEADS, 1, 1.0)}
    rw_t = jnp.zeros((LANES, d), F32).at[:N_EXPERTS].set(router_w.astype(F32).T)
    rb_col = jnp.zeros((LANES, 1), F32).at[:N_EXPERTS, 0].set(router_b.astype(F32))

    mod3 = _ada_mod(cvec, ada_w, ada_b).reshape(depth * MOD_ROWS * 6, 1, d)
    w_prep = jnp.concatenate(
        [w_in[:, :, o_q:o_v], w_in[:, :, o_v:o_lr], w_in[:, :, o_r:o_u], w_in[:, :, o_u:o_sq],
         w_in[:, :, o_sq:o_sk], w_in[:, :, o_sk:], w_in[:, :, o_lr:o_r],
         jnp.zeros((depth, d, LANES - 2 * GLA_RANK), w_in.dtype)], axis=2).astype(BF16)
    w_glu16, w_out16 = s5_w_glu.astype(BF16), w_out.astype(BF16)
    n_exp, _, d_exp = exp_w_gate.shape[1:]
    wg_blk = exp_w_gate.reshape(depth, CAST_BLOCKS, n_exp * d // CAST_BLOCKS, d_exp)
    wu_blk = exp_w_up.reshape(depth, CAST_BLOCKS, n_exp * d // CAST_BLOCKS, d_exp)
    wd_blk = exp_w_down.reshape(depth, CAST_BLOCKS, n_exp * d_exp // CAST_BLOCKS, d)
    s5_mats = jax.vmap(_s5_prep)(s5_a_re, s5_a_im, s5_log_dt, s5_b_re, s5_b_im, s5_c_re, s5_c_im, s5_d)

    out = None
    for layer in range(depth):
        last = layer == depth - 1
        (qk, gv, gr, u, sq, skv, lr), (wg16, wu16) = _in_proj(
            x_src, rows, mod3, norm1[layer], w_prep, outs, heads, swa_q_norm[layer], swa_k_norm[layer],
            cos_tab, sin_tab, tpb, ctx_tiles, n_batch, layer, [wg_blk, wu_blk])

        wa = jnp.zeros((LANES, 2 * hk), F32)
        wa = wa.at[0:GLA_RANK, 0:hk].set(gla_w_alpha[layer, 0].astype(F32))
        wa = wa.at[GLA_RANK:2 * GLA_RANK, hk:].set(gla_w_alpha[layer, 1].astype(F32))
        ba = gla_b_alpha[layer].astype(F32).reshape(1, 2 * hk)
        o_f, o_b = _gla_scan(qk, gv, lr, wa, ba, n_batch, tpb, ctx_tiles)

        ys = _s5_scan(u, s5_mats, n_batch, ctx_len, layer)

        sw = _swa_attend(sq, skv, swa_sink[layer], n_batch, seq_len, ctx_len)

        x_new, h2, route, wd16 = _out_proj(o_f, o_b, gr, ys, sw, x_src, mod3, norm2[layer], gla_o_norm[layer],
                                           w_glu16, s5_b_glu[layer].astype(F32), w_out16, rw_t, rb_col,
                                           tpb, ctx_tiles, n_batch, layer, wd_blk)

        plan, slot_token, pos, wts = _moe_plan(route, rows)
        y_sorted = _moe_experts(plan, slot_token, h2,
                                wg16.reshape(n_exp, d, d_exp), wu16.reshape(n_exp, d, d_exp),
                                wd16.reshape(n_exp, d_exp, d))
        out = _moe_combine(pos, x_new, mod3, wts, y_sorted, n_batch, tpb, ctx_tiles, keep_ctx=not last,
                           layer=layer)
        x_src = out
    return out.reshape(n_batch, seq_len, d)
```

```python
import functools
import math

import jax
import jax.numpy as jnp
from jax import lax
from jax.experimental import pallas as pl
from jax.experimental.pallas import tpu as pltpu

F32 = jnp.float32
BF16 = jnp.bfloat16
I32 = jnp.int32

EPS = 1e-6
NEG_INF = -1e30
LOG2E = math.log2(math.e)

LANES = 128
VMEM_BYTES = 64 * 1024 * 1024

ROW_TILE = 256
MOD_ROWS = 8

GLA_HEADS = 4
GLA_DK = 64
GLA_DV = 128
GLA_RANK = 16
GLA_GATE_NORM = 16.0
GLA_CHUNK = 64

S5_GROUP = 16
S5_STATE = 64
S5_CHUNK = 8
S5_BLOCK_GROUPS = LANES // S5_GROUP

SWA_HEAD_DIM = 128
SWA_HEADS = 8
SWA_KV_HEADS = 2
SWA_GROUP = SWA_HEADS // SWA_KV_HEADS
SWA_BLOCK = 128
ROPE_PAIRS = 32
ROPE_BASE = 10000.0
GRID_W = 64

N_EXPERTS = 16
N_EXPERT_GROUPS = 4
EXPERTS_PER_GROUP = 4
MOE_TILE = 256
DMA_UNROLL = 8
GATHER_RING = 3
CAST_BLOCKS = 64


def _cparams(semantics, vmem_mb):
    return pltpu.CompilerParams(dimension_semantics=semantics,
                                vmem_limit_bytes=vmem_mb * 1024 * 1024)


def _mm(a, b):
    return jnp.dot(a.astype(BF16), b.astype(BF16), preferred_element_type=F32)


def _mm_nt(a, b):
    return lax.dot_general(a.astype(BF16), b.astype(BF16), (((1,), (1,)), ((), ())),
                           preferred_element_type=F32)


def _mm_tn(a, b):
    return lax.dot_general(a.astype(BF16), b.astype(BF16), (((0,), (0,)), ((), ())),
                           preferred_element_type=F32)


def _split(a, parts):
    out, rest = [], a
    for p in range(parts):
        piece = rest.astype(BF16)
        out.append(piece)
        if p + 1 < parts:
            rest = rest - piece.astype(F32)
    return out


def _mm_split(a, b, dims=(((1,), (0,)), ((), ())), parts=3, a_exact=False, b_exact=False):
    pa = [a.astype(BF16)] if a_exact else _split(a, parts)
    pb = [b.astype(BF16)] if b_exact else _split(b, parts)
    out = None
    for i, ai in enumerate(pa):
        for j, bj in enumerate(pb):
            if i + j < parts:
                t = lax.dot_general(ai, bj, dims, preferred_element_type=F32)
                out = t if out is None else out + t
    return out


def _sigmoid(x):
    return 1.0 / (1.0 + jnp.exp(-x))


def _silu(x):
    return x * _sigmoid(x)


def _log_sigmoid(x):
    return jnp.minimum(x, 0.0) - jnp.log(1.0 + jnp.exp(-jnp.abs(x)))


def _gelu_tanh(x):
    return 0.5 * x * (1.0 + jnp.tanh(math.sqrt(2.0 / math.pi) * (x + 0.044715 * (x * x * x))))


def _rms(x, gain):
    return x * lax.rsqrt(jnp.mean(x * x, axis=-1, keepdims=True) + EPS) * gain


def _ada_body(c_ref, w_ref, b_ref, o_ref):
    s = _silu(c_ref[...])
    o_ref[...] = _mm(s, w_ref[...]) + b_ref[...]


def _ada_mod(cvec, w, b):
    rows, d = cvec.shape
    depth, _, n = w.shape
    tn = 1024
    return pl.pallas_call(
        _ada_body,
        grid=(depth, n // tn),
        in_specs=[pl.BlockSpec((rows, d), lambda l, i: (0, 0)),
                  pl.BlockSpec((None, d, tn), lambda l, i: (l, 0, i)),
                  pl.BlockSpec((None, 1, tn), lambda l, i: (l, 0, i))],
        out_specs=pl.BlockSpec((None, rows, tn), lambda l, i: (l, 0, i)),
        out_shape=jax.ShapeDtypeStruct((depth, rows, n), F32),
        compiler_params=_cparams(("arbitrary", "arbitrary"), 40),
        name="ada_mod",
    )(cvec, w, b.reshape(depth, 1, n))


def _mod_row(i, tiles_per_batch, ctx_tiles, n_batch, layer):
    return layer * MOD_ROWS + jnp.where(i % tiles_per_batch < ctx_tiles, n_batch, i // tiles_per_batch)


def _cast_specs(casts, layer, n_steps):
    in_specs, out_specs, out_shapes = [], [], []
    for a in casts:
        nblk, r, c = a.shape[1:]
        assert n_steps >= nblk
        in_specs.append(pl.BlockSpec((None, None, r, c),
                                     lambda i, nblk=nblk: (layer, jnp.minimum(i, nblk - 1), 0, 0)))
        out_specs.append(pl.BlockSpec((None, r, c), lambda i, nblk=nblk: (jnp.minimum(i, nblk - 1), 0, 0)))
        out_shapes.append(jax.ShapeDtypeStruct((nblk, r, c), BF16))
    return in_specs, out_specs, out_shapes


def _cast_blocks(src_refs, dst_refs):
    for src, dst in zip(src_refs, dst_refs):
        dst[...] = src[...].astype(dst.dtype)


def _proj_body(*refs, splits, n_cast, two_sources, ctx_tiles, tiles_per_batch, heads):
    refs = list(refs)
    x_ref = refs.pop(0)
    ctx_ref = refs.pop(0) if two_sources else None
    sh_ref, sc_ref, g_ref, w_ref, qg_ref, kg_ref, cos_ref, sin_ref = refs[:8]
    refs = refs[8:]
    cast_in, out_refs, cast_out = refs[:n_cast], refs[n_cast:len(refs) - n_cast], refs[len(refs) - n_cast:]
    x = x_ref[...]
    if two_sources:
        x = jnp.where(pl.program_id(0) % tiles_per_batch < ctx_tiles, ctx_ref[...], x)
    y = _rms(x, g_ref[...])
    h = (y * (1.0 + sc_ref[0]) + sh_ref[0]).astype(BF16)
    ones_hd, swap_halves = _head_mats()
    cos, sin = cos_ref[...], sin_ref[...]
    hd = SWA_HEAD_DIM
    for k, (o_ref, (c0, width)) in enumerate(zip(out_refs, splits)):
        acc = jnp.dot(h, w_ref[:, c0:c0 + width], preferred_element_type=F32)
        if k in heads:
            n_heads, gain_ref, scale = heads[k][0], (qg_ref, kg_ref)[heads[k][1]], heads[k][2]
            for hh in range(n_heads):
                cols = slice(hh * hd, (hh + 1) * hd)
                rot = _head_rms_rope(acc[:, cols], gain_ref[...], cos, sin, ones_hd, swap_halves)
                o_ref[:, cols] = (rot * scale).astype(o_ref.dtype)
            if n_heads * hd < width:
                o_ref[:, n_heads * hd:] = acc[:, n_heads * hd:].astype(o_ref.dtype)
        else:
            o_ref[...] = acc.astype(o_ref.dtype)
    _cast_blocks(cast_in, cast_out)


def _row_sources(x_src, tiles_per_batch, ctx_tiles):
    tm = ROW_TILE
    if not isinstance(x_src, tuple):
        return [x_src], [pl.BlockSpec((tm, x_src.shape[1]), lambda i: (i, 0))]
    lat, ctxa = x_src
    lat_tiles = tiles_per_batch - ctx_tiles

    def lat_map(i):
        return ((i // tiles_per_batch) * lat_tiles + jnp.maximum(i % tiles_per_batch - ctx_tiles, 0), 0)

    def ctx_map(i):
        return ((i // tiles_per_batch) * ctx_tiles + jnp.minimum(i % tiles_per_batch, ctx_tiles - 1), 0)
    return [lat, ctxa], [pl.BlockSpec((tm, lat.shape[1]), lat_map), pl.BlockSpec((tm, ctxa.shape[1]), ctx_map)]


def _in_proj(x_src, rows, mod3, gain, w, outs, heads, q_gain, k_gain, cos_tab, sin_tab,
             tiles_per_batch, ctx_tiles, n_batch, layer, casts):
    d = w.shape[1]
    tm = ROW_TILE
    hd = SWA_HEAD_DIM
    mrow = functools.partial(_mod_row, tiles_per_batch=tiles_per_batch, ctx_tiles=ctx_tiles,
                             n_batch=n_batch, layer=layer)
    splits, out_specs, out_shapes = [], [], []
    c0 = 0
    for width, dtype in outs:
        splits.append((c0, width))
        out_specs.append(pl.BlockSpec((tm, width), lambda i: (i, 0)))
        out_shapes.append(jax.ShapeDtypeStruct((rows, width), dtype))
        c0 += width
    cast_in, cast_out, cast_shapes = _cast_specs(casts, layer, rows // tm)
    x_args, x_specs = _row_sources(x_src, tiles_per_batch, ctx_tiles)
    tab = pl.BlockSpec((tm, hd), lambda i: (i % tiles_per_batch, 0))
    res = pl.pallas_call(
        functools.partial(_proj_body, splits=tuple(splits), n_cast=len(casts), two_sources=len(x_args) == 2,
                          ctx_tiles=ctx_tiles, tiles_per_batch=tiles_per_batch, heads=heads),
        grid=(rows // tm,),
        in_specs=x_specs + [
                  pl.BlockSpec((1, 1, d), lambda i: (mrow(i) * 6 + 0, 0, 0)),
                  pl.BlockSpec((1, 1, d), lambda i: (mrow(i) * 6 + 1, 0, 0)),
                  pl.BlockSpec((1, d), lambda i: (0, 0)),
                  pl.BlockSpec((None,) + w.shape[1:], lambda i: (layer, 0, 0),
                               pipeline_mode=pl.Buffered(1)),
                  pl.BlockSpec((1, hd), lambda i: (0, 0)), pl.BlockSpec((1, hd), lambda i: (0, 0)),
                  tab, tab] + cast_in,
        out_specs=out_specs + cast_out,
        out_shape=out_shapes + cast_shapes,
        compiler_params=_cparams(("arbitrary",), 56),
        name="in_proj",
    )(*x_args, mod3, mod3, gain.reshape(1, d), w, q_gain.reshape(1, hd).astype(F32),
      k_gain.reshape(1, hd).astype(F32), cos_tab, sin_tab, *casts)
    return res[:len(outs)], res[len(outs):]


def _gla_body(qkf, vf, lrf, qkb, vb, lrb, wa_ref, ba_ref, of_ref, ob_ref, sf_ref, sb_ref):
    t = pl.program_id(1)
    ck = GLA_CHUNK
    hk = GLA_HEADS * GLA_DK

    @pl.when(t == 0)
    def _init():
        sf_ref[...] = jnp.zeros_like(sf_ref)
        sb_ref[...] = jnp.zeros_like(sb_ref)

    tmr = qkf.shape[0]
    n_chunks = tmr // ck
    rowb = lax.broadcasted_iota(I32, (tmr, tmr), 0)
    colb = lax.broadcasted_iota(I32, (tmr, tmr), 1)
    same_chunk = rowb // ck == colb // ck
    row4 = lax.broadcasted_iota(I32, (GLA_HEADS * ck, ck), 0) % ck
    col4 = lax.broadcasted_iota(I32, (GLA_HEADS * ck, ck), 1)
    lane = lax.broadcasted_iota(I32, (1, hk), 1)
    head_masks = [(lane // GLA_DK == h).astype(F32) for h in range(GLA_HEADS)]
    sel = (lax.broadcasted_iota(I32, (tmr, n_chunks * GLA_DV), 0) // ck
           == lax.broadcasted_iota(I32, (tmr, n_chunks * GLA_DV), 1) // GLA_DV).astype(F32)

    def run(qk_ref, v_ref, lr_ref, w, bias, s_ref, o_ref, reverse):
        tri = (same_chunk & (rowb <= colb if reverse else rowb >= colb)).astype(F32)
        amask = row4 <= col4 if reverse else row4 >= col4
        la = _log_sigmoid(_mm_split(lr_ref[...], w, parts=2) + bias) * (1.0 / GLA_GATE_NORM)
        b = _mm_split(tri, la, a_exact=True)
        decay_all = jnp.exp(_mm_split(la, sel, dims=(((0,), (0,)), ((), ())), b_exact=True))
        q = qk_ref[:, 0:hk].astype(F32)
        k = qk_ref[:, hk:2 * hk].astype(F32)
        qd_all = q * (GLA_DK ** -0.5) * jnp.exp(b)
        kd_all = k * jnp.exp(-b)
        state = s_ref[...]
        order = range(n_chunks - 1, -1, -1) if reverse else range(n_chunks)
        for ci in order:
            rows = slice(ci * ck, (ci + 1) * ck)
            v = v_ref[rows, :]
            last = ci * ck if reverse else (ci + 1) * ck - 1
            ke = k[rows, :] * jnp.exp(b[last:last + 1, :] - b[rows, :])
            qd = qd_all[rows, :]
            qs = jnp.concatenate([qd * m for m in head_masks], axis=0).astype(BF16)
            att = jnp.where(amask, _mm_nt(qs, kd_all[rows, :]), 0.0)
            inter = _mm(qs, state)
            outs = []
            for h in range(GLA_HEADS):
                hr = slice(h * ck, (h + 1) * ck)
                outs.append(_mm(att[hr, :], v[:, h * GLA_DV:(h + 1) * GLA_DV]) + inter[hr, :])
            o_ref[rows, :] = jnp.concatenate(outs, axis=1)
            ds = _mm_tn(ke, v)
            ds = jnp.concatenate([ds[h * GLA_DK:(h + 1) * GLA_DK, h * GLA_DV:(h + 1) * GLA_DV]
                                  for h in range(GLA_HEADS)], axis=0)
            state = decay_all[:, ci * GLA_DV:(ci + 1) * GLA_DV] * state + ds
        s_ref[...] = state

    run(qkf, vf, lrf, wa_ref[:, 0:hk], ba_ref[:, 0:hk], sf_ref, of_ref, False)
    run(qkb, vb, lrb, wa_ref[:, hk:2 * hk], ba_ref[:, hk:2 * hk], sb_ref, ob_ref, True)


def _gla_scan(qk, v, lr, wa, ba, n_batch, tiles_per_batch, ctx_tiles):
    rows = qk.shape[0]
    tm = ROW_TILE
    tpb = tiles_per_batch

    def fwd(b, t):
        return (b * tpb + t, 0)

    def bwd(b, t):
        return (b * tpb + jnp.where(t < ctx_tiles, ctx_tiles - 1 - t, tpb - 1 - (t - ctx_tiles)), 0)

    wv = v.shape[1]
    return pl.pallas_call(
        _gla_body,
        grid=(n_batch, tpb),
        in_specs=[pl.BlockSpec((tm, qk.shape[1]), fwd), pl.BlockSpec((tm, wv), fwd),
                  pl.BlockSpec((tm, lr.shape[1]), fwd),
                  pl.BlockSpec((tm, qk.shape[1]), bwd), pl.BlockSpec((tm, wv), bwd),
                  pl.BlockSpec((tm, lr.shape[1]), bwd),
                  pl.BlockSpec(wa.shape, lambda b, t: (0, 0)),
                  pl.BlockSpec(ba.shape, lambda b, t: (0, 0))],
        out_specs=[pl.BlockSpec((tm, wv), fwd), pl.BlockSpec((tm, wv), bwd)],
        out_shape=[jax.ShapeDtypeStruct((rows, wv), F32)] * 2,
        scratch_shapes=[pltpu.VMEM((GLA_HEADS * GLA_DK, GLA_DV), F32)] * 2,
        compiler_params=_cparams(("arbitrary", "arbitrary"), 32),
        name="gla_scan",
    )(qk, v, lr, qk, v, lr, wa, ba)


def _s5_prep(a_re, a_im, log_dt, b_re, b_im, c_re, c_im, d):
    tt = S5_CHUNK
    n_groups = a_re.shape[1]
    bg = S5_BLOCK_GROUPS
    nb = n_groups // bg
    a_re, a_im, b_re, b_im = a_re.astype(F32), a_im.astype(F32), b_re.astype(F32), b_im.astype(F32)
    c_re, c_im = c_re.astype(F32), c_im.astype(F32)
    dt = jnp.exp(log_dt.astype(F32))[..., None]
    mag = jnp.exp(dt * a_re)
    ab_re, ab_im = mag * jnp.cos(dt * a_im), mag * jnp.sin(dt * a_im)
    den = a_re * a_re + a_im * a_im
    nr = ab_re - 1.0
    coef_re = (nr * a_re + ab_im * a_im) / den
    coef_im = (ab_im * a_re - nr * a_im) / den
    bb_re = coef_re[..., None] * b_re - coef_im[..., None] * b_im
    bb_im = coef_re[..., None] * b_im + coef_im[..., None] * b_re
    pr, pi = [jnp.ones_like(ab_re)], [jnp.zeros_like(ab_im)]
    for _ in range(tt):
        pr, pi = pr + [pr[-1] * ab_re - pi[-1] * ab_im], pi + [pr[-1] * ab_im + pi[-1] * ab_re]
    pr, pi = jnp.stack(pr), jnp.stack(pi)
    cpr = c_re[None] * pr[:, :, :, None, :] - c_im[None] * pi[:, :, :, None, :]
    cpi = c_re[None] * pi[:, :, :, None, :] + c_im[None] * pr[:, :, :, None, :]
    kern = (jnp.einsum('kdgcn,dgnm->kdgcm', cpr, bb_re)
            - jnp.einsum('kdgcn,dgnm->kdgcm', cpi, bb_im))
    kdim, half = tt * LANES, bg * S5_STATE
    ax_k, ax_s, ax_tc = jnp.arange(kdim), jnp.arange(half), jnp.arange(tt * S5_GROUP)
    k_group, k_step, k_chan = (ax_k // S5_GROUP) % bg, ax_k // LANES, ax_k % S5_GROUP
    rep_tc = ((ax_tc[:, None] // S5_GROUP == k_step[None, :])
              & (ax_tc[:, None] % S5_GROUP == k_chan[None, :])).astype(F32)
    rep_n = (jnp.arange(S5_STATE)[:, None] == ax_s[None, :] % S5_STATE).astype(F32)
    mask_kk = k_group[:, None] == k_group[None, :]
    mask_ks = k_group[:, None] == ax_s[None, :] // S5_STATE
    expand = lambda table, rep: jnp.einsum('jrk,kc->jrc', table, rep)

    s_idx = jnp.arange(tt)[:, None]
    t_idx = jnp.arange(tt)[None, :]
    lag_f = jnp.clip(t_idx - s_idx, 0, tt)
    lag_b = jnp.clip(s_idx - t_idx, 0, tt)
    toe = (jnp.where((t_idx >= s_idx)[:, :, None, None, None], kern[lag_f, 0], 0.0)
           + jnp.where((s_idx >= t_idx)[:, :, None, None, None], kern[lag_b, 1], 0.0))
    skip = (jnp.eye(tt, dtype=F32)[:, :, None, None, None]
            * d.astype(F32).reshape(n_groups, S5_GROUP)[None, None, :, :, None]
            * jnp.eye(S5_GROUP, dtype=F32)[None, None, None])
    toe = (toe + skip).reshape(tt, tt, nb, bg, S5_GROUP, S5_GROUP)
    toe = toe.transpose(2, 0, 3, 5, 1, 4).reshape(nb, kdim, tt * S5_GROUP)
    m_mat = jnp.where(mask_kk, expand(toe, rep_tc), 0.0)

    def inject(direction, power_of_s):
        p_re, p_im = pr[power_of_s, direction], pi[power_of_s, direction]
        v_re = p_re[..., None] * bb_re[direction][None] - p_im[..., None] * bb_im[direction][None]
        v_im = p_re[..., None] * bb_im[direction][None] + p_im[..., None] * bb_re[direction][None]
        def blk(v):
            v = v.reshape(tt, nb, bg, S5_STATE, S5_GROUP).transpose(1, 0, 2, 4, 3).reshape(nb, kdim, S5_STATE)
            return jnp.where(mask_ks, expand(v, rep_n), 0.0)
        return jnp.concatenate([blk(v_re), blk(v_im)], axis=-1)

    def readout(direction, power_of_t):
        e_re, e_im = cpr[power_of_t, direction], -cpi[power_of_t, direction]
        def blk(v):
            v = v.reshape(tt, nb, bg, S5_GROUP, S5_STATE).transpose(1, 2, 4, 0, 3).reshape(nb, half, tt * S5_GROUP)
            return jnp.where(mask_ks.T, expand(v, rep_tc), 0.0)
        return jnp.concatenate([blk(e_re), blk(e_im)], axis=1)

    steps = jnp.arange(tt)
    g_f = inject(0, tt - 1 - steps)
    g_b = inject(1, steps)
    e_f = readout(0, steps + 1)
    e_b = readout(1, tt - steps)
    a_t = jnp.stack([pr[tt, 0].reshape(nb, half), pi[tt, 0].reshape(nb, half),
                     pr[tt, 1].reshape(nb, half), pi[tt, 1].reshape(nb, half)], axis=1)
    cast = lambda z: z.astype(BF16)
    return cast(m_mat), cast(g_f), cast(g_b), cast(e_f), cast(e_b), a_t


def _s5_body(u_ref, m_ref, gf_ref, gb_ref, ef_ref, eb_ref, a_ref, y_ref, sf_ref, sb_ref,
             *, ctx_rows, n_rows, n_batch):
    half = S5_BLOCK_GROUPS * S5_STATE
    row_blocks = [slice(r * n_rows, (r + 1) * n_rows) for r in range(n_batch)]

    for rows in row_blocks:
        sf_ref[rows, :] = jnp.dot(u_ref[rows, :], gf_ref[0], preferred_element_type=F32)
        sb_ref[rows, :] = jnp.dot(u_ref[rows, :], gb_ref[0], preferred_element_type=F32)

    a = a_ref[0]
    afr, afi, abr, abi = a[0:1, :], a[1:2, :], a[2:3, :], a[3:4, :]

    sub = 8
    cpt = sub // n_batch
    n_tiles, ctx_tiles = n_rows // cpt, ctx_rows // cpt

    def advance(a_re, a_im, s_re, s_im, g):
        return a_re * s_re - a_im * s_im + g[:, 0:half], a_re * s_im + a_im * s_re + g[:, half:2 * half]

    def step(k, carry):
        fr, fi, br, bi = carry
        rf = pl.ds(pl.multiple_of(k * sub, sub), sub)
        g = sf_ref[rf, :]
        ins_r, ins_i = [], []
        for c in range(cpt):
            ins_r.append(fr)
            ins_i.append(fi)
            fr, fi = advance(afr, afi, fr, fi, g[c * n_batch:(c + 1) * n_batch, :])
        sf_ref[rf, 0:half] = jnp.concatenate(ins_r, axis=0)
        sf_ref[rf, half:2 * half] = jnp.concatenate(ins_i, axis=0)
        kb = jnp.where(k < ctx_tiles, ctx_tiles - 1 - k, n_tiles - 1 - (k - ctx_tiles))
        rb = pl.ds(pl.multiple_of(kb * sub, sub), sub)
        g = sb_ref[rb, :]
        ins_r, ins_i = [None] * cpt, [None] * cpt
        for c in range(cpt - 1, -1, -1):
            ins_r[c], ins_i[c] = br, bi
            br, bi = advance(abr, abi, br, bi, g[c * n_batch:(c + 1) * n_batch, :])
        sb_ref[rb, 0:half] = jnp.concatenate(ins_r, axis=0)
        sb_ref[rb, half:2 * half] = jnp.concatenate(ins_i, axis=0)
        return fr, fi, br, bi

    zero = jnp.zeros((n_batch, half), F32)
    lax.fori_loop(0, n_tiles, step, (zero, zero, zero, zero))

    for rows in row_blocks:
        y = (jnp.dot(u_ref[rows, :], m_ref[0], preferred_element_type=F32)
             + jnp.dot(sf_ref[rows, :].astype(BF16), ef_ref[0], preferred_element_type=F32)
             + jnp.dot(sb_ref[rows, :].astype(BF16), eb_ref[0], preferred_element_type=F32))
        y_ref[rows, :] = y.astype(y_ref.dtype)


def _s5_scan(u, mats, n_batch, ctx_len, layer):
    m_mat, g_f, g_b, e_f, e_b, a_t = mats
    rows, width = u.shape
    tt = S5_CHUNK
    nb = width // LANES
    srows = rows // tt // n_batch
    kdim = tt * LANES
    sdim = 2 * S5_BLOCK_GROUPS * S5_STATE
    u5 = u.reshape(n_batch, srows, tt, nb, LANES).transpose(3, 1, 0, 2, 4).reshape(nb, srows * n_batch, kdim)
    once = dict(pipeline_mode=pl.Buffered(1))
    wspec = lambda shape: pl.BlockSpec((None, 1) + shape, lambda j: (layer, j, 0, 0), **once)
    y5 = pl.pallas_call(
        functools.partial(_s5_body, ctx_rows=ctx_len // tt, n_rows=srows, n_batch=n_batch),
        grid=(nb,),
        in_specs=[pl.BlockSpec((None, srows * n_batch, kdim), lambda j: (j, 0, 0), **once),
                  wspec((kdim, kdim)), wspec((kdim, sdim)), wspec((kdim, sdim)),
                  wspec((sdim, kdim)), wspec((sdim, kdim)), wspec((4, sdim // 2))],
        out_specs=pl.BlockSpec((None, srows * n_batch, kdim), lambda j: (j, 0, 0)),
        out_shape=jax.ShapeDtypeStruct((nb, srows * n_batch, kdim), BF16),
        scratch_shapes=[pltpu.VMEM((srows * n_batch, sdim), F32), pltpu.VMEM((srows * n_batch, sdim), F32)],
        compiler_params=_cparams(("arbitrary",), 56),
        name="s5_scan",
    )(u5, m_mat, g_f, g_b, e_f, e_b, a_t)
    return y5.reshape(nb, srows, n_batch, tt, LANES).transpose(2, 1, 3, 0, 4).reshape(rows, width)


def _rope_tables(seq_len):
    pos = jnp.arange(seq_len)
    row = (pos // GRID_W).astype(F32)
    col = (pos % GRID_W).astype(F32)
    inv_freq = ROPE_BASE ** (-jnp.arange(ROPE_PAIRS, dtype=F32) / ROPE_PAIRS)
    ar, ac = row[:, None] * inv_freq, col[:, None] * inv_freq
    cos = jnp.concatenate([jnp.cos(ar), jnp.cos(ar), jnp.cos(ac), jnp.cos(ac)], axis=1)
    sin = jnp.concatenate([-jnp.sin(ar), jnp.sin(ar), -jnp.sin(ac), jnp.sin(ac)], axis=1)
    return cos, sin


def _head_mats():
    hd = SWA_HEAD_DIM
    src = lax.broadcasted_iota(I32, (hd, hd), 0)
    dst = lax.broadcasted_iota(I32, (hd, hd), 1)
    partner = jnp.where(dst % (2 * ROPE_PAIRS) < ROPE_PAIRS, dst + ROPE_PAIRS, dst - ROPE_PAIRS)
    return jnp.ones((hd, hd), BF16), (src == partner).astype(BF16)


def _head_rms_rope(x, gain, cos, sin, ones_hd, swap_halves):
    ss = _mm(x * x, ones_hd)
    xn = x * lax.rsqrt(ss * (1.0 / SWA_HEAD_DIM) + EPS) * gain
    return xn * cos + _mm(xn, swap_halves) * sin


def _swa_body(sink_ref, q_ref, kp_ref, kc_ref, kn_ref, kx_ref, o_ref, *, lat_blocks):
    n = pl.program_id(1)
    hd = SWA_HEAD_DIM
    blk = SWA_BLOCK
    kvw = SWA_KV_HEADS * hd
    row = lax.broadcasted_iota(I32, (SWA_GROUP * blk, blk), 0) % blk
    col = lax.broadcasted_iota(I32, (SWA_GROUP * blk, blk), 1)
    ones_hd = jnp.ones((hd, hd), BF16)

    def lane_fold(op, blocks):
        parts = [b[:, c0:c0 + blk] for b in blocks for c0 in range(0, b.shape[1], blk)]
        out = parts[0]
        for part in parts[1:]:
            out = op(out, part)
        return out

    def attend(is_lat):
        for g in range(SWA_KV_HEADS):
            kcols = slice(g * hd, (g + 1) * hd)
            vcols = slice(kvw + g * hd, kvw + (g + 1) * hd)
            keys = [kx_ref[:, kcols]]
            values = [kx_ref[:, vcols]]
            valid = [None]
            if is_lat:
                wins = (kp_ref, kc_ref, kn_ref)
                keys += [r[:, kcols] for r in wins]
                values += [r[:, vcols] for r in wins]
                valid += [(col >= row) & (col + (n - 1) * blk >= 0), None,
                          (col <= row) & (col + (n + 1) * blk < lat_blocks * blk)]
            heads = range(g * SWA_GROUP, (g + 1) * SWA_GROUP)
            q4 = jnp.concatenate([q_ref[:, h * hd:(h + 1) * hd] for h in heads], axis=0)
            sink = jnp.concatenate([jnp.full((blk, 1), sink_ref[h] * LOG2E, F32) for h in heads], axis=0)
            scores = []
            for kk, ok in zip(keys, valid):
                s = _mm_nt(q4, kk)
                scores.append(s if ok is None else jnp.where(ok, s, NEG_INF))
            m = jnp.maximum(jnp.max(lane_fold(jnp.maximum, scores), axis=-1, keepdims=True), sink)
            probs = [jnp.exp2(s - m).astype(BF16) for s in scores]
            den = jnp.exp2(sink - m)
            for p in probs:
                for c0 in range(0, p.shape[1], blk):
                    den = den + _mm(p[:, c0:c0 + blk], ones_hd)
            acc = None
            for p, vv in zip(probs, values):
                pv = _mm(p, vv)
                acc = pv if acc is None else acc + pv
            out = acc * (1.0 / den)
            for hh, h in enumerate(heads):
                o_ref[:, h * hd:(h + 1) * hd] = out[hh * blk:(hh + 1) * blk, :].astype(o_ref.dtype)

    @pl.when(n < lat_blocks)
    def _lat():
        attend(True)

    @pl.when(n >= lat_blocks)
    def _ctx():
        attend(False)


def _swa_attend(q, kv, sink, n_batch, seq_len, ctx_len):
    rows = q.shape[0]
    blk = SWA_BLOCK
    lat_blocks = seq_len // blk
    ctx_blocks = ctx_len // blk
    bpb = lat_blocks + ctx_blocks

    def q_map(b, n, s):
        return (b * bpb + jnp.where(n < lat_blocks, ctx_blocks + n, n - lat_blocks), 0)

    def win(off):
        return lambda b, n, s: (b * bpb + ctx_blocks + jnp.clip(n + off, 0, lat_blocks - 1), 0)

    kvs = lambda m: pl.BlockSpec((blk, kv.shape[1]), m)
    grid_spec = pltpu.PrefetchScalarGridSpec(
        num_scalar_prefetch=1,
        grid=(n_batch, bpb),
        in_specs=[pl.BlockSpec((blk, q.shape[1]), q_map),
                  kvs(win(-1)), kvs(win(0)), kvs(win(1)),
                  pl.BlockSpec((ctx_len, kv.shape[1]), lambda b, n, s: (b * (bpb * blk // ctx_len), 0))],
        out_specs=pl.BlockSpec((blk, q.shape[1]), q_map),
    )
    return pl.pallas_call(
        functools.partial(_swa_body, lat_blocks=lat_blocks),
        grid_spec=grid_spec,
        out_shape=jax.ShapeDtypeStruct((rows, q.shape[1]), BF16),
        compiler_params=_cparams(("arbitrary", "arbitrary"), 32),
        name="swa_attend",
    )(sink.astype(F32), q, kv, kv, kv, kv)


def _route_rows(logits_t, bias_col):
    aff = _sigmoid(logits_t)
    biased = aff + bias_col
    v = [biased[e:e + 1, :] for e in range(N_EXPERTS)]
    a = [aff[e:e + 1, :] for e in range(N_EXPERTS)]
    gsz = EXPERTS_PER_GROUP
    best_g = best_s = None
    for g in range(N_EXPERT_GROUPS):
        vg = v[g * gsz:(g + 1) * gsz]
        score = None
        for i in range(gsz):
            for k in range(i + 1, gsz):
                pair = vg[i] + vg[k]
                score = pair if score is None else jnp.maximum(score, pair)
        if best_g is None:
            best_g, best_s = jnp.zeros_like(score, dtype=I32), score
        else:
            upd = score > best_s
            best_g = jnp.where(upd, g, best_g)
            best_s = jnp.where(upd, score, best_s)

    def pick(rows):
        out = []
        for i in range(gsz):
            x = rows[i]
            for g in range(1, N_EXPERT_GROUPS):
                x = jnp.where(best_g == g, rows[g * gsz + i], x)
            out.append(x)
        return out

    vb, ab = pick(v), pick(a)
    i1, v1, a1 = jnp.zeros_like(best_g), vb[0], ab[0]
    for i in range(1, gsz):
        upd = vb[i] > v1
        i1, v1, a1 = jnp.where(upd, i, i1), jnp.where(upd, vb[i], v1), jnp.where(upd, ab[i], a1)
    i2 = v2 = a2 = None
    for i in range(gsz):
        cand = jnp.where(i1 == i, -jnp.inf, vb[i])
        if i2 is None:
            i2, v2, a2 = jnp.zeros_like(best_g), cand, ab[0]
        else:
            upd = cand > v2
            i2, v2, a2 = jnp.where(upd, i, i2), jnp.where(upd, cand, v2), jnp.where(upd, ab[i], a2)
    tot = a1 + a2
    e1 = (best_g * gsz + i1).astype(F32)
    e2 = (best_g * gsz + i2).astype(F32)
    zeros = jnp.zeros((4, e1.shape[1]), F32)
    return jnp.concatenate([e1, e2, a1 / tot, a2 / tot, zeros], axis=0)


def _out_body(of_ref, ob_ref, r_ref, ys_ref, sw_ref, *refs, two_sources, ctx_tiles, tiles_per_batch):
    refs = list(refs)
    x_ref = refs.pop(0)
    ctx_ref = refs.pop(0) if two_sources else None
    (g1_ref, sh_ref, sc_ref, n2_ref, og_ref, wglu_ref, bglu_ref, wo_ref, rw_ref, rb_ref, cast_in,
     xo_ref, h_ref, rt_ref, cast_out) = refs
    _cast_blocks([cast_in], [cast_out])
    x_res = x_ref[...]
    if two_sources:
        x_res = jnp.where(pl.program_id(0) % tiles_per_batch < ctx_tiles, ctx_ref[...], x_res)
    o = of_ref[...] + ob_ref[...]
    r = r_ref[...].astype(F32)
    gain = og_ref[...]
    heads = []
    for h in range(GLA_HEADS):
        cols = slice(h * GLA_DV, (h + 1) * GLA_DV)
        heads.append(_rms(o[:, cols], gain[:, cols]))
    gla_y = jnp.concatenate(heads, axis=1) * _silu(r)
    z = _gelu_tanh(ys_ref[...].astype(F32))
    s5_y = z * _sigmoid(_mm(z, wglu_ref[...]) + bglu_ref[...])
    mixed = jnp.concatenate([gla_y.astype(BF16), s5_y.astype(BF16), sw_ref[...]], axis=1)
    y = jnp.dot(mixed, wo_ref[...], preferred_element_type=F32)
    x_new = x_res + g1_ref[0] * y
    xo_ref[...] = x_new
    h2 = _rms(x_new, n2_ref[...]) * (1.0 + sc_ref[0]) + sh_ref[0]
    h_ref[...] = h2
    logits_t = _mm_split(rw_ref[...], h2, dims=(((1,), (1,)), ((), ())), parts=2)
    rt_ref[0] = _route_rows(logits_t, rb_ref[...])


def _out_proj(o_f, o_b, r, ys, sw, x_src, mod3, norm2, o_gain, w_glu, b_glu, w_out, rw_t, rb_col,
              tiles_per_batch, ctx_tiles, n_batch, layer, cast):
    rows, d = o_f.shape[0], w_out.shape[2]
    tm = ROW_TILE
    nt = rows // tm
    mrow = functools.partial(_mod_row, tiles_per_batch=tiles_per_batch, ctx_tiles=ctx_tiles,
                             n_batch=n_batch, layer=layer)
    x_args, x_specs = _row_sources(x_src, tiles_per_batch, ctx_tiles)
    rowspec = lambda a: pl.BlockSpec((tm, a.shape[1]), lambda i: (i, 0))
    const = lambda a: pl.BlockSpec(a.shape, lambda i: (0,) * a.ndim)
    layered = lambda a, **kw: pl.BlockSpec((None,) + a.shape[1:], lambda i: (layer,) + (0,) * (a.ndim - 1), **kw)
    modspec = lambda k: pl.BlockSpec((1, 1, d), lambda i: (mrow(i) * 6 + k, 0, 0))
    consts = [norm2.reshape(1, d), o_gain.reshape(1, -1)]
    cast_in, cast_out, cast_shapes = _cast_specs([cast], layer, nt)
    return pl.pallas_call(
        functools.partial(_out_body, two_sources=len(x_args) == 2, ctx_tiles=ctx_tiles,
                          tiles_per_batch=tiles_per_batch),
        grid=(nt,),
        in_specs=[rowspec(o_f), rowspec(o_b), rowspec(r), rowspec(ys), rowspec(sw)] + x_specs
                 + [modspec(2), modspec(3), modspec(4)]
                 + [const(a) for a in consts]
                 + [layered(w_glu), const(b_glu.reshape(1, -1)),
                    layered(w_out, pipeline_mode=pl.Buffered(1)), const(rw_t), const(rb_col)] + cast_in,
        out_specs=[pl.BlockSpec((tm, d), lambda i: (i, 0)), pl.BlockSpec((tm, d), lambda i: (i, 0)),
                   pl.BlockSpec((1, 8, tm), lambda i: (i, 0, 0))] + cast_out,
        out_shape=[jax.ShapeDtypeStruct((rows, d), F32), jax.ShapeDtypeStruct((rows, d), F32),
                   jax.ShapeDtypeStruct((nt, 8, tm), F32)] + cast_shapes,
        compiler_params=_cparams(("arbitrary",), 48),
        name="out_proj",
    )(o_f, o_b, r, ys, sw, *x_args, mod3, mod3, mod3, *consts, w_glu, b_glu.reshape(1, -1), w_out, rw_t, rb_col,
      cast)


def _row_copy(src_hbm, row, dst, slot, sem):
    return pltpu.make_async_copy(src_hbm.at[pl.ds(row, 1)], dst.at[pl.ds(slot, 1)], sem)


def _gather_start(src_hbm, idx_ref, dst, sem, n_rows, unrolled=False):
    if unrolled:
        for r in range(n_rows):
            _row_copy(src_hbm, idx_ref[0, 0, r], dst, r, sem).start()
        return

    def body(g, carry):
        for u in range(DMA_UNROLL):
            r = g * DMA_UNROLL + u
            _row_copy(src_hbm, idx_ref[0, 0, r], dst, r, sem).start()
        return carry
    lax.fori_loop(0, n_rows // DMA_UNROLL, body, 0)


def _gather_wait(src_hbm, dst, sem, n_rows):
    for r in range(n_rows):
        _row_copy(src_hbm, 0, dst, r, sem).wait()


def _ring_gather(step, n_steps, src_hbm, idx_refs, ring, sem, work):
    depth = GATHER_RING
    n_rows = ring.shape[1]

    @pl.when(step == 0)
    def _prime():
        for k in range(depth - 1):
            _gather_start(src_hbm, idx_refs[k], ring.at[k], sem.at[k], n_rows)

    slot = step % depth
    ahead = (step + depth - 1) % depth
    _gather_wait(src_hbm, ring.at[slot], sem.at[slot], n_rows)
    work(ring.at[slot],
         lambda unrolled: _gather_start(src_hbm, idx_refs[depth - 1], ring.at[ahead], sem.at[ahead], n_rows,
                                        unrolled))

    @pl.when(step == n_steps - 1)
    def _drain():
        for k in range(1, depth):
            late = (step + k) % depth
            _gather_wait(src_hbm, ring.at[late], sem.at[late], n_rows)


PLAN_EXPERT, PLAN_VALID, PLAN_FIRST, PLAN_SLOT, PLAN_NEXT, PLAN_HAS_NEXT = range(6)


def _moe_body(plan_ref, idx0_ref, idx1_ref, idx2_ref, x_hbm, wg_hbm, wu_hbm, wd_hbm, o_ref,
              ring, sem, wg_buf, wu_buf, wd_buf, wsem, *, n_steps):
    i = pl.program_id(0)
    expert = plan_ref[PLAN_EXPERT, i]
    wslot = plan_ref[PLAN_SLOT, i]

    def weight_copies(e, slot):
        pairs = ((wg_hbm, wg_buf), (wu_hbm, wu_buf), (wd_hbm, wd_buf))
        return [pltpu.make_async_copy(src.at[e], dst.at[slot], wsem.at[slot]) for src, dst in pairs]

    @pl.when(i == 0)
    def _first_weights():
        for c in weight_copies(expert, 0):
            c.start()

    @pl.when(plan_ref[PLAN_FIRST, i] > 0)
    def _switch():
        for c in weight_copies(expert, wslot):
            c.wait()

        @pl.when(plan_ref[PLAN_HAS_NEXT, i] > 0)
        def _prefetch():
            for c in weight_copies(plan_ref[PLAN_NEXT, i], 1 - wslot):
                c.start()

    def work(cur, start_next):
        @pl.when(plan_ref[PLAN_VALID, i] > 0)
        def _active():
            start_next(True)
            x = cur[...].astype(BF16)
            gate = jnp.dot(x, wg_buf[wslot], preferred_element_type=F32)
            up = jnp.dot(x, wu_buf[wslot], preferred_element_type=F32)
            act = (_silu(gate) * up).astype(BF16)
            o_ref[...] = jnp.dot(act, wd_buf[wslot], preferred_element_type=F32)

        @pl.when(plan_ref[PLAN_VALID, i] == 0)
        def _idle():
            start_next(False)
            o_ref[...] = jnp.zeros_like(o_ref)

    _ring_gather(i, n_steps, x_hbm, (idx0_ref, idx1_ref, idx2_ref), ring, sem, work)


def _moe_experts(plan, slot_token, tokens, w_gate, w_up, w_down):
    tm = MOE_TILE
    nt = plan.shape[1]
    d = tokens.shape[1]
    de = w_gate.shape[2]
    idx = lambda off: pl.BlockSpec((1, 1, tm), lambda i, plan: (jnp.minimum(i + off, nt - 1), 0, 0),
                                   memory_space=pltpu.SMEM)
    hbm = pl.BlockSpec(memory_space=pl.ANY)
    grid_spec = pltpu.PrefetchScalarGridSpec(
        num_scalar_prefetch=1,
        grid=(nt,),
        in_specs=[idx(k) for k in range(GATHER_RING)] + [hbm, hbm, hbm, hbm],
        out_specs=pl.BlockSpec((tm, d), lambda i, plan: (i, 0)),
        scratch_shapes=[pltpu.VMEM((GATHER_RING, tm, d), F32), pltpu.SemaphoreType.DMA((GATHER_RING,)),
                        pltpu.VMEM((2, d, de), BF16), pltpu.VMEM((2, d, de), BF16), pltpu.VMEM((2, de, d), BF16),
                        pltpu.SemaphoreType.DMA((2,))],
    )
    slots = slot_token.reshape(nt, 1, tm)
    return pl.pallas_call(
        functools.partial(_moe_body, n_steps=nt),
        grid_spec=grid_spec,
        out_shape=jax.ShapeDtypeStruct((nt * tm, d), F32),
        compiler_params=_cparams(("arbitrary",), 56),
        name="moe_experts",
    )(plan, *([slots] * GATHER_RING), tokens, w_gate, w_up, w_down)


def _combine_body(pos0_ref, pos1_ref, pos2_ref, x_ref, g2_ref, w_ref, y_hbm, o_ref, ring, sem, *, n_steps):
    tm = x_ref.shape[0]

    def work(cur, start_next):
        start_next(True)
        w = w_ref[...]
        f = w[:, 0:1] * cur[0:tm, :] + w[:, 1:2] * cur[tm:2 * tm, :]
        o_ref[...] = x_ref[...] + g2_ref[0] * f

    _ring_gather(pl.program_id(0), n_steps, y_hbm, (pos0_ref, pos1_ref, pos2_ref), ring, sem, work)


def _moe_combine(pos, x_new, mod3, wts, y_sorted, n_batch, tiles_per_batch, ctx_tiles, keep_ctx, layer):
    rows, d = x_new.shape
    tm = ROW_TILE
    tpb = tiles_per_batch
    first = 0 if keep_ctx else ctx_tiles
    out_tpb = tpb - first
    ns = n_batch * out_tpb
    in_tile = lambda s: (s // out_tpb) * tpb + first + s % out_tpb
    mrow = lambda s: layer * MOD_ROWS + jnp.where(first + s % out_tpb < ctx_tiles, n_batch, s // out_tpb)
    pos3 = pos.reshape(rows // tm, tm, 2).transpose(0, 2, 1).reshape(rows // tm, 1, 2 * tm)
    idx = lambda off: pl.BlockSpec((1, 1, 2 * tm), lambda s: (in_tile(jnp.minimum(s + off, ns - 1)), 0, 0),
                                   memory_space=pltpu.SMEM)
    return pl.pallas_call(
        functools.partial(_combine_body, n_steps=ns),
        grid=(ns,),
        in_specs=[idx(k) for k in range(GATHER_RING)] + [
                  pl.BlockSpec((tm, d), lambda s: (in_tile(s), 0)),
                  pl.BlockSpec((1, 1, d), lambda s: (mrow(s) * 6 + 5, 0, 0)),
                  pl.BlockSpec((tm, 2), lambda s: (in_tile(s), 0)),
                  pl.BlockSpec(memory_space=pl.ANY)],
        out_specs=pl.BlockSpec((tm, d), lambda s: (s, 0)),
        out_shape=jax.ShapeDtypeStruct((ns * tm, d), F32),
        scratch_shapes=[pltpu.VMEM((GATHER_RING, 2 * tm, d), F32), pltpu.SemaphoreType.DMA((GATHER_RING,))],
        compiler_params=_cparams(("arbitrary",), 40),
        name="moe_combine",
    )(*([pos3] * GATHER_RING), x_new, mod3, wts, y_sorted)


def _moe_plan(route, n_rows):
    tm = MOE_TILE
    e = route[:, 0:2, :].astype(I32).transpose(0, 2, 1).reshape(n_rows, 2)
    wts = route[:, 2:4, :].transpose(0, 2, 1).reshape(n_rows, 2)
    e_flat = e.reshape(-1)
    onehot = (e_flat[:, None] == jnp.arange(N_EXPERTS, dtype=I32)[None, :]).astype(I32)
    csum = jnp.cumsum(onehot, axis=0)
    rank = jnp.sum(onehot * csum, axis=1) - 1
    counts = csum[-1]
    tiles_e = (counts + tm - 1) // tm
    tile_end = jnp.cumsum(tiles_e)
    tile_start = tile_end - tiles_e
    pos = tile_start[e_flat] * tm + rank
    nt = (2 * n_rows) // tm + N_EXPERTS
    slot_token = jnp.zeros((nt * tm,), I32).at[pos].set(jnp.arange(2 * n_rows, dtype=I32) // 2,
                                                        unique_indices=True)
    n_active = tile_end[-1]
    tile_id = jnp.minimum(jnp.arange(nt, dtype=I32), n_active - 1)
    tile_expert = jnp.minimum(jnp.sum((tile_id[:, None] >= tile_end[None, :]).astype(I32), axis=1),
                              N_EXPERTS - 1).astype(I32)
    tile_ids = jnp.arange(nt, dtype=I32)
    tile_valid = (tile_ids < n_active).astype(I32)
    first = jnp.concatenate([jnp.ones((1,), I32), (tile_expert[1:] != tile_expert[:-1]).astype(I32)])
    wslot = (jnp.cumsum(first) - 1) % 2
    later_first = (tile_ids[None, :] > tile_ids[:, None]) & (first[None, :] > 0)
    next_first = jnp.min(jnp.where(later_first, tile_ids[None, :], nt), axis=1)
    has_next = (next_first < nt).astype(I32)
    next_expert = tile_expert[jnp.minimum(next_first, nt - 1)]
    plan = jnp.stack([tile_expert, tile_valid, first, wslot.astype(I32), next_expert, has_next]).astype(I32)
    return plan, slot_token, pos.reshape(n_rows, 2), wts


def kernel(x, c, ctx, c_ctx, ada_w, ada_b, norm1, norm2, w_in, gla_w_alpha, gla_b_alpha, gla_o_norm,
           s5_a_re, s5_a_im, s5_log_dt, s5_b_re, s5_b_im, s5_c_re, s5_c_im, s5_d, s5_w_glu, s5_b_glu,
           swa_q_norm, swa_k_norm, swa_sink, w_out, router_w, router_b, exp_w_gate, exp_w_up, exp_w_down):
    n_batch, seq_len, d = x.shape
    ctx_len = ctx.shape[1]
    depth = ada_w.shape[0]
    tm = ROW_TILE
    per_batch = ctx_len + seq_len
    assert ctx_len % tm == 0 and seq_len % tm == 0
    tpb, ctx_tiles = per_batch // tm, ctx_len // tm
    rows = n_batch * per_batch

    hk = GLA_HEADS * GLA_DK
    gla_w = GLA_HEADS * GLA_DV
    s5_w = s5_d.shape[1]
    swa_w = SWA_HEADS * SWA_HEAD_DIM
    kv_w = SWA_KV_HEADS * SWA_HEAD_DIM
    o_q, o_k, o_v = 0, hk, 2 * hk
    o_lr = o_v + gla_w
    o_r = o_lr + 2 * GLA_RANK
    o_u = o_r + gla_w
    o_sq = o_u + s5_w
    o_sk = o_sq + swa_w
    outs = [(2 * hk, BF16), (gla_w, BF16), (gla_w, BF16), (s5_w, BF16), (swa_w, BF16), (2 * kv_w, BF16),
            (LANES, F32)]

    x_src = (x.reshape(n_batch * seq_len, d), ctx.reshape(n_batch * ctx_len, d))
    cvec = jnp.concatenate([c, c_ctx[None], jnp.zeros((8 - n_batch - 1, d), c.dtype)], axis=0)
    cos, sin = _rope_tables(seq_len)
    cos_tab = jnp.concatenate([jnp.ones((ctx_len, SWA_HEAD_DIM), F32), cos], axis=0)
    sin_tab = jnp.concatenate([jnp.zeros((ctx_len, SWA_HEAD_DIM), F32), sin], axis=0)
    heads = {4: (SWA_HEADS, 0, SWA_HEAD_DIM ** -0.5 * LOG2E), 5: (SWA_KV_HEADS, 1, 1.0)}
    rw_t = jnp.zeros((LANES, d), F32).at[:N_EXPERTS].set(router_w.astype(F32).T)
    rb_col = jnp.zeros((LANES, 1), F32).at[:N_EXPERTS, 0].set(router_b.astype(F32))

    mod3 = _ada_mod(cvec, ada_w, ada_b).reshape(depth * MOD_ROWS * 6, 1, d)
    w_prep = jnp.concatenate(
        [w_in[:, :, o_q:o_v], w_in[:, :, o_v:o_lr], w_in[:, :, o_r:o_u], w_in[:, :, o_u:o_sq],
         w_in[:, :, o_sq:o_sk], w_in[:, :, o_sk:], w_in[:, :, o_lr:o_r],
         jnp.zeros((depth, d, LANES - 2 * GLA_RANK), w_in.dtype)], axis=2).astype(BF16)
    w_glu16, w_out16 = s5_w_glu.astype(BF16), w_out.astype(BF16)
    n_exp, _, d_exp = exp_w_gate.shape[1:]
    wg_blk = exp_w_gate.reshape(depth, CAST_BLOCKS, n_exp * d // CAST_BLOCKS, d_exp)
    wu_blk = exp_w_up.reshape(depth, CAST_BLOCKS, n_exp * d // CAST_BLOCKS, d_exp)
    wd_blk = exp_w_down.reshape(depth, CAST_BLOCKS, n_exp * d_exp // CAST_BLOCKS, d)
    s5_mats = jax.vmap(_s5_prep)(s5_a_re, s5_a_im, s5_log_dt, s5_b_re, s5_b_im, s5_c_re, s5_c_im, s5_d)

    out = None
    for layer in range(depth):
        last = layer == depth - 1
        (qk, gv, gr, u, sq, skv, lr), (wg16, wu16) = _in_proj(
            x_src, rows, mod3, norm1[layer], w_prep, outs, heads, swa_q_norm[layer], swa_k_norm[layer],
            cos_tab, sin_tab, tpb, ctx_tiles, n_batch, layer, [wg_blk, wu_blk])

        wa = jnp.zeros((LANES, 2 * hk), F32)
        wa = wa.at[0:GLA_RANK, 0:hk].set(gla_w_alpha[layer, 0].astype(F32))
        wa = wa.at[GLA_RANK:2 * GLA_RANK, hk:].set(gla_w_alpha[layer, 1].astype(F32))
        ba = gla_b_alpha[layer].astype(F32).reshape(1, 2 * hk)
        o_f, o_b = _gla_scan(qk, gv, lr, wa, ba, n_batch, tpb, ctx_tiles)

        ys = _s5_scan(u, s5_mats, n_batch, ctx_len, layer)

        sw = _swa_attend(sq, skv, swa_sink[layer], n_batch, seq_len, ctx_len)

        x_new, h2, route, wd16 = _out_proj(o_f, o_b, gr, ys, sw, x_src, mod3, norm2[layer], gla_o_norm[layer],
                                           w_glu16, s5_b_glu[layer].astype(F32), w_out16, rw_t, rb_col,
                                           tpb, ctx_tiles, n_batch, layer, wd_blk)

        plan, slot_token, pos, wts = _moe_plan(route, rows)
        y_sorted = _moe_experts(plan, slot_token, h2,
                                wg16.reshape(n_exp, d, d_exp), wu16.reshape(n_exp, d, d_exp),
                                wd16.reshape(n_exp, d_exp, d))
        out = _moe_combine(pos, x_new, mod3, wts, y_sorted, n_batch, tpb, ctx_tiles, keep_ctx=not last,
                           layer=layer)
        x_src = out
    return out.reshape(n_batch, seq_len, d)
```

```python
import functools
import math

import jax
import jax.numpy as jnp
from jax import lax
from jax.experimental import pallas as pl
from jax.experimental.pallas import tpu as pltpu

F32 = jnp.float32
BF16 = jnp.bfloat16
I32 = jnp.int32

EPS = 1e-6
NEG_INF = -1e30
LOG2E = math.log2(math.e)

LANES = 128
VMEM_BYTES = 64 * 1024 * 1024

ROW_TILE = 256
MOD_ROWS = 8

GLA_HEADS = 4
GLA_DK = 64
GLA_DV = 128
GLA_RANK = 16
GLA_GATE_NORM = 16.0
GLA_CHUNK = 64

S5_GROUP = 16
S5_STATE = 64
S5_CHUNK = 8
S5_BLOCK_GROUPS = LANES // S5_GROUP

SWA_HEAD_DIM = 128
SWA_HEADS = 8
SWA_KV_HEADS = 2
SWA_GROUP = SWA_HEADS // SWA_KV_HEADS
SWA_BLOCK = 128
ROPE_PAIRS = 32
ROPE_BASE = 10000.0
GRID_W = 64

N_EXPERTS = 16
N_EXPERT_GROUPS = 4
EXPERTS_PER_GROUP = 4
MOE_TILE = 256
DMA_UNROLL = 8
GATHER_RING = 3
CAST_BLOCKS = 64


def _cparams(semantics, vmem_mb):
    return pltpu.CompilerParams(dimension_semantics=semantics,
                                vmem_limit_bytes=vmem_mb * 1024 * 1024)


def _mm(a, b):
    return jnp.dot(a.astype(BF16), b.astype(BF16), preferred_element_type=F32)


def _mm_nt(a, b):
    return lax.dot_general(a.astype(BF16), b.astype(BF16), (((1,), (1,)), ((), ())),
                           preferred_element_type=F32)


def _mm_tn(a, b):
    return lax.dot_general(a.astype(BF16), b.astype(BF16), (((0,), (0,)), ((), ())),
                           preferred_element_type=F32)


def _split(a, parts):
    out, rest = [], a
    for p in range(parts):
        piece = rest.astype(BF16)
        out.append(piece)
        if p + 1 < parts:
            rest = rest - piece.astype(F32)
    return out


def _mm_split(a, b, dims=(((1,), (0,)), ((), ())), parts=3, a_exact=False, b_exact=False):
    pa = [a.astype(BF16)] if a_exact else _split(a, parts)
    pb = [b.astype(BF16)] if b_exact else _split(b, parts)
    out = None
    for i, ai in enumerate(pa):
        for j, bj in enumerate(pb):
            if i + j < parts:
                t = lax.dot_general(ai, bj, dims, preferred_element_type=F32)
                out = t if out is None else out + t
    return out


def _sigmoid(x):
    return 1.0 / (1.0 + jnp.exp(-x))


def _silu(x):
    return x * _sigmoid(x)


def _log_sigmoid(x):
    return jnp.minimum(x, 0.0) - jnp.log(1.0 + jnp.exp(-jnp.abs(x)))


def _gelu_tanh(x):
    return 0.5 * x * (1.0 + jnp.tanh(math.sqrt(2.0 / math.pi) * (x + 0.044715 * (x * x * x))))


def _rms(x, gain):
    return x * lax.rsqrt(jnp.mean(x * x, axis=-1, keepdims=True) + EPS) * gain


def _ada_body(c_ref, w_ref, b_ref, o_ref):
    s = _silu(c_ref[...])
    o_ref[...] = _mm(s, w_ref[...]) + b_ref[...]


def _ada_mod(cvec, w, b):
    rows, d = cvec.shape
    depth, _, n = w.shape
    tn = 1024
    return pl.pallas_call(
        _ada_body,
        grid=(depth, n // tn),
        in_specs=[pl.BlockSpec((rows, d), lambda l, i: (0, 0)),
                  pl.BlockSpec((None, d, tn), lambda l, i: (l, 0, i)),
                  pl.BlockSpec((None, 1, tn), lambda l, i: (l, 0, i))],
        out_specs=pl.BlockSpec((None, rows, tn), lambda l, i: (l, 0, i)),
        out_shape=jax.ShapeDtypeStruct((depth, rows, n), F32),
        compiler_params=_cparams(("arbitrary", "arbitrary"), 40),
        name="ada_mod",
    )(cvec, w, b.reshape(depth, 1, n))


def _mod_row(i, tiles_per_batch, ctx_tiles, n_batch, layer):
    return layer * MOD_ROWS + jnp.where(i % tiles_per_batch < ctx_tiles, n_batch, i // tiles_per_batch)


def _cast_specs(casts, layer, n_steps):
    in_specs, out_specs, out_shapes = [], [], []
    for a in casts:
        nblk, r, c = a.shape[1:]
        assert n_steps >= nblk
        in_specs.append(pl.BlockSpec((None, None, r, c),
                                     lambda i, nblk=nblk: (layer, jnp.minimum(i, nblk - 1), 0, 0)))
        out_specs.append(pl.BlockSpec((None, r, c), lambda i, nblk=nblk: (jnp.minimum(i, nblk - 1), 0, 0)))
        out_shapes.append(jax.ShapeDtypeStruct((nblk, r, c), BF16))
    return in_specs, out_specs, out_shapes


def _cast_blocks(src_refs, dst_refs):
    for src, dst in zip(src_refs, dst_refs):
        dst[...] = src[...].astype(dst.dtype)


def _proj_body(*refs, splits, n_cast, two_sources, ctx_tiles, tiles_per_batch, heads):
    refs = list(refs)
    x_ref = refs.pop(0)
    ctx_ref = refs.pop(0) if two_sources else None
    sh_ref, sc_ref, g_ref, w_ref, qg_ref, kg_ref, cos_ref, sin_ref = refs[:8]
    refs = refs[8:]
    cast_in, out_refs, cast_out = refs[:n_cast], refs[n_cast:len(refs) - n_cast], refs[len(refs) - n_cast:]
    x = x_ref[...]
    if two_sources:
        x = jnp.where(pl.program_id(0) % tiles_per_batch < ctx_tiles, ctx_ref[...], x)
    y = _rms(x, g_ref[...])
    h = (y * (1.0 + sc_ref[0]) + sh_ref[0]).astype(BF16)
    ones_hd, swap_halves = _head_mats()
    cos, sin = cos_ref[...], sin_ref[...]
    hd = SWA_HEAD_DIM
    for k, (o_ref, (c0, width)) in enumerate(zip(out_refs, splits)):
        acc = jnp.dot(h, w_ref[:, c0:c0 + width], preferred_element_type=F32)
        if k in heads:
            n_heads, gain_ref, scale = heads[k][0], (qg_ref, kg_ref)[heads[k][1]], heads[k][2]
            for hh in range(n_heads):
                cols = slice(hh * hd, (hh + 1) * hd)
                rot = _head_rms_rope(acc[:, cols], gain_ref[...], cos, sin, ones_hd, swap_halves)
                o_ref[:, cols] = (rot * scale).astype(o_ref.dtype)
            if n_heads * hd < width:
                o_ref[:, n_heads * hd:] = acc[:, n_heads * hd:].astype(o_ref.dtype)
        else:
            o_ref[...] = acc.astype(o_ref.dtype)
    _cast_blocks(cast_in, cast_out)


def _row_sources(x_src, tiles_per_batch, ctx_tiles):
    tm = ROW_TILE
    if not isinstance(x_src, tuple):
        return [x_src], [pl.BlockSpec((tm, x_src.shape[1]), lambda i: (i, 0))]
    lat, ctxa = x_src
    lat_tiles = tiles_per_batch - ctx_tiles

    def lat_map(i):
        return ((i // tiles_per_batch) * lat_tiles + jnp.maximum(i % tiles_per_batch - ctx_tiles, 0), 0)

    def ctx_map(i):
        return ((i // tiles_per_batch) * ctx_tiles + jnp.minimum(i % tiles_per_batch, ctx_tiles - 1), 0)
    return [lat, ctxa], [pl.BlockSpec((tm, lat.shape[1]), lat_map), pl.BlockSpec((tm, ctxa.shape[1]), ctx_map)]


def _in_proj(x_src, rows, mod3, gain, w, outs, heads, q_gain, k_gain, cos_tab, sin_tab,
             tiles_per_batch, ctx_tiles, n_batch, layer, casts):
    d = w.shape[1]
    tm = ROW_TILE
    hd = SWA_HEAD_DIM
    mrow = functools.partial(_mod_row, tiles_per_batch=tiles_per_batch, ctx_tiles=ctx_tiles,
                             n_batch=n_batch, layer=layer)
    splits, out_specs, out_shapes = [], [], []
    c0 = 0
    for width, dtype in outs:
        splits.append((c0, width))
        out_specs.append(pl.BlockSpec((tm, width), lambda i: (i, 0)))
        out_shapes.append(jax.ShapeDtypeStruct((rows, width), dtype))
        c0 += width
    cast_in, cast_out, cast_shapes = _cast_specs(casts, layer, rows // tm)
    x_args, x_specs = _row_sources(x_src, tiles_per_batch, ctx_tiles)
    tab = pl.BlockSpec((tm, hd), lambda i: (i % tiles_per_batch, 0))
    res = pl.pallas_call(
        functools.partial(_proj_body, splits=tuple(splits), n_cast=len(casts), two_sources=len(x_args) == 2,
                          ctx_tiles=ctx_tiles, tiles_per_batch=tiles_per_batch, heads=heads),
        grid=(rows // tm,),
        in_specs=x_specs + [
                  pl.BlockSpec((1, 1, d), lambda i: (mrow(i) * 6 + 0, 0, 0)),
                  pl.BlockSpec((1, 1, d), lambda i: (mrow(i) * 6 + 1, 0, 0)),
                  pl.BlockSpec((1, d), lambda i: (0, 0)),
                  pl.BlockSpec((None,) + w.shape[1:], lambda i: (layer, 0, 0),
                               pipeline_mode=pl.Buffered(1)),
                  pl.BlockSpec((1, hd), lambda i: (0, 0)), pl.BlockSpec((1, hd), lambda i: (0, 0)),
                  tab, tab] + cast_in,
        out_specs=out_specs + cast_out,
        out_shape=out_shapes + cast_shapes,
        compiler_params=_cparams(("arbitrary",), 56),
        name="in_proj",
    )(*x_args, mod3, mod3, gain.reshape(1, d), w, q_gain.reshape(1, hd).astype(F32),
      k_gain.reshape(1, hd).astype(F32), cos_tab, sin_tab, *casts)
    return res[:len(outs)], res[len(outs):]


def _gla_body(qkf, vf, lrf, qkb, vb, lrb, wa_ref, ba_ref, of_ref, ob_ref, sf_ref, sb_ref):
    t = pl.program_id(1)
    ck = GLA_CHUNK
    hk = GLA_HEADS * GLA_DK

    @pl.when(t == 0)
    def _init():
        sf_ref[...] = jnp.zeros_like(sf_ref)
        sb_ref[...] = jnp.zeros_like(sb_ref)

    tmr = qkf.shape[0]
    n_chunks = tmr // ck
    rowb = lax.broadcasted_iota(I32, (tmr, tmr), 0)
    colb = lax.broadcasted_iota(I32, (tmr, tmr), 1)
    same_chunk = rowb // ck == colb // ck
    row4 = lax.broadcasted_iota(I32, (GLA_HEADS * ck, ck), 0) % ck
    col4 = lax.broadcasted_iota(I32, (GLA_HEADS * ck, ck), 1)
    lane = lax.broadcasted_iota(I32, (1, hk), 1)
    head_masks = [(lane // GLA_DK == h).astype(F32) for h in range(GLA_HEADS)]
    sel = (lax.broadcasted_iota(I32, (tmr, n_chunks * GLA_DV), 0) // ck
           == lax.broadcasted_iota(I32, (tmr, n_chunks * GLA_DV), 1) // GLA_DV).astype(F32)

    def run(qk_ref, v_ref, lr_ref, w, bias, s_ref, o_ref, reverse):
        tri = (same_chunk & (rowb <= colb if reverse else rowb >= colb)).astype(F32)
        amask = row4 <= col4 if reverse else row4 >= col4
        la = _log_sigmoid(_mm_split(lr_ref[...], w, parts=2) + bias) * (1.0 / GLA_GATE_NORM)
        b = _mm_split(tri, la, parts=2, a_exact=True)
        decay_all = jnp.exp(_mm_split(la, sel, dims=(((0,), (0,)), ((), ())), parts=2, b_exact=True))
        q = qk_ref[:, 0:hk].astype(F32)
        k = qk_ref[:, hk:2 * hk].astype(F32)
        qd_all = q * (GLA_DK ** -0.5) * jnp.exp(b)
        kd_all = k * jnp.exp(-b)
        state = s_ref[...]
        order = range(n_chunks - 1, -1, -1) if reverse else range(n_chunks)
        for ci in order:
            rows = slice(ci * ck, (ci + 1) * ck)
            v = v_ref[rows, :]
            last = ci * ck if reverse else (ci + 1) * ck - 1
            ke = k[rows, :] * jnp.exp(b[last:last + 1, :] - b[rows, :])
            qd = qd_all[rows, :]
            qs = jnp.concatenate([qd * m for m in head_masks], axis=0).astype(BF16)
            att = jnp.where(amask, _mm_nt(qs, kd_all[rows, :]), 0.0)
            inter = _mm(qs, state)
            outs = []
            for h in range(GLA_HEADS):
                hr = slice(h * ck, (h + 1) * ck)
                outs.append(_mm(att[hr, :], v[:, h * GLA_DV:(h + 1) * GLA_DV]) + inter[hr, :])
            o_ref[rows, :] = jnp.concatenate(outs, axis=1)
            ds = _mm_tn(ke, v)
            ds = jnp.concatenate([ds[h * GLA_DK:(h + 1) * GLA_DK, h * GLA_DV:(h + 1) * GLA_DV]
                                  for h in range(GLA_HEADS)], axis=0)
            state = decay_all[:, ci * GLA_DV:(ci + 1) * GLA_DV] * state + ds
        s_ref[...] = state

    run(qkf, vf, lrf, wa_ref[:, 0:hk], ba_ref[:, 0:hk], sf_ref, of_ref, False)
    run(qkb, vb, lrb, wa_ref[:, hk:2 * hk], ba_ref[:, hk:2 * hk], sb_ref, ob_ref, True)


def _gla_scan(qk, v, lr, wa, ba, n_batch, tiles_per_batch, ctx_tiles):
    rows = qk.shape[0]
    tm = ROW_TILE
    tpb = tiles_per_batch

    def fwd(b, t):
        return (b * tpb + t, 0)

    def bwd(b, t):
        return (b * tpb + jnp.where(t < ctx_tiles, ctx_tiles - 1 - t, tpb - 1 - (t - ctx_tiles)), 0)

    wv = v.shape[1]
    return pl.pallas_call(
        _gla_body,
        grid=(n_batch, tpb),
        in_specs=[pl.BlockSpec((tm, qk.shape[1]), fwd), pl.BlockSpec((tm, wv), fwd),
                  pl.BlockSpec((tm, lr.shape[1]), fwd),
                  pl.BlockSpec((tm, qk.shape[1]), bwd), pl.BlockSpec((tm, wv), bwd),
                  pl.BlockSpec((tm, lr.shape[1]), bwd),
                  pl.BlockSpec(wa.shape, lambda b, t: (0, 0)),
                  pl.BlockSpec(ba.shape, lambda b, t: (0, 0))],
        out_specs=[pl.BlockSpec((tm, wv), fwd), pl.BlockSpec((tm, wv), bwd)],
        out_shape=[jax.ShapeDtypeStruct((rows, wv), F32)] * 2,
        scratch_shapes=[pltpu.VMEM((GLA_HEADS * GLA_DK, GLA_DV), F32)] * 2,
        compiler_params=_cparams(("arbitrary", "arbitrary"), 32),
        name="gla_scan",
    )(qk, v, lr, qk, v, lr, wa, ba)


def _s5_prep(a_re, a_im, log_dt, b_re, b_im, c_re, c_im, d):
    tt = S5_CHUNK
    n_groups = a_re.shape[1]
    bg = S5_BLOCK_GROUPS
    nb = n_groups // bg
    a_re, a_im, b_re, b_im = a_re.astype(F32), a_im.astype(F32), b_re.astype(F32), b_im.astype(F32)
    c_re, c_im = c_re.astype(F32), c_im.astype(F32)
    dt = jnp.exp(log_dt.astype(F32))[..., None]
    mag = jnp.exp(dt * a_re)
    ab_re, ab_im = mag * jnp.cos(dt * a_im), mag * jnp.sin(dt * a_im)
    den = a_re * a_re + a_im * a_im
    nr = ab_re - 1.0
    coef_re = (nr * a_re + ab_im * a_im) / den
    coef_im = (ab_im * a_re - nr * a_im) / den
    bb_re = coef_re[..., None] * b_re - coef_im[..., None] * b_im
    bb_im = coef_re[..., None] * b_im + coef_im[..., None] * b_re
    pr, pi = [jnp.ones_like(ab_re)], [jnp.zeros_like(ab_im)]
    for _ in range(tt):
        pr, pi = pr + [pr[-1] * ab_re - pi[-1] * ab_im], pi + [pr[-1] * ab_im + pi[-1] * ab_re]
    pr, pi = jnp.stack(pr), jnp.stack(pi)
    cpr = c_re[None] * pr[:, :, :, None, :] - c_im[None] * pi[:, :, :, None, :]
    cpi = c_re[None] * pi[:, :, :, None, :] + c_im[None] * pr[:, :, :, None, :]
    kern = (jnp.einsum('kdgcn,dgnm->kdgcm', cpr, bb_re)
            - jnp.einsum('kdgcn,dgnm->kdgcm', cpi, bb_im))
    kdim, half = tt * LANES, bg * S5_STATE
    ax_k, ax_s, ax_tc = jnp.arange(kdim), jnp.arange(half), jnp.arange(tt * S5_GROUP)
    k_group, k_step, k_chan = (ax_k // S5_GROUP) % bg, ax_k // LANES, ax_k % S5_GROUP
    rep_tc = ((ax_tc[:, None] // S5_GROUP == k_step[None, :])
              & (ax_tc[:, None] % S5_GROUP == k_chan[None, :])).astype(F32)
    rep_n = (jnp.arange(S5_STATE)[:, None] == ax_s[None, :] % S5_STATE).astype(F32)
    mask_kk = k_group[:, None] == k_group[None, :]
    mask_ks = k_group[:, None] == ax_s[None, :] // S5_STATE
    expand = lambda table, rep: jnp.einsum('jrk,kc->jrc', table, rep)

    s_idx = jnp.arange(tt)[:, None]
    t_idx = jnp.arange(tt)[None, :]
    lag_f = jnp.clip(t_idx - s_idx, 0, tt)
    lag_b = jnp.clip(s_idx - t_idx, 0, tt)
    toe = (jnp.where((t_idx >= s_idx)[:, :, None, None, None], kern[lag_f, 0], 0.0)
           + jnp.where((s_idx >= t_idx)[:, :, None, None, None], kern[lag_b, 1], 0.0))
    skip = (jnp.eye(tt, dtype=F32)[:, :, None, None, None]
            * d.astype(F32).reshape(n_groups, S5_GROUP)[None, None, :, :, None]
            * jnp.eye(S5_GROUP, dtype=F32)[None, None, None])
    toe = (toe + skip).reshape(tt, tt, nb, bg, S5_GROUP, S5_GROUP)
    toe = toe.transpose(2, 0, 3, 5, 1, 4).reshape(nb, kdim, tt * S5_GROUP)
    m_mat = jnp.where(mask_kk, expand(toe, rep_tc), 0.0)

    def inject(direction, power_of_s):
        p_re, p_im = pr[power_of_s, direction], pi[power_of_s, direction]
        v_re = p_re[..., None] * bb_re[direction][None] - p_im[..., None] * bb_im[direction][None]
        v_im = p_re[..., None] * bb_im[direction][None] + p_im[..., None] * bb_re[direction][None]
        def blk(v):
            v = v.reshape(tt, nb, bg, S5_STATE, S5_GROUP).transpose(1, 0, 2, 4, 3).reshape(nb, kdim, S5_STATE)
            return jnp.where(mask_ks, expand(v, rep_n), 0.0)
        return jnp.concatenate([blk(v_re), blk(v_im)], axis=-1)

    def readout(direction, power_of_t):
        e_re, e_im = cpr[power_of_t, direction], -cpi[power_of_t, direction]
        def blk(v):
            v = v.reshape(tt, nb, bg, S5_GROUP, S5_STATE).transpose(1, 2, 4, 0, 3).reshape(nb, half, tt * S5_GROUP)
            return jnp.where(mask_ks.T, expand(v, rep_tc), 0.0)
        return jnp.concatenate([blk(e_re), blk(e_im)], axis=1)

    steps = jnp.arange(tt)
    g_f = inject(0, tt - 1 - steps)
    g_b = inject(1, steps)
    e_f = readout(0, steps + 1)
    e_b = readout(1, tt - steps)
    a_t = jnp.stack([pr[tt, 0].reshape(nb, half), pi[tt, 0].reshape(nb, half),
                     pr[tt, 1].reshape(nb, half), pi[tt, 1].reshape(nb, half)], axis=1)
    cast = lambda z: z.astype(BF16)
    return cast(m_mat), cast(g_f), cast(g_b), cast(e_f), cast(e_b), a_t


def _s5_body(u_ref, m_ref, gf_ref, gb_ref, ef_ref, eb_ref, a_ref, y_ref, sf_ref, sb_ref,
             *, ctx_rows, n_rows, n_batch):
    half = S5_BLOCK_GROUPS * S5_STATE
    row_blocks = [slice(r * n_rows, (r + 1) * n_rows) for r in range(n_batch)]

    for rows in row_blocks:
        sf_ref[rows, :] = jnp.dot(u_ref[rows, :], gf_ref[0], preferred_element_type=F32)
        sb_ref[rows, :] = jnp.dot(u_ref[rows, :], gb_ref[0], preferred_element_type=F32)

    a = a_ref[0]
    afr, afi, abr, abi = a[0:1, :], a[1:2, :], a[2:3, :], a[3:4, :]

    sub = 8
    cpt = sub // n_batch
    n_tiles, ctx_tiles = n_rows // cpt, ctx_rows // cpt

    def advance(a_re, a_im, s_re, s_im, g):
        return a_re * s_re - a_im * s_im + g[:, 0:half], a_re * s_im + a_im * s_re + g[:, half:2 * half]

    def step(k, carry):
        fr, fi, br, bi = carry
        rf = pl.ds(pl.multiple_of(k * sub, sub), sub)
        g = sf_ref[rf, :]
        ins_r, ins_i = [], []
        for c in range(cpt):
            ins_r.append(fr)
            ins_i.append(fi)
            fr, fi = advance(afr, afi, fr, fi, g[c * n_batch:(c + 1) * n_batch, :])
        sf_ref[rf, 0:half] = jnp.concatenate(ins_r, axis=0)
        sf_ref[rf, half:2 * half] = jnp.concatenate(ins_i, axis=0)
        kb = jnp.where(k < ctx_tiles, ctx_tiles - 1 - k, n_tiles - 1 - (k - ctx_tiles))
        rb = pl.ds(pl.multiple_of(kb * sub, sub), sub)
        g = sb_ref[rb, :]
        ins_r, ins_i = [None] * cpt, [None] * cpt
        for c in range(cpt - 1, -1, -1):
            ins_r[c], ins_i[c] = br, bi
            br, bi = advance(abr, abi, br, bi, g[c * n_batch:(c + 1) * n_batch, :])
        sb_ref[rb, 0:half] = jnp.concatenate(ins_r, axis=0)
        sb_ref[rb, half:2 * half] = jnp.concatenate(ins_i, axis=0)
        return fr, fi, br, bi

    zero = jnp.zeros((n_batch, half), F32)
    lax.fori_loop(0, n_tiles, step, (zero, zero, zero, zero))

    for rows in row_blocks:
        y = (jnp.dot(u_ref[rows, :], m_ref[0], preferred_element_type=F32)
             + jnp.dot(sf_ref[rows, :].astype(BF16), ef_ref[0], preferred_element_type=F32)
             + jnp.dot(sb_ref[rows, :].astype(BF16), eb_ref[0], preferred_element_type=F32))
        y_ref[rows, :] = y.astype(y_ref.dtype)


def _s5_scan(u, mats, n_batch, ctx_len, layer):
    m_mat, g_f, g_b, e_f, e_b, a_t = mats
    rows, width = u.shape
    tt = S5_CHUNK
    nb = width // LANES
    srows = rows // tt // n_batch
    kdim = tt * LANES
    sdim = 2 * S5_BLOCK_GROUPS * S5_STATE
    u5 = u.reshape(n_batch, srows, tt, nb, LANES).transpose(3, 1, 0, 2, 4).reshape(nb, srows * n_batch, kdim)
    once = dict(pipeline_mode=pl.Buffered(1))
    wspec = lambda shape: pl.BlockSpec((None, 1) + shape, lambda j: (layer, j, 0, 0), **once)
    y5 = pl.pallas_call(
        functools.partial(_s5_body, ctx_rows=ctx_len // tt, n_rows=srows, n_batch=n_batch),
        grid=(nb,),
        in_specs=[pl.BlockSpec((None, srows * n_batch, kdim), lambda j: (j, 0, 0), **once),
                  wspec((kdim, kdim)), wspec((kdim, sdim)), wspec((kdim, sdim)),
                  wspec((sdim, kdim)), wspec((sdim, kdim)), wspec((4, sdim // 2))],
        out_specs=pl.BlockSpec((None, srows * n_batch, kdim), lambda j: (j, 0, 0)),
        out_shape=jax.ShapeDtypeStruct((nb, srows * n_batch, kdim), BF16),
        scratch_shapes=[pltpu.VMEM((srows * n_batch, sdim), F32), pltpu.VMEM((srows * n_batch, sdim), F32)],
        compiler_params=_cparams(("arbitrary",), 56),
        name="s5_scan",
    )(u5, m_mat, g_f, g_b, e_f, e_b, a_t)
    return y5.reshape(nb, srows, n_batch, tt, LANES).transpose(2, 1, 3, 0, 4).reshape(rows, width)


def _rope_tables(seq_len):
    pos = jnp.arange(seq_len)
    row = (pos // GRID_W).astype(F32)
    col = (pos % GRID_W).astype(F32)
    inv_freq = ROPE_BASE ** (-jnp.arange(ROPE_PAIRS, dtype=F32) / ROPE_PAIRS)
    ar, ac = row[:, None] * inv_freq, col[:, None] * inv_freq
    cos = jnp.concatenate([jnp.cos(ar), jnp.cos(ar), jnp.cos(ac), jnp.cos(ac)], axis=1)
    sin = jnp.concatenate([-jnp.sin(ar), jnp.sin(ar), -jnp.sin(ac), jnp.sin(ac)], axis=1)
    return cos, sin


def _head_mats():
    hd = SWA_HEAD_DIM
    src = lax.broadcasted_iota(I32, (hd, hd), 0)
    dst = lax.broadcasted_iota(I32, (hd, hd), 1)
    partner = jnp.where(dst % (2 * ROPE_PAIRS) < ROPE_PAIRS, dst + ROPE_PAIRS, dst - ROPE_PAIRS)
    return jnp.ones((hd, hd), BF16), (src == partner).astype(BF16)


def _head_rms_rope(x, gain, cos, sin, ones_hd, swap_halves):
    ss = _mm(x * x, ones_hd)
    xn = x * lax.rsqrt(ss * (1.0 / SWA_HEAD_DIM) + EPS) * gain
    return xn * cos + _mm(xn, swap_halves) * sin


def _swa_body(sink_ref, q_ref, kp_ref, kc_ref, kn_ref, kx_ref, o_ref, *, lat_blocks):
    n = pl.program_id(1)
    hd = SWA_HEAD_DIM
    blk = SWA_BLOCK
    kvw = SWA_KV_HEADS * hd
    row = lax.broadcasted_iota(I32, (SWA_GROUP * blk, blk), 0) % blk
    col = lax.broadcasted_iota(I32, (SWA_GROUP * blk, blk), 1)
    ones_hd = jnp.ones((hd, hd), BF16)

    def lane_fold(op, blocks):
        parts = [b[:, c0:c0 + blk] for b in blocks for c0 in range(0, b.shape[1], blk)]
        out = parts[0]
        for part in parts[1:]:
            out = op(out, part)
        return out

    def attend(is_lat):
        for g in range(SWA_KV_HEADS):
            kcols = slice(g * hd, (g + 1) * hd)
            vcols = slice(kvw + g * hd, kvw + (g + 1) * hd)
            keys = [kx_ref[:, kcols]]
            values = [kx_ref[:, vcols]]
            valid = [None]
            if is_lat:
                wins = (kp_ref, kc_ref, kn_ref)
                keys += [r[:, kcols] for r in wins]
                values += [r[:, vcols] for r in wins]
                valid += [(col >= row) & (col + (n - 1) * blk >= 0), None,
                          (col <= row) & (col + (n + 1) * blk < lat_blocks * blk)]
            heads = range(g * SWA_GROUP, (g + 1) * SWA_GROUP)
            q4 = jnp.concatenate([q_ref[:, h * hd:(h + 1) * hd] for h in heads], axis=0)
            sink = jnp.concatenate([jnp.full((blk, 1), sink_ref[h] * LOG2E, F32) for h in heads], axis=0)
            scores = []
            for kk, ok in zip(keys, valid):
                s = _mm_nt(q4, kk)
                scores.append(s if ok is None else jnp.where(ok, s, NEG_INF))
            m = jnp.maximum(jnp.max(lane_fold(jnp.maximum, scores), axis=-1, keepdims=True), sink)
            probs = [jnp.exp2(s - m).astype(BF16) for s in scores]
            den = jnp.exp2(sink - m)
            for p in probs:
                for c0 in range(0, p.shape[1], blk):
                    den = den + _mm(p[:, c0:c0 + blk], ones_hd)
            acc = None
            for p, vv in zip(probs, values):
                pv = _mm(p, vv)
                acc = pv if acc is None else acc + pv
            out = acc * (1.0 / den)
            for hh, h in enumerate(heads):
                o_ref[:, h * hd:(h + 1) * hd] = out[hh * blk:(hh + 1) * blk, :].astype(o_ref.dtype)

    @pl.when(n < lat_blocks)
    def _lat():
        attend(True)

    @pl.when(n >= lat_blocks)
    def _ctx():
        attend(False)


def _swa_attend(q, kv, sink, n_batch, seq_len, ctx_len):
    rows = q.shape[0]
    blk = SWA_BLOCK
    lat_blocks = seq_len // blk
    ctx_blocks = ctx_len // blk
    bpb = lat_blocks + ctx_blocks

    def q_map(b, n, s):
        return (b * bpb + jnp.where(n < lat_blocks, ctx_blocks + n, n - lat_blocks), 0)

    def win(off):
        return lambda b, n, s: (b * bpb + ctx_blocks + jnp.clip(n + off, 0, lat_blocks - 1), 0)

    kvs = lambda m: pl.BlockSpec((blk, kv.shape[1]), m)
    grid_spec = pltpu.PrefetchScalarGridSpec(
        num_scalar_prefetch=1,
        grid=(n_batch, bpb),
        in_specs=[pl.BlockSpec((blk, q.shape[1]), q_map),
                  kvs(win(-1)), kvs(win(0)), kvs(win(1)),
                  pl.BlockSpec((ctx_len, kv.shape[1]), lambda b, n, s: (b * (bpb * blk // ctx_len), 0))],
        out_specs=pl.BlockSpec((blk, q.shape[1]), q_map),
    )
    return pl.pallas_call(
        functools.partial(_swa_body, lat_blocks=lat_blocks),
        grid_spec=grid_spec,
        out_shape=jax.ShapeDtypeStruct((rows, q.shape[1]), BF16),
        compiler_params=_cparams(("arbitrary", "arbitrary"), 32),
        name="swa_attend",
    )(sink.astype(F32), q, kv, kv, kv, kv)


def _route_rows(logits_t, bias_col):
    aff = _sigmoid(logits_t)
    biased = aff + bias_col
    v = [biased[e:e + 1, :] for e in range(N_EXPERTS)]
    a = [aff[e:e + 1, :] for e in range(N_EXPERTS)]
    gsz = EXPERTS_PER_GROUP
    best_g = best_s = None
    for g in range(N_EXPERT_GROUPS):
        vg = v[g * gsz:(g + 1) * gsz]
        score = None
        for i in range(gsz):
            for k in range(i + 1, gsz):
                pair = vg[i] + vg[k]
                score = pair if score is None else jnp.maximum(score, pair)
        if best_g is None:
            best_g, best_s = jnp.zeros_like(score, dtype=I32), score
        else:
            upd = score > best_s
            best_g = jnp.where(upd, g, best_g)
            best_s = jnp.where(upd, score, best_s)

    def pick(rows):
        out = []
        for i in range(gsz):
            x = rows[i]
            for g in range(1, N_EXPERT_GROUPS):
                x = jnp.where(best_g == g, rows[g * gsz + i], x)
            out.append(x)
        return out

    vb, ab = pick(v), pick(a)
    i1, v1, a1 = jnp.zeros_like(best_g), vb[0], ab[0]
    for i in range(1, gsz):
        upd = vb[i] > v1
        i1, v1, a1 = jnp.where(upd, i, i1), jnp.where(upd, vb[i], v1), jnp.where(upd, ab[i], a1)
    i2 = v2 = a2 = None
    for i in range(gsz):
        cand = jnp.where(i1 == i, -jnp.inf, vb[i])
        if i2 is None:
            i2, v2, a2 = jnp.zeros_like(best_g), cand, ab[0]
        else:
            upd = cand > v2
            i2, v2, a2 = jnp.where(upd, i, i2), jnp.where(upd, cand, v2), jnp.where(upd, ab[i], a2)
    tot = a1 + a2
    e1 = (best_g * gsz + i1).astype(F32)
    e2 = (best_g * gsz + i2).astype(F32)
    zeros = jnp.zeros((4, e1.shape[1]), F32)
    return jnp.concatenate([e1, e2, a1 / tot, a2 / tot, zeros], axis=0)


def _out_body(of_ref, ob_ref, r_ref, ys_ref, sw_ref, *refs, two_sources, ctx_tiles, tiles_per_batch):
    refs = list(refs)
    x_ref = refs.pop(0)
    ctx_ref = refs.pop(0) if two_sources else None
    (g1_ref, sh_ref, sc_ref, n2_ref, og_ref, wglu_ref, bglu_ref, wo_ref, rw_ref, rb_ref, cast_in,
     xo_ref, h_ref, rt_ref, cast_out) = refs
    _cast_blocks([cast_in], [cast_out])
    x_res = x_ref[...]
    if two_sources:
        x_res = jnp.where(pl.program_id(0) % tiles_per_batch < ctx_tiles, ctx_ref[...], x_res)
    o = of_ref[...] + ob_ref[...]
    r = r_ref[...].astype(F32)
    gain = og_ref[...]
    heads = []
    for h in range(GLA_HEADS):
        cols = slice(h * GLA_DV, (h + 1) * GLA_DV)
        heads.append(_rms(o[:, cols], gain[:, cols]))
    gla_y = jnp.concatenate(heads, axis=1) * _silu(r)
    z = _gelu_tanh(ys_ref[...].astype(F32))
    s5_y = z * _sigmoid(_mm(z, wglu_ref[...]) + bglu_ref[...])
    mixed = jnp.concatenate([gla_y.astype(BF16), s5_y.astype(BF16), sw_ref[...]], axis=1)
    y = jnp.dot(mixed, wo_ref[...], preferred_element_type=F32)
    x_new = x_res + g1_ref[0] * y
    xo_ref[...] = x_new
    h2 = _rms(x_new, n2_ref[...]) * (1.0 + sc_ref[0]) + sh_ref[0]
    h_ref[...] = h2
    logits_t = _mm_split(rw_ref[...], h2, dims=(((1,), (1,)), ((), ())), parts=2)
    rt_ref[0] = _route_rows(logits_t, rb_ref[...])


def _out_proj(o_f, o_b, r, ys, sw, x_src, mod3, norm2, o_gain, w_glu, b_glu, w_out, rw_t, rb_col,
              tiles_per_batch, ctx_tiles, n_batch, layer, cast):
    rows, d = o_f.shape[0], w_out.shape[2]
    tm = ROW_TILE
    nt = rows // tm
    mrow = functools.partial(_mod_row, tiles_per_batch=tiles_per_batch, ctx_tiles=ctx_tiles,
                             n_batch=n_batch, layer=layer)
    x_args, x_specs = _row_sources(x_src, tiles_per_batch, ctx_tiles)
    rowspec = lambda a: pl.BlockSpec((tm, a.shape[1]), lambda i: (i, 0))
    const = lambda a: pl.BlockSpec(a.shape, lambda i: (0,) * a.ndim)
    layered = lambda a, **kw: pl.BlockSpec((None,) + a.shape[1:], lambda i: (layer,) + (0,) * (a.ndim - 1), **kw)
    modspec = lambda k: pl.BlockSpec((1, 1, d), lambda i: (mrow(i) * 6 + k, 0, 0))
    consts = [norm2.reshape(1, d), o_gain.reshape(1, -1)]
    cast_in, cast_out, cast_shapes = _cast_specs([cast], layer, nt)
    return pl.pallas_call(
        functools.partial(_out_body, two_sources=len(x_args) == 2, ctx_tiles=ctx_tiles,
                          tiles_per_batch=tiles_per_batch),
        grid=(nt,),
        in_specs=[rowspec(o_f), rowspec(o_b), rowspec(r), rowspec(ys), rowspec(sw)] + x_specs
                 + [modspec(2), modspec(3), modspec(4)]
                 + [const(a) for a in consts]
                 + [layered(w_glu), const(b_glu.reshape(1, -1)),
                    layered(w_out, pipeline_mode=pl.Buffered(1)), const(rw_t), const(rb_col)] + cast_in,
        out_specs=[pl.BlockSpec((tm, d), lambda i: (i, 0)), pl.BlockSpec((tm, d), lambda i: (i, 0)),
                   pl.BlockSpec((1, 8, tm), lambda i: (i, 0, 0))] + cast_out,
        out_shape=[jax.ShapeDtypeStruct((rows, d), F32), jax.ShapeDtypeStruct((rows, d), F32),
                   jax.ShapeDtypeStruct((nt, 8, tm), F32)] + cast_shapes,
        compiler_params=_cparams(("arbitrary",), 48),
        name="out_proj",
    )(o_f, o_b, r, ys, sw, *x_args, mod3, mod3, mod3, *consts, w_glu, b_glu.reshape(1, -1), w_out, rw_t, rb_col,
      cast)


def _row_copy(src_hbm, row, dst, slot, sem):
    return pltpu.make_async_copy(src_hbm.at[pl.ds(row, 1)], dst.at[pl.ds(slot, 1)], sem)


def _gather_start(src_hbm, idx_ref, dst, sem, n_rows, unrolled=False):
    if unrolled:
        for r in range(n_rows):
            _row_copy(src_hbm, idx_ref[0, 0, r], dst, r, sem).start()
        return

    def body(g, carry):
        for u in range(DMA_UNROLL):
            r = g * DMA_UNROLL + u
            _row_copy(src_hbm, idx_ref[0, 0, r], dst, r, sem).start()
        return carry
    lax.fori_loop(0, n_rows // DMA_UNROLL, body, 0)


def _gather_wait(src_hbm, dst, sem, n_rows):
    for r in range(n_rows):
        _row_copy(src_hbm, 0, dst, r, sem).wait()


def _ring_gather(step, n_steps, src_hbm, idx_refs, ring, sem, work):
    depth = GATHER_RING
    n_rows = ring.shape[1]

    @pl.when(step == 0)
    def _prime():
        for k in range(depth - 1):
            _gather_start(src_hbm, idx_refs[k], ring.at[k], sem.at[k], n_rows)

    slot = step % depth
    ahead = (step + depth - 1) % depth
    _gather_wait(src_hbm, ring.at[slot], sem.at[slot], n_rows)
    work(ring.at[slot],
         lambda unrolled: _gather_start(src_hbm, idx_refs[depth - 1], ring.at[ahead], sem.at[ahead], n_rows,
                                        unrolled))

    @pl.when(step == n_steps - 1)
    def _drain():
        for k in range(1, depth):
            late = (step + k) % depth
            _gather_wait(src_hbm, ring.at[late], sem.at[late], n_rows)


PLAN_EXPERT, PLAN_VALID, PLAN_FIRST, PLAN_SLOT, PLAN_NEXT, PLAN_HAS_NEXT = range(6)


def _moe_body(plan_ref, idx0_ref, idx1_ref, idx2_ref, x_hbm, wg_hbm, wu_hbm, wd_hbm, o_ref,
              ring, sem, wg_buf, wu_buf, wd_buf, wsem, *, n_steps):
    i = pl.program_id(0)
    expert = plan_ref[PLAN_EXPERT, i]
    wslot = plan_ref[PLAN_SLOT, i]

    def weight_copies(e, slot):
        pairs = ((wg_hbm, wg_buf), (wu_hbm, wu_buf), (wd_hbm, wd_buf))
        return [pltpu.make_async_copy(src.at[e], dst.at[slot], wsem.at[slot]) for src, dst in pairs]

    @pl.when(i == 0)
    def _first_weights():
        for c in weight_copies(expert, 0):
            c.start()

    @pl.when(plan_ref[PLAN_FIRST, i] > 0)
    def _switch():
        for c in weight_copies(expert, wslot):
            c.wait()

        @pl.when(plan_ref[PLAN_HAS_NEXT, i] > 0)
        def _prefetch():
            for c in weight_copies(plan_ref[PLAN_NEXT, i], 1 - wslot):
                c.start()

    def work(cur, start_next):
        @pl.when(plan_ref[PLAN_VALID, i] > 0)
        def _active():
            start_next(True)
            x = cur[...].astype(BF16)
            gate = jnp.dot(x, wg_buf[wslot], preferred_element_type=F32)
            up = jnp.dot(x, wu_buf[wslot], preferred_element_type=F32)
            act = (_silu(gate) * up).astype(BF16)
            o_ref[...] = jnp.dot(act, wd_buf[wslot], preferred_element_type=F32)

        @pl.when(plan_ref[PLAN_VALID, i] == 0)
        def _idle():
            start_next(False)
            o_ref[...] = jnp.zeros_like(o_ref)

    _ring_gather(i, n_steps, x_hbm, (idx0_ref, idx1_ref, idx2_ref), ring, sem, work)


def _moe_experts(plan, slot_token, tokens, w_gate, w_up, w_down):
    tm = MOE_TILE
    nt = plan.shape[1]
    d = tokens.shape[1]
    de = w_gate.shape[2]
    idx = lambda off: pl.BlockSpec((1, 1, tm), lambda i, plan: (jnp.minimum(i + off, nt - 1), 0, 0),
                                   memory_space=pltpu.SMEM)
    hbm = pl.BlockSpec(memory_space=pl.ANY)
    grid_spec = pltpu.PrefetchScalarGridSpec(
        num_scalar_prefetch=1,
        grid=(nt,),
        in_specs=[idx(k) for k in range(GATHER_RING)] + [hbm, hbm, hbm, hbm],
        out_specs=pl.BlockSpec((tm, d), lambda i, plan: (i, 0)),
        scratch_shapes=[pltpu.VMEM((GATHER_RING, tm, d), F32), pltpu.SemaphoreType.DMA((GATHER_RING,)),
                        pltpu.VMEM((2, d, de), BF16), pltpu.VMEM((2, d, de), BF16), pltpu.VMEM((2, de, d), BF16),
                        pltpu.SemaphoreType.DMA((2,))],
    )
    slots = slot_token.reshape(nt, 1, tm)
    return pl.pallas_call(
        functools.partial(_moe_body, n_steps=nt),
        grid_spec=grid_spec,
        out_shape=jax.ShapeDtypeStruct((nt * tm, d), F32),
        compiler_params=_cparams(("arbitrary",), 56),
        name="moe_experts",
    )(plan, *([slots] * GATHER_RING), tokens, w_gate, w_up, w_down)


def _combine_body(pos0_ref, pos1_ref, pos2_ref, x_ref, g2_ref, w_ref, y_hbm, o_ref, ring, sem, *, n_steps):
    tm = x_ref.shape[0]

    def work(cur, start_next):
        start_next(True)
        w = w_ref[...]
        f = w[:, 0:1] * cur[0:tm, :] + w[:, 1:2] * cur[tm:2 * tm, :]
        o_ref[...] = x_ref[...] + g2_ref[0] * f

    _ring_gather(pl.program_id(0), n_steps, y_hbm, (pos0_ref, pos1_ref, pos2_ref), ring, sem, work)


def _moe_combine(pos, x_new, mod3, wts, y_sorted, n_batch, tiles_per_batch, ctx_tiles, keep_ctx, layer):
    rows, d = x_new.shape
    tm = ROW_TILE
    tpb = tiles_per_batch
    first = 0 if keep_ctx else ctx_tiles
    out_tpb = tpb - first
    ns = n_batch * out_tpb
    in_tile = lambda s: (s // out_tpb) * tpb + first + s % out_tpb
    mrow = lambda s: layer * MOD_ROWS + jnp.where(first + s % out_tpb < ctx_tiles, n_batch, s // out_tpb)
    pos3 = pos.reshape(rows // tm, tm, 2).transpose(0, 2, 1).reshape(rows // tm, 1, 2 * tm)
    idx = lambda off: pl.BlockSpec((1, 1, 2 * tm), lambda s: (in_tile(jnp.minimum(s + off, ns - 1)), 0, 0),
                                   memory_space=pltpu.SMEM)
    return pl.pallas_call(
        functools.partial(_combine_body, n_steps=ns),
        grid=(ns,),
        in_specs=[idx(k) for k in range(GATHER_RING)] + [
                  pl.BlockSpec((tm, d), lambda s: (in_tile(s), 0)),
                  pl.BlockSpec((1, 1, d), lambda s: (mrow(s) * 6 + 5, 0, 0)),
                  pl.BlockSpec((tm, 2), lambda s: (in_tile(s), 0)),
                  pl.BlockSpec(memory_space=pl.ANY)],
        out_specs=pl.BlockSpec((tm, d), lambda s: (s, 0)),
        out_shape=jax.ShapeDtypeStruct((ns * tm, d), F32),
        scratch_shapes=[pltpu.VMEM((GATHER_RING, 2 * tm, d), F32), pltpu.SemaphoreType.DMA((GATHER_RING,))],
        compiler_params=_cparams(("arbitrary",), 40),
        name="moe_combine",
    )(*([pos3] * GATHER_RING), x_new, mod3, wts, y_sorted)


def _moe_plan(route, n_rows):
    tm = MOE_TILE
    e = route[:, 0:2, :].astype(I32).transpose(0, 2, 1).reshape(n_rows, 2)
    wts = route[:, 2:4, :].transpose(0, 2, 1).reshape(n_rows, 2)
    e_flat = e.reshape(-1)
    onehot = (e_flat[:, None] == jnp.arange(N_EXPERTS, dtype=I32)[None, :]).astype(I32)
    csum = jnp.cumsum(onehot, axis=0)
    rank = jnp.sum(onehot * csum, axis=1) - 1
    counts = csum[-1]
    tiles_e = (counts + tm - 1) // tm
    tile_end = jnp.cumsum(tiles_e)
    tile_start = tile_end - tiles_e
    pos = tile_start[e_flat] * tm + rank
    nt = (2 * n_rows) // tm + N_EXPERTS
    slot_token = jnp.zeros((nt * tm,), I32).at[pos].set(jnp.arange(2 * n_rows, dtype=I32) // 2,
                                                        unique_indices=True)
    n_active = tile_end[-1]
    tile_id = jnp.minimum(jnp.arange(nt, dtype=I32), n_active - 1)
    tile_expert = jnp.minimum(jnp.sum((tile_id[:, None] >= tile_end[None, :]).astype(I32), axis=1),
                              N_EXPERTS - 1).astype(I32)
    tile_ids = jnp.arange(nt, dtype=I32)
    tile_valid = (tile_ids < n_active).astype(I32)
    first = jnp.concatenate([jnp.ones((1,), I32), (tile_expert[1:] != tile_expert[:-1]).astype(I32)])
    wslot = (jnp.cumsum(first) - 1) % 2
    later_first = (tile_ids[None, :] > tile_ids[:, None]) & (first[None, :] > 0)
    next_first = jnp.min(jnp.where(later_first, tile_ids[None, :], nt), axis=1)
    has_next = (next_first < nt).astype(I32)
    next_expert = tile_expert[jnp.minimum(next_first, nt - 1)]
    plan = jnp.stack([tile_expert, tile_valid, first, wslot.astype(I32), next_expert, has_next]).astype(I32)
    return plan, slot_token, pos.reshape(n_rows, 2), wts


def kernel(x, c, ctx, c_ctx, ada_w, ada_b, norm1, norm2, w_in, gla_w_alpha, gla_b_alpha, gla_o_norm,
           s5_a_re, s5_a_im, s5_log_dt, s5_b_re, s5_b_im, s5_c_re, s5_c_im, s5_d, s5_w_glu, s5_b_glu,
           swa_q_norm, swa_k_norm, swa_sink, w_out, router_w, router_b, exp_w_gate, exp_w_up, exp_w_down):
    n_batch, seq_len, d = x.shape
    ctx_len = ctx.shape[1]
    depth = ada_w.shape[0]
    tm = ROW_TILE
    per_batch = ctx_len + seq_len
    assert ctx_len % tm == 0 and seq_len % tm == 0
    tpb, ctx_tiles = per_batch // tm, ctx_len // tm
    rows = n_batch * per_batch

    hk = GLA_HEADS * GLA_DK
    gla_w = GLA_HEADS * GLA_DV
    s5_w = s5_d.shape[1]
    swa_w = SWA_HEADS * SWA_HEAD_DIM
    kv_w = SWA_KV_HEADS * SWA_HEAD_DIM
    o_q, o_k, o_v = 0, hk, 2 * hk
    o_lr = o_v + gla_w
    o_r = o_lr + 2 * GLA_RANK
    o_u = o_r + gla_w
    o_sq = o_u + s5_w
    o_sk = o_sq + swa_w
    outs = [(2 * hk, BF16), (gla_w, BF16), (gla_w, BF16), (s5_w, BF16), (swa_w, BF16), (2 * kv_w, BF16),
            (LANES, F32)]

    x_src = (x.reshape(n_batch * seq_len, d), ctx.reshape(n_batch * ctx_len, d))
    cvec = jnp.concatenate([c, c_ctx[None], jnp.zeros((8 - n_batch - 1, d), c.dtype)], axis=0)
    cos, sin = _rope_tables(seq_len)
    cos_tab = jnp.concatenate([jnp.ones((ctx_len, SWA_HEAD_DIM), F32), cos], axis=0)
    sin_tab = jnp.concatenate([jnp.zeros((ctx_len, SWA_HEAD_DIM), F32), sin], axis=0)
    heads = {4: (SWA_HEADS, 0, SWA_HEAD_DIM ** -0.5 * LOG2E), 5: (SWA_KV_HEADS, 1, 1.0)}
    rw_t = jnp.zeros((LANES, d), F32).at[:N_EXPERTS].set(router_w.astype(F32).T)
    rb_col = jnp.zeros((LANES, 1), F32).at[:N_EXPERTS, 0].set(router_b.astype(F32))

    mod3 = _ada_mod(cvec, ada_w, ada_b).reshape(depth * MOD_ROWS * 6, 1, d)
    w_prep = jnp.concatenate(
        [w_in[:, :, o_q:o_v], w_in[:, :, o_v:o_lr], w_in[:, :, o_r:o_u], w_in[:, :, o_u:o_sq],
         w_in[:, :, o_sq:o_sk], w_in[:, :, o_sk:], w_in[:, :, o_lr:o_r],
         jnp.zeros((depth, d, LANES - 2 * GLA_RANK), w_in.dtype)], axis=2).astype(BF16)
    w_glu16, w_out16 = s5_w_glu.astype(BF16), w_out.astype(BF16)
    n_exp, _, d_exp = exp_w_gate.shape[1:]
    wg_blk = exp_w_gate.reshape(depth, CAST_BLOCKS, n_exp * d // CAST_BLOCKS, d_exp)
    wu_blk = exp_w_up.reshape(depth, CAST_BLOCKS, n_exp * d // CAST_BLOCKS, d_exp)
    wd_blk = exp_w_down.reshape(depth, CAST_BLOCKS, n_exp * d_exp // CAST_BLOCKS, d)
    s5_mats = jax.vmap(_s5_prep)(s5_a_re, s5_a_im, s5_log_dt, s5_b_re, s5_b_im, s5_c_re, s5_c_im, s5_d)

    out = None
    for layer in range(depth):
        last = layer == depth - 1
        (qk, gv, gr, u, sq, skv, lr), (wg16, wu16) = _in_proj(
            x_src, rows, mod3, norm1[layer], w_prep, outs, heads, swa_q_norm[layer], swa_k_norm[layer],
            cos_tab, sin_tab, tpb, ctx_tiles, n_batch, layer, [wg_blk, wu_blk])

        wa = jnp.zeros((LANES, 2 * hk), F32)
        wa = wa.at[0:GLA_RANK, 0:hk].set(gla_w_alpha[layer, 0].astype(F32))
        wa = wa.at[GLA_RANK:2 * GLA_RANK, hk:].set(gla_w_alpha[layer, 1].astype(F32))
        ba = gla_b_alpha[layer].astype(F32).reshape(1, 2 * hk)
        o_f, o_b = _gla_scan(qk, gv, lr, wa, ba, n_batch, tpb, ctx_tiles)

        ys = _s5_scan(u, s5_mats, n_batch, ctx_len, layer)

        sw = _swa_attend(sq, skv, swa_sink[layer], n_batch, seq_len, ctx_len)

        x_new, h2, route, wd16 = _out_proj(o_f, o_b, gr, ys, sw, x_src, mod3, norm2[layer], gla_o_norm[layer],
                                           w_glu16, s5_b_glu[layer].astype(F32), w_out16, rw_t, rb_col,
                                           tpb, ctx_tiles, n_batch, layer, wd_blk)

        plan, slot_token, pos, wts = _moe_plan(route, rows)
        y_sorted = _moe_experts(plan, slot_token, h2,
                                wg16.reshape(n_exp, d, d_exp), wu16.reshape(n_exp, d, d_exp),
                                wd16.reshape(n_exp, d_exp, d))
        out = _moe_combine(pos, x_new, mod3, wts, y_sorted, n_batch, tpb, ctx_tiles, keep_ctx=not last,
                           layer=layer)
        x_src = out
    return out.reshape(n_batch, seq_len, d)
```
